```python
import math
import numpy as np
import jax
import jax.numpy as jnp
from jax import lax

D_MODEL = 1024
BATCH = 8
SEQ = 4096
DEPTH = 1

MLA_HEADS = 8
MLA_NOPE = 64
MLA_ROPE = 32
MLA_V = 64
MLA_Q_LORA = 384
MLA_KV_LORA = 256
ROPE_THETA = 10000.0
Q_BLOCK = 128

DIL_HEADS = 8
DIL_HEAD_DIM = 64
DIL_PATTERNS = ((128, 1), (512, 4), (2048, 16))

REL_BUCKETS = 32
REL_MAX_EXACT = 8
REL_MAX_DIST = 1024

N_EXPERTS = 16
EC_CAPACITY_FACTOR = 2
D_FF_EXPERT = 1536

MLA_WIDTH = MLA_HEADS * MLA_V
DIL_WIDTH = DIL_HEADS * DIL_HEAD_DIM
MIX_WIDTH = MLA_WIDTH + DIL_WIDTH
IN_WIDTH = MLA_Q_LORA + MLA_KV_LORA + MLA_ROPE + 3 * DIL_WIDTH
DEEPNORM_ALPHA = (2 * DEPTH) ** 0.25
DEEPNORM_BETA = (8 * DEPTH) ** -0.25
NORM_EPS = 1e-6
NEG_INF = -1e30

kernel_name = "hybrid_mla_dilated_ec_moe_deepnorm_adaln"


def layer_norm(x, g, b):
    xf = x.astype(jnp.float32)
    mu = jnp.mean(xf, axis=-1, keepdims=True)
    var = jnp.mean(jnp.square(xf - mu), axis=-1, keepdims=True)
    y = (xf - mu) * lax.rsqrt(var + NORM_EPS) * g.astype(jnp.float32) + b.astype(jnp.float32)
    return y.astype(x.dtype)


def rms_norm(x, g):
    xf = x.astype(jnp.float32)
    y = xf * lax.rsqrt(jnp.mean(jnp.square(xf), axis=-1, keepdims=True) + NORM_EPS)
    return (y * g.astype(jnp.float32)).astype(x.dtype)


def rope(t, pos):
    r = t.shape[-1]
    inv = ROPE_THETA ** (-jnp.arange(0, r, 2, dtype=jnp.float32) / r)
    ang = pos.astype(jnp.float32)[:, None] * inv[None, :]
    cos = jnp.cos(ang)[None, :, None, :]
    sin = jnp.sin(ang)[None, :, None, :]
    tf = t.astype(jnp.float32)
    t1, t2 = tf[..., : r // 2], tf[..., r // 2:]
    return jnp.concatenate([t1 * cos - t2 * sin, t1 * sin + t2 * cos], axis=-1).astype(t.dtype)


def t5_bucket(rel):
    half = REL_BUCKETS // 2
    ret = (rel > 0).astype(np.int32) * half
    n = np.abs(rel)
    large = REL_MAX_EXACT + (np.log(np.maximum(n, 1) / REL_MAX_EXACT)
                             / np.log(REL_MAX_DIST / REL_MAX_EXACT)
                             * (half - REL_MAX_EXACT)).astype(np.int32)
    large = np.minimum(large, half - 1)
    return ret + np.where(n < REL_MAX_EXACT, n, large).astype(np.int32)


def mla_mixer(c_q, c_kv, k_rope, q_norm_g, w_uq, kv_norm_g, w_ukv):
    b, s, _ = c_q.shape
    dq = MLA_NOPE + MLA_ROPE
    pos = jnp.arange(s)
    q = (rms_norm(c_q, q_norm_g) @ w_uq).reshape(b, s, MLA_HEADS, dq)
    q = jnp.concatenate([q[..., :MLA_NOPE], rope(q[..., MLA_NOPE:], pos)], axis=-1)
    kv = (rms_norm(c_kv, kv_norm_g) @ w_ukv).reshape(b, s, MLA_HEADS, MLA_NOPE + MLA_V)
    kr = jnp.broadcast_to(rope(k_rope[:, :, None, :], pos), (b, s, MLA_HEADS, MLA_ROPE))
    k = jnp.concatenate([kv[..., :MLA_NOPE], kr], axis=-1)
    v = kv[..., MLA_NOPE:]
    scale = dq ** -0.5
    nq = s // Q_BLOCK
    qb = jnp.moveaxis(q.reshape(b, nq, Q_BLOCK, MLA_HEADS, dq), 1, 0)

    def block(q_blk):
        sc = jnp.einsum('bqhd,bkhd->bhqk', q_blk, k).astype(jnp.float32) * scale
        p = jax.nn.softmax(sc, axis=-1).astype(v.dtype)
        return jnp.einsum('bhqk,bkhd->bqhd', p, v)

    o = lax.map(block, qb)
    return jnp.moveaxis(o, 0, 1).reshape(b, s, MLA_WIDTH)


def banded_attention(q, k, v, bias, half):
    b, g, l, h, dh = q.shape
    nb = -(-l // half)
    lp = nb * half
    qp = jnp.pad(q, ((0, 0), (0, 0), (0, lp - l), (0, 0), (0, 0))).reshape(b, g, nb, half, h, dh)
    pad_kv = ((0, 0), (0, 0), (half, lp - l + half), (0, 0), (0, 0))
    kp = jnp.pad(k, pad_kv)
    vp = jnp.pad(v, pad_kv)

    def band(t):
        return jnp.concatenate(
            [t[:, :, i * half: i * half + lp].reshape(b, g, nb, half, h, dh) for i in range(3)], axis=3)

    kb, vb = band(kp), band(vp)
    rel = np.arange(3 * half)[None, :] - half - np.arange(half)[:, None]
    key_pos = np.arange(nb)[:, None] * half + np.arange(3 * half)[None, :] - half
    mask = (np.abs(rel) <= half)[None] & ((key_pos >= 0) & (key_pos < l))[:, None, :]
    sc = jnp.einsum('bgnqhd,bgnkhd->bgnhqk', qp, kb).astype(jnp.float32) * (dh ** -0.5)
    sc = sc + jnp.transpose(bias, (2, 0, 1))[None, None, None]
    sc = jnp.where(mask[None, None, :, None], sc, NEG_INF)
    m = jnp.max(sc, axis=-1, keepdims=True)
    e = jnp.exp(sc - m)
    den = jnp.sum(e, axis=-1)
    o = jnp.einsum('bgnhqk,bgnkhd->bgnqhd', e, vb.astype(jnp.float32))
    o = o / jnp.swapaxes(den, -1, -2)[..., None]
    lse = jnp.swapaxes(m[..., 0] + jnp.log(den), -1, -2)
    o = o.reshape(b, g, lp, h, dh)[:, :, :l]
    lse = lse.reshape(b, g, lp, h)[:, :, :l]
    return o, lse


def dilated_mixer(q, k, v, rel_bias):
    b, s, h, dh = q.shape
    outs, lses = [], []
    for window, dil in DIL_PATTERNS:
        half = window // 2 // dil
        l = s // dil

        def perm(t):
            return t.reshape(b, l, dil, h, dh).transpose(0, 2, 1, 3, 4)

        rel = np.arange(3 * half)[None, :] - half - np.arange(half)[:, None]
        bias = rel_bias[jnp.asarray(t5_bucket(rel * dil))].astype(jnp.float32)
        o, lse = banded_attention(perm(q), perm(k), perm(v), bias, half)
        outs.append(o.transpose(0, 2, 1, 3, 4).reshape(b, s, h, dh))
        lses.append(lse.transpose(0, 2, 1, 3).reshape(b, s, h))
    w = jax.nn.softmax(jnp.stack(lses, axis=0), axis=0)
    o = jnp.sum(w[..., None] * jnp.stack(outs, axis=0), axis=0)
    return o.reshape(b, s, h * dh).astype(q.dtype)


def expert_choice_ffn(h, w_router, w_gate, w_up, w_down):
    b, s, d = h.shape
    cap = max(1, EC_CAPACITY_FACTOR * s // N_EXPERTS)
    aff = jax.nn.softmax((h @ w_router).astype(jnp.float32), axis=-1)
    vals, idx = lax.top_k(jnp.swapaxes(aff, 1, 2), cap)
    bidx = jnp.arange(b)[:, None, None]
    xin = h[bidx, idx]
    gt = jnp.einsum('becd,edf->becf', xin, w_gate)
    up = jnp.einsum('becd,edf->becf', xin, w_up)
    y = jnp.einsum('becf,efd->becd', jax.nn.silu(gt) * up, w_down)
    y = y * vals[..., None].astype(h.dtype)
    return jnp.zeros_like(h).at[bidx, idx].add(y)


def setup_inputs(seed: int = 0) -> dict:
    key = jax.random.key(seed)
    ks = jax.random.split(key, 20)
    f32 = jnp.float32
    L, D = DEPTH, D_MODEL

    def nrm(k, shape, scale):
        return jax.random.normal(k, shape, f32) * scale

    return {
        "x": nrm(ks[0], (BATCH, SEQ, D), 1.0),
        "c": nrm(ks[1], (BATCH, D), 1.0),
        "w_ada": nrm(ks[2], (L, D, 6 * D), 0.5 * D ** -0.5),
        "b_ada": nrm(ks[3], (L, 6 * D), 0.02),
        "w_in": nrm(ks[4], (L, D, IN_WIDTH), D ** -0.5),
        "q_norm_g": 1.0 + nrm(ks[5], (L, MLA_Q_LORA), 0.02),
        "w_uq": nrm(ks[6], (L, MLA_Q_LORA, MLA_HEADS * (MLA_NOPE + MLA_ROPE)), MLA_Q_LORA ** -0.5),
        "kv_norm_g": 1.0 + nrm(ks[7], (L, MLA_KV_LORA), 0.02),
        "w_ukv": nrm(ks[8], (L, MLA_KV_LORA, MLA_HEADS * (MLA_NOPE + MLA_V)), MLA_KV_LORA ** -0.5),
        "rel_bias": nrm(ks[9], (REL_BUCKETS, DIL_HEADS), 0.1),
        "w_out": nrm(ks[10], (L, MIX_WIDTH, D), MIX_WIDTH ** -0.5 * DEEPNORM_BETA),
        "ln1_g": 1.0 + nrm(ks[11], (L, D), 0.02),
        "ln1_b": nrm(ks[12], (L, D), 0.02),
        "w_router": nrm(ks[13], (L, D, N_EXPERTS), D ** -0.5),
        "w_gate": nrm(ks[14], (L, N_EXPERTS, D, D_FF_EXPERT), D ** -0.5),
        "w_up": nrm(ks[15], (L, N_EXPERTS, D, D_FF_EXPERT), D ** -0.5),
        "w_down": nrm(ks[16], (L, N_EXPERTS, D_FF_EXPERT, D), D_FF_EXPERT ** -0.5 * DEEPNORM_BETA),
        "ln2_g": 1.0 + nrm(ks[17], (L, D), 0.02),
        "ln2_b": nrm(ks[18], (L, D), 0.02),
    }


def reference(x, c, w_ada, b_ada, w_in, q_norm_g, w_uq, kv_norm_g, w_ukv, rel_bias,
              w_out, ln1_g, ln1_b, w_router, w_gate, w_up, w_down, ln2_g, ln2_b):
    b, s, _ = x.shape
    splits = np.cumsum([MLA_Q_LORA, MLA_KV_LORA, MLA_ROPE, DIL_WIDTH, DIL_WIDTH])
    cond = jax.nn.silu(c)
    for l in range(DEPTH):
        mod = cond @ w_ada[l] + b_ada[l]
        sh1, sc1, g1, sh2, sc2, g2 = [m[:, None, :] for m in jnp.split(mod, 6, axis=-1)]

        h = x * (1.0 + sc1) + sh1
        proj = h @ w_in[l]
        c_q, c_kv, k_rope, dq, dk, dv = jnp.split(proj, splits, axis=-1)
        mla_out = mla_mixer(c_q, c_kv, k_rope, q_norm_g[l], w_uq[l], kv_norm_g[l], w_ukv[l])
        hs = (b, s, DIL_HEADS, DIL_HEAD_DIM)
        dil_out = dilated_mixer(dq.reshape(hs), dk.reshape(hs), dv.reshape(hs), rel_bias)
        mix = jnp.concatenate([mla_out, dil_out], axis=-1) @ w_out[l]
        x = layer_norm(DEEPNORM_ALPHA * x + g1 * mix, ln1_g[l], ln1_b[l])

        h = x * (1.0 + sc2) + sh2
        moe = expert_choice_ffn(h, w_router[l], w_gate[l], w_up[l], w_down[l])
        x = layer_norm(DEEPNORM_ALPHA * x + g2 * moe, ln2_g[l], ln2_b[l])
    return x
```

```python
import functools
import math

import numpy as np
import jax
import jax.numpy as jnp
from jax import lax
from jax.experimental import pallas as pl
from jax.experimental.pallas import tpu as pltpu

MLA_HEADS = 8
MLA_NOPE = 64
MLA_ROPE = 32
MLA_V = 64
MLA_Q_LORA = 384
MLA_KV_LORA = 256
ROPE_THETA = 10000.0
DIL_HEADS = 8
DIL_HEAD_DIM = 64
DIL_PATTERNS = ((128, 1), (512, 4), (2048, 16))
REL_BUCKETS = 32
REL_MAX_EXACT = 8
REL_MAX_DIST = 1024
N_EXPERTS = 16
EC_CAPACITY_FACTOR = 2
NORM_EPS = 1e-6
NEG_INF = -1e30

LANES = 128
HEAD_SLAB = 128
VMEM_LIMIT = 48 * 1024 * 1024

F32 = jnp.float32
BF16 = jnp.bfloat16
HALF_ROPE = MLA_ROPE // 2
DIL_WIDTH = DIL_HEADS * DIL_HEAD_DIM
MLA_WIDTH = MLA_HEADS * MLA_V
BAND_HALF = 64
Q_TILE_DIL = 2 * BAND_HALF
K_WIN_DIL = 4 * BAND_HALF


def _params(sem, vmem=VMEM_LIMIT):
    return pltpu.CompilerParams(dimension_semantics=sem, vmem_limit_bytes=vmem)


def _dot(a, b):
    return jnp.dot(a, b, preferred_element_type=F32)


def _dot_nt(a, b):
    return lax.dot_general(a, b, (((1,), (1,)), ((), ())), preferred_element_type=F32)


def _layer_norm(y, g, b):
    mu = jnp.mean(y, axis=-1, keepdims=True)
    d = y - mu
    var = jnp.mean(d * d, axis=-1, keepdims=True)
    return d * lax.rsqrt(var + NORM_EPS) * g + b


def _ada_kernel(c_ref, w_ref, b_ref, o_ref):
    c = c_ref[...]
    s = c * (1.0 / (1.0 + jnp.exp(-c)))
    o_ref[...] = jnp.dot(s, w_ref[...], preferred_element_type=F32,
                         precision=lax.Precision.HIGHEST) + b_ref[...]


def _ada(c, w_ada, b_ada):
    bsz, d = c.shape
    n = w_ada.shape[1]
    tn = 1024
    return pl.pallas_call(
        _ada_kernel,
        grid=(n // tn,),
        in_specs=[pl.BlockSpec((bsz, d), lambda j: (0, 0)),
                  pl.BlockSpec((d, tn), lambda j: (0, j)),
                  pl.BlockSpec((1, tn), lambda j: (0, j))],
        out_specs=pl.BlockSpec((bsz, tn), lambda j: (0, j)),
        out_shape=jax.ShapeDtypeStruct((bsz, n), F32),
        compiler_params=_params(("arbitrary",)),
        name="ada",
    )(c, w_ada, b_ada.reshape(1, n))


_C_Q = 0
_C_KV = MLA_Q_LORA
_C_DQ = MLA_Q_LORA + MLA_KV_LORA
_C_DK = _C_DQ + DIL_WIDTH
_C_DV = _C_DK + DIL_WIDTH
_C_KR = _C_DV + DIL_WIDTH
_C_KRS = _C_KR + HEAD_SLAB
_C_END = _C_KRS + HEAD_SLAB


def _proj_kernel(x_ref, sc_ref, sh_ref, win_ref, gq_ref, wqp_ref, wqs_ref, gkv_ref,
                 wk_ref, wv_ref, cosq_ref, sinq_ref, cosk_ref, sink_ref,
                 q_ref, k_ref, v_ref, dq_ref, dk_ref, dv_ref):
    h = (x_ref[0] * (1.0 + sc_ref[0]) + sh_ref[0]).astype(BF16)
    proj = _dot(h, win_ref[...])

    c_q = proj[:, _C_Q:_C_KV]
    cqn = (c_q * lax.rsqrt(jnp.mean(c_q * c_q, axis=-1, keepdims=True) + NORM_EPS)
           * gq_ref[...]).astype(BF16)
    q_pre = _dot(cqn, wqp_ref[...])
    q_sw = _dot(cqn, wqs_ref[...])
    cosq = cosq_ref[...]
    sinq = sinq_ref[...]

    c_kv = proj[:, _C_KV:_C_DQ]
    ckvn = (c_kv * lax.rsqrt(jnp.mean(c_kv * c_kv, axis=-1, keepdims=True) + NORM_EPS)
            * gkv_ref[...]).astype(BF16)
    k_nope = _dot(ckvn, wk_ref[...])
    v_ref[0] = _dot(ckvn, wv_ref[...]).astype(BF16)
    k_rope = proj[:, _C_KR:_C_KRS] * cosk_ref[...] + proj[:, _C_KRS:_C_END] * sink_ref[...]

    for hd in range(MLA_HEADS):
        sl = slice(hd * HEAD_SLAB, (hd + 1) * HEAD_SLAB)
        q_ref[0, :, sl] = (q_pre[:, sl] * cosq + q_sw[:, sl] * sinq).astype(BF16)
        k_ref[0, :, sl] = (k_nope[:, sl] + k_rope).astype(BF16)

    dq_ref[0] = (proj[:, _C_DQ:_C_DK] * (DIL_HEAD_DIM ** -0.5)).astype(BF16)
    dk_ref[0] = proj[:, _C_DK:_C_DV].astype(BF16)
    dv_ref[0] = proj[:, _C_DV:_C_KR].astype(BF16)


def _proj(x, mod3, w_in_r, gq, wq_pre, wq_sw, gkv, wk, wv, cosq, sinq, cosk, sink, tm=512):
    bsz, s, d = x.shape
    full = lambda a: pl.BlockSpec(a.shape, lambda b, i: (0,) * a.ndim)
    tab = pl.BlockSpec((tm, LANES), lambda b, i: (i, 0))
    hq = MLA_HEADS * HEAD_SLAB
    outs = [jax.ShapeDtypeStruct((bsz, s, hq), BF16),
            jax.ShapeDtypeStruct((bsz, s, hq), BF16),
            jax.ShapeDtypeStruct((bsz, s, MLA_WIDTH), BF16),
            jax.ShapeDtypeStruct((bsz, s, DIL_WIDTH), BF16),
            jax.ShapeDtypeStruct((bsz, s, DIL_WIDTH), BF16),
            jax.ShapeDtypeStruct((bsz, s, DIL_WIDTH), BF16)]
    ospec = lambda w: pl.BlockSpec((1, tm, w), lambda b, i: (b, i, 0))
    return pl.pallas_call(
        _proj_kernel,
        grid=(bsz, s // tm),
        in_specs=[pl.BlockSpec((1, tm, d), lambda b, i: (b, i, 0)),
                  pl.BlockSpec((1, 1, d), lambda b, i: (6 * b + 1, 0, 0)),
                  pl.BlockSpec((1, 1, d), lambda b, i: (6 * b + 0, 0, 0)),
                  full(w_in_r), full(gq), full(wq_pre), full(wq_sw), full(gkv),
                  full(wk), full(wv), tab, tab, tab, tab],
        out_specs=[ospec(hq), ospec(hq), ospec(MLA_WIDTH), ospec(DIL_WIDTH),
                   ospec(DIL_WIDTH), ospec(DIL_WIDTH)],
        out_shape=outs,
        compiler_params=_params(("parallel", "arbitrary")),
        name="proj",
    )(x, mod3, mod3, w_in_r, gq, wq_pre, wq_sw, gkv, wk, wv, cosq, sinq, cosk, sink)


def _mla_kernel(q_ref, k_ref, v_ref, o_ref):
    v = v_ref[0]
    outs = []
    for j in range(2):
        sl = slice(j * HEAD_SLAB, (j + 1) * HEAD_SLAB)
        s = _dot_nt(q_ref[0, :, sl], k_ref[0, :, sl])
        m = jnp.max(s, axis=-1, keepdims=True)
        p = jnp.exp(s - m)
        l = jnp.sum(p, axis=-1, keepdims=True)
        outs.append(_dot(p.astype(BF16), v) / l)
    lane = lax.broadcasted_iota(jnp.int32, outs[0].shape, 1)
    o_ref[0] = jnp.where(lane < MLA_V, outs[0], outs[1]).astype(o_ref.dtype)


def _mla(q, k, v, tq=256):
    bsz, s, _ = q.shape
    pairs = MLA_HEADS // 2
    return pl.pallas_call(
        _mla_kernel,
        grid=(bsz, pairs, s // tq),
        in_specs=[pl.BlockSpec((1, tq, 2 * HEAD_SLAB), lambda b, p, i: (b, i, p)),
                  pl.BlockSpec((1, s, 2 * HEAD_SLAB), lambda b, p, i: (b, 0, p)),
                  pl.BlockSpec((1, s, 2 * MLA_V), lambda b, p, i: (b, 0, p))],
        out_specs=pl.BlockSpec((1, tq, 2 * MLA_V), lambda b, p, i: (b, i, p)),
        out_shape=jax.ShapeDtypeStruct((bsz, s, MLA_WIDTH), BF16),
        compiler_params=_params(("parallel", "arbitrary", "arbitrary")),
        name="mla",
    )(q, k, v)


def _t5_bucket(rel):
    half = REL_BUCKETS // 2
    ret = (rel > 0).astype(np.int32) * half
    n = np.abs(rel)
    large = REL_MAX_EXACT + (np.log(np.maximum(n, 1) / REL_MAX_EXACT)
                             / np.log(REL_MAX_DIST / REL_MAX_EXACT)
                             * (half - REL_MAX_EXACT)).astype(np.int32)
    large = np.minimum(large, half - 1)
    return ret + np.where(n < REL_MAX_EXACT, n, large).astype(np.int32)


def _bucket_tiles():
    a = np.arange(Q_TILE_DIL)[:, None]
    j = np.arange(K_WIN_DIL)[None, :]
    rel = j - BAND_HALF - a
    tiles = []
    for _, dil in DIL_PATTERNS:
        tiles.append(np.where(np.abs(rel) <= BAND_HALF, _t5_bucket(rel * dil), -1))
    return np.stack(tiles).astype(np.int32)


def _bias_kernel(rb_ref, bucket_ref, o_ref):
    bucket = bucket_ref[0]
    accs = [jnp.where(bucket < 0, NEG_INF, 0.0).astype(F32) for _ in range(DIL_HEADS)]
    for bk in range(REL_BUCKETS):
        hit = bucket == bk
        for hd in range(DIL_HEADS):
            accs[hd] = jnp.where(hit, rb_ref[bk, hd], accs[hd])
    for hd in range(DIL_HEADS):
        o_ref[0, hd] = accs[hd]


def _bias_tiles(rel_bias):
    buckets = jnp.asarray(_bucket_tiles())
    npat = len(DIL_PATTERNS)
    return pl.pallas_call(
        _bias_kernel,
        grid=(npat,),
        in_specs=[pl.BlockSpec(memory_space=pltpu.SMEM),
                  pl.BlockSpec((1, Q_TILE_DIL, K_WIN_DIL), lambda p: (p, 0, 0))],
        out_specs=pl.BlockSpec((1, DIL_HEADS, Q_TILE_DIL, K_WIN_DIL), lambda p: (p, 0, 0, 0)),
        out_shape=jax.ShapeDtypeStruct((npat, DIL_HEADS, Q_TILE_DIL, K_WIN_DIL), F32),
        compiler_params=_params(("arbitrary",)),
        name="dil_bias",
    )(rel_bias.astype(F32), buckets)


def _dil_kernel(q_ref, k_ref, v_ref, bias_ref, o_ref, lse_ref, kpad_ref, vpad_ref, *, length):
    zeros = jnp.zeros((BAND_HALF, LANES), BF16)
    kpad_ref[0:BAND_HALF, :] = zeros
    vpad_ref[0:BAND_HALF, :] = zeros
    kpad_ref[BAND_HALF + length:, :] = zeros
    vpad_ref[BAND_HALF + length:, :] = zeros
    kpad_ref[BAND_HALF:BAND_HALF + length, :] = k_ref[0]
    vpad_ref[BAND_HALF:BAND_HALF + length, :] = v_ref[0]

    lane_q = lax.broadcasted_iota(jnp.int32, (Q_TILE_DIL, LANES), 1)
    low_q = lane_q < DIL_HEAD_DIM
    kcol = lax.broadcasted_iota(jnp.int32, (1, K_WIN_DIL), 1) - BAND_HALF

    def tile(i, carry):
        r0 = pl.multiple_of(i * Q_TILE_DIL, Q_TILE_DIL)
        q = q_ref[0, pl.ds(r0, Q_TILE_DIL), :]
        kw = kpad_ref[pl.ds(r0, K_WIN_DIL), :]
        vw = vpad_ref[pl.ds(r0, K_WIN_DIL), :]
        kpos = kcol + r0
        valid = (kpos >= 0) & (kpos < length)
        outs, lses = [], []
        for j in range(2):
            qh = jnp.where(low_q if j == 0 else ~low_q, q, jnp.zeros_like(q))
            s = _dot_nt(qh, kw) + bias_ref[0, j]
            s = jnp.where(valid, s, NEG_INF)
            m = jnp.max(s, axis=-1, keepdims=True)
            e = jnp.exp(s - m)
            den = jnp.sum(e, axis=-1, keepdims=True)
            outs.append(_dot(e.astype(BF16), vw) / den)
            lses.append(m + jnp.log(den))
        o_ref[0, pl.ds(r0, Q_TILE_DIL), :] = jnp.where(low_q, outs[0], outs[1]).astype(o_ref.dtype)
        lse_ref[0, pl.ds(r0, Q_TILE_DIL), :] = jnp.where(low_q, lses[0], lses[1])
        return carry

    lax.fori_loop(0, length // Q_TILE_DIL, tile, 0)


def _dilated(dq, dk, dv, bias, dil):
    bsz, s, w = dq.shape
    length = s // dil
    pairs = DIL_HEADS // 2
    view = lambda t: t.reshape(bsz, length, dil * w)
    blk = pl.BlockSpec((1, length, LANES), lambda b, r, p: (b, 0, r * pairs + p))
    o, lse = pl.pallas_call(
        functools.partial(_dil_kernel, length=length),
        grid=(bsz, dil, pairs),
        in_specs=[blk, blk, blk,
                  pl.BlockSpec((1, 2, Q_TILE_DIL, K_WIN_DIL), lambda b, r, p: (0, p, 0, 0))],
        out_specs=[blk, blk],
        out_shape=[jax.ShapeDtypeStruct((bsz, length, dil * w), BF16),
                   jax.ShapeDtypeStruct((bsz, length, dil * w), F32)],
        scratch_shapes=[pltpu.VMEM((length + 2 * BAND_HALF, LANES), BF16),
                        pltpu.VMEM((length + 2 * BAND_HALF, LANES), BF16)],
        compiler_params=_params(("parallel", "arbitrary", "arbitrary")),
        name=f"dilated_d{dil}",
    )(view(dq), view(dk), view(dv), bias)
    return o.reshape(bsz, s, w), lse.reshape(bsz, s, w)


def _mix_kernel(x_ref, mla_ref, o1_ref, o2_ref, o3_ref, l1_ref, l2_ref, l3_ref,
                g1_ref, sc2_ref, sh2_ref, wout_ref, lg_ref, lb_ref, wr_ref,
                x1_ref, h2_ref, aff_ref, *, alpha):
    l1, l2, l3 = l1_ref[0], l2_ref[0], l3_ref[0]
    m = jnp.maximum(jnp.maximum(l1, l2), l3)
    e1, e2, e3 = jnp.exp(l1 - m), jnp.exp(l2 - m), jnp.exp(l3 - m)
    dil = (e1 * o1_ref[0].astype(F32) + e2 * o2_ref[0].astype(F32)
           + e3 * o3_ref[0].astype(F32)) / (e1 + e2 + e3)
    mix = _dot(mla_ref[0], wout_ref[0:MLA_WIDTH, :]) + _dot(dil.astype(BF16), wout_ref[MLA_WIDTH:, :])
    y = alpha * x_ref[0] + g1_ref[0] * mix
    x1 = _layer_norm(y, lg_ref[...], lb_ref[...])
    x1_ref[0] = x1
    h2 = x1 * (1.0 + sc2_ref[0]) + sh2_ref[0]
    h2_ref[0] = h2
    logits = lax.dot_general(wr_ref[...], h2, (((1,), (1,)), ((), ())),
                             preferred_element_type=F32,
                             precision=lax.Precision.HIGHEST)
    mx = jnp.max(logits, axis=0, keepdims=True)
    ex = jnp.exp(logits - mx)
    aff_ref[0] = ex / jnp.sum(ex, axis=0, keepdims=True)


def _mix(x, mla, os_, lses, mod3, w_out, ln_g, ln_b, w_router_t, alpha, tm=512):
    bsz, s, d = x.shape
    row = lambda w: pl.BlockSpec((1, tm, w), lambda b, i: (b, i, 0))
    modspec = lambda k: pl.BlockSpec((1, 1, d), lambda b, i: (6 * b + k, 0, 0))
    full = lambda a: pl.BlockSpec(a.shape, lambda b, i: (0,) * a.ndim)
    return pl.pallas_call(
        functools.partial(_mix_kernel, alpha=alpha),
        grid=(bsz, s // tm),
        in_specs=[row(d), row(MLA_WIDTH)] + [row(DIL_WIDTH)] * 6
                 + [modspec(2), modspec(4), modspec(3),
                    full(w_out), full(ln_g), full(ln_b), full(w_router_t)],
        out_specs=[row(d), row(d),
                   pl.BlockSpec((1, N_EXPERTS, tm), lambda b, i: (b, 0, i))],
        out_shape=[jax.ShapeDtypeStruct((bsz, s, d), F32),
                   jax.ShapeDtypeStruct((bsz, s, d), F32),
                   jax.ShapeDtypeStruct((bsz, N_EXPERTS, s), F32)],
        compiler_params=_params(("parallel", "arbitrary")),
        name="mix",
    )(x, mla, *os_, *lses, mod3, mod3, mod3, w_out, ln_g, ln_b, w_router_t)


def _cumsum_lanes(x, tri):
    rows, n = x.shape
    carry = jnp.zeros((rows, 1), F32)
    parts = []
    for j in range(n // LANES):
        inc = _dot(x[:, j * LANES:(j + 1) * LANES].astype(BF16), tri) + carry
        parts.append(inc)
        carry = inc[:, LANES - 1:LANES]
    return jnp.concatenate(parts, axis=1)


def _topk_kernel(aff_ref, idx_ref, val_ref, key_ref, *, cap):
    a = aff_ref[0]
    n_e, s = a.shape
    bits = pltpu.bitcast(a, jnp.int32)
    t = jnp.zeros((n_e, 1), jnp.int32)
    for bit in range(30, -1, -1):
        cand = t | (1 << bit)
        cnt = jnp.sum((bits >= cand).astype(jnp.int32), axis=1, keepdims=True)
        t = jnp.where(cnt >= cap, cand, t)
    gt = bits > t
    eq = bits == t
    n_gt = jnp.sum(gt.astype(jnp.int32), axis=1, keepdims=True)
    ri = lax.broadcasted_iota(jnp.int32, (LANES, LANES), 0)
    ci = lax.broadcasted_iota(jnp.int32, (LANES, LANES), 1)
    tri = jnp.where(ri <= ci, 1.0, 0.0).astype(BF16)
    eq_f = jnp.where(eq, 1.0, 0.0)
    rank_eq = _cumsum_lanes(eq_f, tri) - eq_f
    sel = gt | (eq & (rank_eq < (cap - n_gt).astype(F32)))
    sel_f = jnp.where(sel, 1.0, 0.0)
    pos = _cumsum_lanes(sel_f, tri) - sel_f
    key_ref[...] = jnp.where(sel, pos.astype(jnp.int32), -1)

    tok = lax.broadcasted_iota(jnp.int32, (1, s), 1)
    tok_hi = (tok >> 6).astype(F32)
    tok_lo = (tok & 63).astype(F32)
    slot = lax.broadcasted_iota(jnp.int32, (cap, 1), 0)
    rid = lax.broadcasted_iota(jnp.int32, (8, s), 0)

    def per_expert(e, carry):
        key = key_ref[pl.ds(e, 1), :]
        onehot = jnp.where(key == slot, 1.0, 0.0).astype(BF16)
        ar = aff_ref[0, pl.ds(e, 1), :]
        a_hi = ar.astype(BF16).astype(F32)
        r1 = ar - a_hi
        a_mid = r1.astype(BF16).astype(F32)
        a_lo = r1 - a_mid
        lhs = jnp.where(rid == 0, tok_hi,
              jnp.where(rid == 1, tok_lo,
              jnp.where(rid == 2, a_hi,
              jnp.where(rid == 3, a_mid,
              jnp.where(rid == 4, a_lo, 0.0))))).astype(BF16)
        res = _dot_nt(lhs, onehot)
        idx_ref[0, pl.ds(e, 1), :] = (res[0:1] * 64.0 + res[1:2]).astype(jnp.int32)
        val_ref[0, pl.ds(e, 1), :] = res[2:3] + res[3:4] + res[4:5]
        return carry

    lax.fori_loop(0, n_e, per_expert, 0)


def _topk(aff, cap):
    bsz, n_e, s = aff.shape
    return pl.pallas_call(
        functools.partial(_topk_kernel, cap=cap),
        grid=(bsz,),
        in_specs=[pl.BlockSpec((1, n_e, s), lambda b: (b, 0, 0))],
        out_specs=[pl.BlockSpec((1, n_e, cap), lambda b: (b, 0, 0)),
                   pl.BlockSpec((1, n_e, cap), lambda b: (b, 0, 0))],
        out_shape=[jax.ShapeDtypeStruct((bsz, n_e, cap), jnp.int32),
                   jax.ShapeDtypeStruct((bsz, n_e, cap), F32)],
        scratch_shapes=[pltpu.VMEM((n_e, s), jnp.int32)],
        compiler_params=_params(("parallel",)),
        name="topk",
    )(aff)


GATHER_UNROLL = 8


def _gather_kernel(idx_ref, h_ref, o_ref, *, cap):
    def chunk(g, carry):
        base = g * GATHER_UNROLL
        for u in range(GATHER_UNROLL):
            c = base + u
            o_ref[0, 0, c] = h_ref[0, idx_ref[0, 0, c]]
        return carry

    lax.fori_loop(0, cap // GATHER_UNROLL, chunk, 0)


def _gather(idx3, h4, n_e, cap):
    bsz, s, _, d = h4.shape
    return pl.pallas_call(
        functools.partial(_gather_kernel, cap=cap),
        grid=(bsz, n_e),
        in_specs=[pl.BlockSpec((1, 1, cap), lambda b, e: (b * n_e + e, 0, 0),
                               memory_space=pltpu.SMEM),
                  pl.BlockSpec((1, s, 1, d), lambda b, e: (b, 0, 0, 0))],
        out_specs=pl.BlockSpec((1, 1, cap, 1, d), lambda b, e: (b, e, 0, 0, 0)),
        out_shape=jax.ShapeDtypeStruct((bsz, n_e, cap, 1, d), F32),
        compiler_params=_params(("parallel", "arbitrary")),
        name="gather",
    )(idx3, h4)


def _ffn_kernel(x_ref, wg_ref, wu_ref, wd_ref, y_ref):
    x = x_ref[0, 0]
    gt = _dot(x, wg_ref[0])
    up = _dot(x, wu_ref[0])
    act = (gt * (1.0 / (1.0 + jnp.exp(-gt))) * up).astype(BF16)
    y_ref[0, 0] = _dot(act, wd_ref[0])


def _ffn(xin, wg, wu, wd):
    bsz, n_e, cap, d = xin.shape
    f = wg.shape[-1]
    return pl.pallas_call(
        _ffn_kernel,
        grid=(n_e, bsz),
        in_specs=[pl.BlockSpec((1, 1, cap, d), lambda e, b: (b, e, 0, 0)),
                  pl.BlockSpec((1, d, f), lambda e, b: (e, 0, 0)),
                  pl.BlockSpec((1, d, f), lambda e, b: (e, 0, 0)),
                  pl.BlockSpec((1, f, d), lambda e, b: (e, 0, 0))],
        out_specs=pl.BlockSpec((1, 1, cap, d), lambda e, b: (b, e, 0, 0)),
        out_shape=jax.ShapeDtypeStruct((bsz, n_e, cap, d), F32),
        compiler_params=_params(("parallel", "arbitrary")),
        name="ffn",
    )(xin, wg, wu, wd)


SCATTER_UNROLL = 8


def _combine_kernel(idx_ref, val_ref, y_ref, o_ref, *, cap):
    @pl.when(pl.program_id(1) == 0)
    def _():
        o_ref[...] = jnp.zeros_like(o_ref)

    def chunk(g, carry):
        base = g * SCATTER_UNROLL
        rows = []
        for u in range(SCATTER_UNROLL):
            c = base + u
            t = idx_ref[0, 0, c]
            rows.append((t, o_ref[0, t] + val_ref[0, 0, c] * y_ref[0, 0, c]))
        for t, r in rows:
            o_ref[0, t] = r
        return carry

    lax.fori_loop(0, cap // SCATTER_UNROLL, chunk, 0)


def _combine(idx3, val3, y5, s):
    bsz, n_e, cap, _, d = y5.shape
    sm = lambda: pl.BlockSpec((1, 1, cap), lambda b, e: (b * n_e + e, 0, 0),
                              memory_space=pltpu.SMEM)
    return pl.pallas_call(
        functools.partial(_combine_kernel, cap=cap),
        grid=(bsz, n_e),
        in_specs=[sm(), sm(),
                  pl.BlockSpec((1, 1, cap, 1, d), lambda b, e: (b, e, 0, 0, 0))],
        out_specs=pl.BlockSpec((1, s, 1, d), lambda b, e: (b, 0, 0, 0)),
        out_shape=jax.ShapeDtypeStruct((bsz, s, 1, d), F32),
        compiler_params=_params(("parallel", "arbitrary")),
        name="combine",
    )(idx3, val3, y5)


def _final_kernel(x1_ref, moe_ref, g2_ref, lg_ref, lb_ref, o_ref, *, alpha):
    y = alpha * x1_ref[0] + g2_ref[0] * moe_ref[0]
    o_ref[0] = _layer_norm(y, lg_ref[...], lb_ref[...])


def _final(x1, moe, mod3, ln_g, ln_b, alpha, tm=512):
    bsz, s, d = x1.shape
    row = pl.BlockSpec((1, tm, d), lambda b, i: (b, i, 0))
    vec = pl.BlockSpec((1, d), lambda b, i: (0, 0))
    return pl.pallas_call(
        functools.partial(_final_kernel, alpha=alpha),
        grid=(bsz, s // tm),
        in_specs=[row, row, pl.BlockSpec((1, 1, d), lambda b, i: (6 * b + 5, 0, 0)), vec, vec],
        out_specs=row,
        out_shape=jax.ShapeDtypeStruct((bsz, s, d), F32),
        compiler_params=_params(("parallel", "arbitrary")),
        name="final",
    )(x1, moe, mod3, ln_g, ln_b)


def _rope_tables(s):
    inv = ROPE_THETA ** (-jnp.arange(0, MLA_ROPE, 2, dtype=F32) / MLA_ROPE)
    ang = jnp.arange(s, dtype=F32)[:, None] * inv[None, :]
    cos, sin = jnp.cos(ang), jnp.sin(ang)
    scale = (MLA_NOPE + MLA_ROPE) ** -0.5
    z32 = jnp.zeros((s, HEAD_SLAB - MLA_NOPE - MLA_ROPE), F32)
    z64 = jnp.zeros((s, MLA_NOPE), F32)
    cosq = jnp.concatenate([jnp.full((s, MLA_NOPE), scale, F32), cos * scale, cos * scale, z32], 1)
    sinq_scaled = jnp.concatenate([z64, sin * scale, sin * scale, z32], 1)
    cosk = jnp.concatenate([z64, cos, cos, z32], 1)
    sink = jnp.concatenate([z64, sin, sin, z32], 1)
    return cosq, sinq_scaled, cosk, sink


def _layout_weights(w_in, w_uq, w_ukv):
    d = w_in.shape[0]
    c0 = MLA_Q_LORA + MLA_KV_LORA
    kr = w_in[:, c0:c0 + MLA_ROPE]
    t1, t2 = kr[:, :HALF_ROPE], kr[:, HALF_ROPE:]
    z = lambda n: jnp.zeros((d, n), w_in.dtype)
    pad = HEAD_SLAB - MLA_NOPE - MLA_ROPE
    w_in_r = jnp.concatenate(
        [w_in[:, :c0], w_in[:, c0 + MLA_ROPE:],
         z(MLA_NOPE), t1, t2, z(pad),
         z(MLA_NOPE), -t2, t1, z(pad)], axis=1).astype(BF16)

    ql = w_uq.shape[0]
    wq = w_uq.reshape(ql, MLA_HEADS, MLA_NOPE + MLA_ROPE)
    qn, q1, q2 = wq[..., :MLA_NOPE], wq[..., MLA_NOPE:MLA_NOPE + HALF_ROPE], wq[..., MLA_NOPE + HALF_ROPE:]
    zq = lambda n: jnp.zeros((ql, MLA_HEADS, n), w_uq.dtype)
    wq_pre = jnp.concatenate([qn, q1, q2, zq(pad)], -1).reshape(ql, -1).astype(BF16)
    wq_sw = jnp.concatenate([zq(MLA_NOPE), -q2, q1, zq(pad)], -1).reshape(ql, -1).astype(BF16)

    kl = w_ukv.shape[0]
    wkv = w_ukv.reshape(kl, MLA_HEADS, MLA_NOPE + MLA_V)
    wk = jnp.concatenate([wkv[..., :MLA_NOPE],
                          jnp.zeros((kl, MLA_HEADS, HEAD_SLAB - MLA_NOPE), w_ukv.dtype)],
                         -1).reshape(kl, -1).astype(BF16)
    wv = wkv[..., MLA_NOPE:].reshape(kl, -1).astype(BF16)
    return w_in_r, wq_pre, wq_sw, wk, wv


def kernel(x, c, w_ada, b_ada, w_in, q_norm_g, w_uq, kv_norm_g, w_ukv, rel_bias, w_out, ln1_g,
           ln1_b, w_router, w_gate, w_up, w_down, ln2_g, ln2_b):
    bsz, s, d = x.shape
    depth = w_ada.shape[0]
    alpha = (2 * depth) ** 0.25
    cap = max(1, EC_CAPACITY_FACTOR * s // N_EXPERTS)
    assert s % (Q_TILE_DIL * max(dl for _, dl in DIL_PATTERNS)) == 0
    assert all(win // 2 // dl == BAND_HALF for win, dl in DIL_PATTERNS)

    cosq, sinq, cosk, sink = _rope_tables(s)
    bias = _bias_tiles(rel_bias)

    for l in range(depth):
        mod3 = _ada(c, w_ada[l], b_ada[l]).reshape(bsz * 6, 1, d)
        w_in_r, wq_pre, wq_sw, wk, wv = _layout_weights(w_in[l], w_uq[l], w_ukv[l])
        q, k, v, dq, dk, dv = _proj(
            x, mod3, w_in_r, q_norm_g[l].reshape(1, -1), wq_pre, wq_sw,
            kv_norm_g[l].reshape(1, -1), wk, wv, cosq, sinq, cosk, sink)
        mla = _mla(q, k, v)
        os_, lses = [], []
        for p, (_, dil) in enumerate(DIL_PATTERNS):
            o, lse = _dilated(dq, dk, dv, bias[p:p + 1], dil)
            os_.append(o)
            lses.append(lse)
        x1, h2, aff = _mix(x, mla, os_, lses, mod3, w_out[l].astype(BF16),
                           ln1_g[l].reshape(1, d), ln1_b[l].reshape(1, d),
                           w_router[l].T, alpha)
        idx, vals = _topk(aff, cap)
        idx3 = idx.reshape(bsz * N_EXPERTS, 1, cap)
        val3 = vals.reshape(bsz * N_EXPERTS, 1, cap)
        xin = _gather(idx3, h2.reshape(bsz, s, 1, d), N_EXPERTS, cap)
        xin = xin.reshape(bsz, N_EXPERTS, cap, d).astype(BF16)
        y = _ffn(xin, w_gate[l].astype(BF16), w_up[l].astype(BF16), w_down[l].astype(BF16))
        moe = _combine(idx3, val3, y.reshape(bsz, N_EXPERTS, cap, 1, d), s)
        x = _final(x1, moe.reshape(bsz, s, d), mod3, ln2_g[l].reshape(1, d),
                   ln2_b[l].reshape(1, d), alpha)
    return x
```

```python
import functools
import math

import numpy as np
import jax
import jax.numpy as jnp
from jax import lax
from jax.experimental import pallas as pl
from jax.experimental.pallas import tpu as pltpu

MLA_HEADS = 8
MLA_NOPE = 64
MLA_ROPE = 32
MLA_V = 64
MLA_Q_LORA = 384
MLA_KV_LORA = 256
ROPE_THETA = 10000.0
DIL_HEADS = 8
DIL_HEAD_DIM = 64
DIL_PATTERNS = ((128, 1), (512, 4), (2048, 16))
REL_BUCKETS = 32
REL_MAX_EXACT = 8
REL_MAX_DIST = 1024
N_EXPERTS = 16
EC_CAPACITY_FACTOR = 2
NORM_EPS = 1e-6
NEG_INF = -1e30

LANES = 128
HEAD_SLAB = 128
VMEM_LIMIT = 48 * 1024 * 1024

F32 = jnp.float32
BF16 = jnp.bfloat16
HALF_ROPE = MLA_ROPE // 2
DIL_WIDTH = DIL_HEADS * DIL_HEAD_DIM
MLA_WIDTH = MLA_HEADS * MLA_V
BAND_HALF = 64
Q_TILE_DIL = 2 * BAND_HALF
K_WIN_DIL = 4 * BAND_HALF


def _params(sem, vmem=VMEM_LIMIT):
    return pltpu.CompilerParams(dimension_semantics=sem, vmem_limit_bytes=vmem)


def _dot(a, b):
    return jnp.dot(a, b, preferred_element_type=F32)


def _dot_nt(a, b):
    return lax.dot_general(a, b, (((1,), (1,)), ((), ())), preferred_element_type=F32)


def _layer_norm(y, g, b):
    mu = jnp.mean(y, axis=-1, keepdims=True)
    d = y - mu
    var = jnp.mean(d * d, axis=-1, keepdims=True)
    return d * lax.rsqrt(var + NORM_EPS) * g + b


def _ada_kernel(c_ref, w_ref, b_ref, o_ref):
    c = c_ref[...]
    s = c * (1.0 / (1.0 + jnp.exp(-c)))
    o_ref[...] = jnp.dot(s, w_ref[...], preferred_element_type=F32,
                         precision=lax.Precision.HIGHEST) + b_ref[...]


def _ada(c, w_ada, b_ada):
    bsz, d = c.shape
    n = w_ada.shape[1]
    tn = 1024
    return pl.pallas_call(
        _ada_kernel,
        grid=(n // tn,),
        in_specs=[pl.BlockSpec((bsz, d), lambda j: (0, 0)),
                  pl.BlockSpec((d, tn), lambda j: (0, j)),
                  pl.BlockSpec((1, tn), lambda j: (0, j))],
        out_specs=pl.BlockSpec((bsz, tn), lambda j: (0, j)),
        out_shape=jax.ShapeDtypeStruct((bsz, n), F32),
        compiler_params=_params(("arbitrary",)),
        name="ada",
    )(c, w_ada, b_ada.reshape(1, n))


_C_Q = 0
_C_KV = MLA_Q_LORA
_C_DQ = MLA_Q_LORA + MLA_KV_LORA
_C_DK = _C_DQ + DIL_WIDTH
_C_DV = _C_DK + DIL_WIDTH
_C_KR = _C_DV + DIL_WIDTH
_C_KRS = _C_KR + HEAD_SLAB
_C_END = _C_KRS + HEAD_SLAB


def _proj_kernel(x_ref, sc_ref, sh_ref, win_ref, gq_ref, wqp_ref, wqs_ref, gkv_ref,
                 wk_ref, wv_ref, cosq_ref, sinq_ref, cosk_ref, sink_ref,
                 q_ref, k_ref, v_ref, dq_ref, dk_ref, dv_ref):
    h = (x_ref[0] * (1.0 + sc_ref[0]) + sh_ref[0]).astype(BF16)
    proj = _dot(h, win_ref[...])

    c_q = proj[:, _C_Q:_C_KV]
    cqn = (c_q * lax.rsqrt(jnp.mean(c_q * c_q, axis=-1, keepdims=True) + NORM_EPS)
           * gq_ref[...]).astype(BF16)
    q_pre = _dot(cqn, wqp_ref[...])
    q_sw = _dot(cqn, wqs_ref[...])
    cosq = cosq_ref[...]
    sinq = sinq_ref[...]

    c_kv = proj[:, _C_KV:_C_DQ]
    ckvn = (c_kv * lax.rsqrt(jnp.mean(c_kv * c_kv, axis=-1, keepdims=True) + NORM_EPS)
            * gkv_ref[...]).astype(BF16)
    k_nope = _dot(ckvn, wk_ref[...])
    v_ref[0] = _dot(ckvn, wv_ref[...]).astype(BF16)
    k_rope = proj[:, _C_KR:_C_KRS] * cosk_ref[...] + proj[:, _C_KRS:_C_END] * sink_ref[...]

    for hd in range(MLA_HEADS):
        sl = slice(hd * HEAD_SLAB, (hd + 1) * HEAD_SLAB)
        q_ref[0, :, sl] = (q_pre[:, sl] * cosq + q_sw[:, sl] * sinq).astype(BF16)
        k_ref[0, :, sl] = (k_nope[:, sl] + k_rope).astype(BF16)

    dq_ref[0] = proj[:, _C_DQ:_C_DK] * (DIL_HEAD_DIM ** -0.5)
    dk_ref[0] = proj[:, _C_DK:_C_DV]
    dv_ref[0] = proj[:, _C_DV:_C_KR]


def _proj(x, mod3, w_in_r, gq, wq_pre, wq_sw, gkv, wk, wv, cosq, sinq, cosk, sink, tm=512):
    bsz, s, d = x.shape
    full = lambda a: pl.BlockSpec(a.shape, lambda b, i: (0,) * a.ndim)
    tab = pl.BlockSpec((tm, LANES), lambda b, i: (i, 0))
    hq = MLA_HEADS * HEAD_SLAB
    outs = [jax.ShapeDtypeStruct((bsz, s, hq), BF16),
            jax.ShapeDtypeStruct((bsz, s, hq), BF16),
            jax.ShapeDtypeStruct((bsz, s, MLA_WIDTH), BF16),
            jax.ShapeDtypeStruct((bsz, s, DIL_WIDTH), F32),
            jax.ShapeDtypeStruct((bsz, s, DIL_WIDTH), F32),
            jax.ShapeDtypeStruct((bsz, s, DIL_WIDTH), F32)]
    ospec = lambda w: pl.BlockSpec((1, tm, w), lambda b, i: (b, i, 0))
    return pl.pallas_call(
        _proj_kernel,
        grid=(bsz, s // tm),
        in_specs=[pl.BlockSpec((1, tm, d), lambda b, i: (b, i, 0)),
                  pl.BlockSpec((1, 1, d), lambda b, i: (6 * b + 1, 0, 0)),
                  pl.BlockSpec((1, 1, d), lambda b, i: (6 * b + 0, 0, 0)),
                  full(w_in_r), full(gq), full(wq_pre), full(wq_sw), full(gkv),
                  full(wk), full(wv), tab, tab, tab, tab],
        out_specs=[ospec(hq), ospec(hq), ospec(MLA_WIDTH), ospec(DIL_WIDTH),
                   ospec(DIL_WIDTH), ospec(DIL_WIDTH)],
        out_shape=outs,
        compiler_params=_params(("parallel", "arbitrary")),
        name="proj",
    )(x, mod3, mod3, w_in_r, gq, wq_pre, wq_sw, gkv, wk, wv, cosq, sinq, cosk, sink)


def _mla_kernel(q_ref, k_ref, v_ref, o_ref):
    v = v_ref[0]
    outs = []
    for j in range(2):
        sl = slice(j * HEAD_SLAB, (j + 1) * HEAD_SLAB)
        s = _dot_nt(q_ref[0, :, sl], k_ref[0, :, sl])
        m = jnp.max(s, axis=-1, keepdims=True)
        p = jnp.exp(s - m)
        l = jnp.sum(p, axis=-1, keepdims=True)
        outs.append(_dot(p.astype(BF16), v) / l)
    lane = lax.broadcasted_iota(jnp.int32, outs[0].shape, 1)
    o_ref[0] = jnp.where(lane < MLA_V, outs[0], outs[1]).astype(o_ref.dtype)


def _mla(q, k, v, tq=256):
    bsz, s, _ = q.shape
    pairs = MLA_HEADS // 2
    return pl.pallas_call(
        _mla_kernel,
        grid=(bsz, pairs, s // tq),
        in_specs=[pl.BlockSpec((1, tq, 2 * HEAD_SLAB), lambda b, p, i: (b, i, p)),
                  pl.BlockSpec((1, s, 2 * HEAD_SLAB), lambda b, p, i: (b, 0, p)),
                  pl.BlockSpec((1, s, 2 * MLA_V), lambda b, p, i: (b, 0, p))],
        out_specs=pl.BlockSpec((1, tq, 2 * MLA_V), lambda b, p, i: (b, i, p)),
        out_shape=jax.ShapeDtypeStruct((bsz, s, MLA_WIDTH), BF16),
        compiler_params=_params(("parallel", "arbitrary", "arbitrary")),
        name="mla",
    )(q, k, v)


def _t5_bucket(rel):
    half = REL_BUCKETS // 2
    ret = (rel > 0).astype(np.int32) * half
    n = np.abs(rel)
    large = REL_MAX_EXACT + (np.log(np.maximum(n, 1) / REL_MAX_EXACT)
                             / np.log(REL_MAX_DIST / REL_MAX_EXACT)
                             * (half - REL_MAX_EXACT)).astype(np.int32)
    large = np.minimum(large, half - 1)
    return ret + np.where(n < REL_MAX_EXACT, n, large).astype(np.int32)


def _bucket_tiles():
    a = np.arange(Q_TILE_DIL)[:, None]
    j = np.arange(K_WIN_DIL)[None, :]
    rel = j - BAND_HALF - a
    in_range = [np.ones_like(j, bool), j >= BAND_HALF, j < K_WIN_DIL - BAND_HALF]
    tiles = []
    for _, dil in DIL_PATTERNS:
        for ok in in_range:
            tiles.append(np.where((np.abs(rel) <= BAND_HALF) & ok, _t5_bucket(rel * dil), -1))
    return np.stack(tiles).astype(np.int32)


N_EDGE = 3


def _bias_kernel(rb_ref, bucket_ref, o_ref):
    bucket = bucket_ref[0]
    accs = [jnp.where(bucket < 0, NEG_INF, 0.0).astype(F32) for _ in range(DIL_HEADS)]
    for bk in range(REL_BUCKETS):
        hit = bucket == bk
        for hd in range(DIL_HEADS):
            accs[hd] = jnp.where(hit, rb_ref[bk, hd], accs[hd])
    for hd in range(DIL_HEADS):
        o_ref[0, hd] = accs[hd]


def _bias_tiles(rel_bias):
    buckets = jnp.asarray(_bucket_tiles())
    n = buckets.shape[0]
    return pl.pallas_call(
        _bias_kernel,
        grid=(n,),
        in_specs=[pl.BlockSpec(memory_space=pltpu.SMEM),
                  pl.BlockSpec((1, Q_TILE_DIL, K_WIN_DIL), lambda p: (p, 0, 0))],
        out_specs=pl.BlockSpec((1, DIL_HEADS, Q_TILE_DIL, K_WIN_DIL), lambda p: (p, 0, 0, 0)),
        out_shape=jax.ShapeDtypeStruct((n, DIL_HEADS, Q_TILE_DIL, K_WIN_DIL), F32),
        compiler_params=_params(("arbitrary",)),
        name="dil_bias",
    )(rel_bias.astype(F32), buckets)


DIL_TILES_PER_ITER = 8
DIL_COPY_ROWS = 256
DIL_MERGE_ROWS = 256


def _dil_kernel(q_ref, k_ref, v_ref, bias_ref, o_ref,
                k0_ref, k1_ref, vp_ref, op_ref, lp_ref, *, seq):
    lane = lax.broadcasted_iota(jnp.int32, (1, LANES), 1)
    low = lane < DIL_HEAD_DIM
    zeros = jnp.zeros((BAND_HALF, LANES), BF16)

    def rows(start, size, dil):
        return pl.ds(start, size) if dil == 1 else pl.ds(start, size, stride=dil)

    for p, (_, dil) in enumerate(DIL_PATTERNS):
        length = seq // dil
        ntile = length // Q_TILE_DIL
        span = length + 2 * BAND_HALF

        def deinterleave(r, carry, dil=dil, length=length, span=span):
            base = pl.multiple_of(r * span, 2 * BAND_HALF)
            for ref in (k0_ref, k1_ref, vp_ref):
                ref[pl.ds(base, BAND_HALF), :] = zeros
                ref[pl.ds(base + BAND_HALF + length, BAND_HALF), :] = zeros
            for cidx in range(length // DIL_COPY_ROWS):
                src = rows(r + dil * DIL_COPY_ROWS * cidx, DIL_COPY_ROWS, dil)
                dst = pl.ds(pl.multiple_of(base + BAND_HALF + DIL_COPY_ROWS * cidx, BAND_HALF),
                            DIL_COPY_ROWS)
                kk = k_ref[0, src, :]
                k0_ref[dst, :] = jnp.where(low, kk, 0.0).astype(BF16)
                k1_ref[dst, :] = jnp.where(low, 0.0, kk).astype(BF16)
                vp_ref[dst, :] = v_ref[0, src, :].astype(BF16)
            return carry

        lax.fori_loop(0, dil, deinterleave, 0)

        def tiles(it, carry, p=p, dil=dil, ntile=ntile, span=span):
            for u in range(DIL_TILES_PER_ITER):
                t = it * DIL_TILES_PER_ITER + u
                r = t // ntile
                i = t % ntile
                edge = jnp.where(i == 0, 1, jnp.where(i == ntile - 1, 2, 0))
                r0 = i * Q_TILE_DIL
                tok = rows(r + dil * r0, Q_TILE_DIL, dil)
                win = pl.ds(pl.multiple_of(r * span + r0, Q_TILE_DIL), K_WIN_DIL)
                q = q_ref[0, tok, :].astype(BF16)
                vw = vp_ref[win, :]
                outs, lses = [], []
                for j, kref in enumerate((k0_ref, k1_ref)):
                    s = _dot_nt(q, kref[win, :]) + bias_ref[p * N_EDGE + edge, j]
                    m = jnp.max(s, axis=-1, keepdims=True)
                    e = jnp.exp(s - m)
                    den = jnp.sum(e, axis=-1, keepdims=True)
                    outs.append(_dot(e.astype(BF16), vw) / den)
                    lses.append(m + jnp.log(den))
                op_ref[p, tok, :] = jnp.where(low, outs[0], outs[1])
                lp_ref[p, tok, :] = jnp.where(low, lses[0], lses[1])
            return carry

        lax.fori_loop(0, seq // Q_TILE_DIL // DIL_TILES_PER_ITER, tiles, 0)

    def merge(g, carry):
        sl = pl.ds(pl.multiple_of(g * DIL_MERGE_ROWS, DIL_MERGE_ROWS), DIL_MERGE_ROWS)
        ls = [lp_ref[p, sl, :] for p in range(len(DIL_PATTERNS))]
        m = functools.reduce(jnp.maximum, ls)
        es = [jnp.exp(l - m) for l in ls]
        num = functools.reduce(lambda a, b: a + b,
                               [e * op_ref[p, sl, :] for p, e in enumerate(es)])
        den = functools.reduce(lambda a, b: a + b, es)
        o_ref[0, sl, :] = (num / den).astype(o_ref.dtype)
        return carry

    lax.fori_loop(0, seq // DIL_MERGE_ROWS, merge, 0)


def _dilated(dq, dk, dv, bias):
    bsz, s, w = dq.shape
    pairs = DIL_HEADS // 2
    npat = len(DIL_PATTERNS)
    blk = pl.BlockSpec((1, s, LANES), lambda b, p: (b, 0, p))
    pad_rows = s + 2 * BAND_HALF * max(dl for _, dl in DIL_PATTERNS)
    assert (s // Q_TILE_DIL) % DIL_TILES_PER_ITER == 0
    return pl.pallas_call(
        functools.partial(_dil_kernel, seq=s),
        grid=(bsz, pairs),
        in_specs=[blk, blk, blk,
                  pl.BlockSpec((npat * N_EDGE, 2, Q_TILE_DIL, K_WIN_DIL), lambda b, p: (0, p, 0, 0))],
        out_specs=blk,
        out_shape=jax.ShapeDtypeStruct((bsz, s, w), BF16),
        scratch_shapes=[pltpu.VMEM((pad_rows, LANES), BF16),
                        pltpu.VMEM((pad_rows, LANES), BF16),
                        pltpu.VMEM((pad_rows, LANES), BF16),
                        pltpu.VMEM((npat, s, LANES), F32),
                        pltpu.VMEM((npat, s, LANES), F32)],
        compiler_params=_params(("parallel", "arbitrary")),
        name="dilated",
    )(dq, dk, dv, bias)


def _mix_kernel(x_ref, mla_ref, dil_ref,
                g1_ref, sc2_ref, sh2_ref, wout_ref, lg_ref, lb_ref, wr_ref,
                x1_ref, h2_ref, aff_ref, *, alpha):
    mix = _dot(mla_ref[0], wout_ref[0:MLA_WIDTH, :]) + _dot(dil_ref[0], wout_ref[MLA_WIDTH:, :])
    y = alpha * x_ref[0] + g1_ref[0] * mix
    x1 = _layer_norm(y, lg_ref[...], lb_ref[...])
    x1_ref[0] = x1
    h2 = x1 * (1.0 + sc2_ref[0]) + sh2_ref[0]
    h2_ref[0] = h2
    logits = lax.dot_general(wr_ref[...], h2, (((1,), (1,)), ((), ())),
                             preferred_element_type=F32,
                             precision=lax.Precision.HIGHEST)
    mx = jnp.max(logits, axis=0, keepdims=True)
    ex = jnp.exp(logits - mx)
    aff_ref[0] = ex / jnp.sum(ex, axis=0, keepdims=True)


def _mix(x, mla, dil, mod3, w_out, ln_g, ln_b, w_router_t, alpha, tm=512):
    bsz, s, d = x.shape
    row = lambda w: pl.BlockSpec((1, tm, w), lambda b, i: (b, i, 0))
    modspec = lambda k: pl.BlockSpec((1, 1, d), lambda b, i: (6 * b + k, 0, 0))
    full = lambda a: pl.BlockSpec(a.shape, lambda b, i: (0,) * a.ndim)
    return pl.pallas_call(
        functools.partial(_mix_kernel, alpha=alpha),
        grid=(bsz, s // tm),
        in_specs=[row(d), row(MLA_WIDTH), row(DIL_WIDTH)]
                 + [modspec(2), modspec(4), modspec(3),
                    full(w_out), full(ln_g), full(ln_b), full(w_router_t)],
        out_specs=[row(d), row(d),
                   pl.BlockSpec((1, N_EXPERTS, tm), lambda b, i: (b, 0, i))],
        out_shape=[jax.ShapeDtypeStruct((bsz, s, d), F32),
                   jax.ShapeDtypeStruct((bsz, s, d), F32),
                   jax.ShapeDtypeStruct((bsz, N_EXPERTS, s), F32)],
        compiler_params=_params(("parallel", "arbitrary")),
        name="mix",
    )(x, mla, dil, mod3, mod3, mod3, w_out, ln_g, ln_b, w_router_t)


def _cumsum_lanes(x, tri):
    rows, n = x.shape
    carry = jnp.zeros((rows, 1), F32)
    parts = []
    for j in range(n // LANES):
        inc = _dot(x[:, j * LANES:(j + 1) * LANES].astype(BF16), tri) + carry
        parts.append(inc)
        carry = inc[:, LANES - 1:LANES]
    return jnp.concatenate(parts, axis=1)


def _topk_kernel(aff_ref, idx_ref, val_ref, key_ref, *, cap):
    a = aff_ref[0]
    n_e, s = a.shape
    bits = pltpu.bitcast(a, jnp.int32)
    t = jnp.zeros((n_e, 1), jnp.int32)
    for bit in range(30, -1, -1):
        cand = t | (1 << bit)
        cnt = jnp.sum((bits >= cand).astype(jnp.int32), axis=1, keepdims=True)
        t = jnp.where(cnt >= cap, cand, t)
    gt = bits > t
    eq = bits == t
    n_gt = jnp.sum(gt.astype(jnp.int32), axis=1, keepdims=True)
    ri = lax.broadcasted_iota(jnp.int32, (LANES, LANES), 0)
    ci = lax.broadcasted_iota(jnp.int32, (LANES, LANES), 1)
    tri = jnp.where(ri <= ci, 1.0, 0.0).astype(BF16)
    eq_f = jnp.where(eq, 1.0, 0.0)
    rank_eq = _cumsum_lanes(eq_f, tri) - eq_f
    sel = gt | (eq & (rank_eq < (cap - n_gt).astype(F32)))
    sel_f = jnp.where(sel, 1.0, 0.0)
    pos = _cumsum_lanes(sel_f, tri) - sel_f
    key_ref[...] = jnp.where(sel, pos.astype(jnp.int32), -1)

    tok = lax.broadcasted_iota(jnp.int32, (1, s), 1)
    tok_hi = (tok >> 6).astype(F32)
    tok_lo = (tok & 63).astype(F32)
    slot = lax.broadcasted_iota(jnp.int32, (cap, 1), 0)
    rid = lax.broadcasted_iota(jnp.int32, (8, s), 0)

    def per_expert(e, carry):
        key = key_ref[pl.ds(e, 1), :]
        onehot = jnp.where(key == slot, 1.0, 0.0).astype(BF16)
        ar = aff_ref[0, pl.ds(e, 1), :]
        a_hi = ar.astype(BF16).astype(F32)
        r1 = ar - a_hi
        a_mid = r1.astype(BF16).astype(F32)
        a_lo = r1 - a_mid
        lhs = jnp.where(rid == 0, tok_hi,
              jnp.where(rid == 1, tok_lo,
              jnp.where(rid == 2, a_hi,
              jnp.where(rid == 3, a_mid,
              jnp.where(rid == 4, a_lo, 0.0))))).astype(BF16)
        res = _dot_nt(lhs, onehot)
        idx_ref[0, pl.ds(e, 1), :] = (res[0:1] * 64.0 + res[1:2]).astype(jnp.int32)
        val_ref[0, pl.ds(e, 1), :] = res[2:3] + res[3:4] + res[4:5]
        return carry

    lax.fori_loop(0, n_e, per_expert, 0)


def _topk(aff, cap):
    bsz, n_e, s = aff.shape
    return pl.pallas_call(
        functools.partial(_topk_kernel, cap=cap),
        grid=(bsz,),
        in_specs=[pl.BlockSpec((1, n_e, s), lambda b: (b, 0, 0))],
        out_specs=[pl.BlockSpec((1, n_e, cap), lambda b: (b, 0, 0)),
                   pl.BlockSpec((1, n_e, cap), lambda b: (b, 0, 0))],
        out_shape=[jax.ShapeDtypeStruct((bsz, n_e, cap), jnp.int32),
                   jax.ShapeDtypeStruct((bsz, n_e, cap), F32)],
        scratch_shapes=[pltpu.VMEM((n_e, s), jnp.int32)],
        compiler_params=_params(("parallel",)),
        name="topk",
    )(aff)


GATHER_UNROLL = 8


def _gather_kernel(idx_ref, h_ref, o_ref, *, cap):
    def chunk(g, carry):
        base = g * GATHER_UNROLL
        for u in range(GATHER_UNROLL):
            c = base + u
            o_ref[0, 0, c] = h_ref[0, idx_ref[0, 0, c]]
        return carry

    lax.fori_loop(0, cap // GATHER_UNROLL, chunk, 0)


def _gather(idx3, h4, n_e, cap):
    bsz, s, _, d = h4.shape
    return pl.pallas_call(
        functools.partial(_gather_kernel, cap=cap),
        grid=(bsz, n_e),
        in_specs=[pl.BlockSpec((1, 1, cap), lambda b, e: (b * n_e + e, 0, 0),
                               memory_space=pltpu.SMEM),
                  pl.BlockSpec((1, s, 1, d), lambda b, e: (b, 0, 0, 0))],
        out_specs=pl.BlockSpec((1, 1, cap, 1, d), lambda b, e: (b, e, 0, 0, 0)),
        out_shape=jax.ShapeDtypeStruct((bsz, n_e, cap, 1, d), F32),
        compiler_params=_params(("parallel", "arbitrary")),
        name="gather",
    )(idx3, h4)


def _ffn_kernel(x_ref, wg_ref, wu_ref, wd_ref, y_ref):
    x = x_ref[0, 0]
    gt = _dot(x, wg_ref[0])
    up = _dot(x, wu_ref[0])
    act = (gt * (1.0 / (1.0 + jnp.exp(-gt))) * up).astype(BF16)
    y_ref[0, 0] = _dot(act, wd_ref[0])


def _ffn(xin, wg, wu, wd):
    bsz, n_e, cap, d = xin.shape
    f = wg.shape[-1]
    return pl.pallas_call(
        _ffn_kernel,
        grid=(n_e, bsz),
        in_specs=[pl.BlockSpec((1, 1, cap, d), lambda e, b: (b, e, 0, 0)),
                  pl.BlockSpec((1, d, f), lambda e, b: (e, 0, 0)),
                  pl.BlockSpec((1, d, f), lambda e, b: (e, 0, 0)),
                  pl.BlockSpec((1, f, d), lambda e, b: (e, 0, 0))],
        out_specs=pl.BlockSpec((1, 1, cap, d), lambda e, b: (b, e, 0, 0)),
        out_shape=jax.ShapeDtypeStruct((bsz, n_e, cap, d), F32),
        compiler_params=_params(("parallel", "arbitrary")),
        name="ffn",
    )(xin, wg, wu, wd)


SCATTER_UNROLL = 8


def _combine_kernel(idx_ref, val_ref, y_ref, o_ref, *, cap):
    @pl.when(pl.program_id(1) == 0)
    def _():
        o_ref[...] = jnp.zeros_like(o_ref)

    def chunk(g, carry):
        base = g * SCATTER_UNROLL
        rows = []
        for u in range(SCATTER_UNROLL):
            c = base + u
            t = idx_ref[0, 0, c]
            rows.append((t, o_ref[0, t] + val_ref[0, 0, c] * y_ref[0, 0, c]))
        for t, r in rows:
            o_ref[0, t] = r
        return carry

    lax.fori_loop(0, cap // SCATTER_UNROLL, chunk, 0)


def _combine(idx3, val3, y5, s):
    bsz, n_e, cap, _, d = y5.shape
    sm = lambda: pl.BlockSpec((1, 1, cap), lambda b, e: (b * n_e + e, 0, 0),
                              memory_space=pltpu.SMEM)
    return pl.pallas_call(
        functools.partial(_combine_kernel, cap=cap),
        grid=(bsz, n_e),
        in_specs=[sm(), sm(),
                  pl.BlockSpec((1, 1, cap, 1, d), lambda b, e: (b, e, 0, 0, 0))],
        out_specs=pl.BlockSpec((1, s, 1, d), lambda b, e: (b, 0, 0, 0)),
        out_shape=jax.ShapeDtypeStruct((bsz, s, 1, d), F32),
        compiler_params=_params(("parallel", "arbitrary")),
        name="combine",
    )(idx3, val3, y5)


def _final_kernel(x1_ref, moe_ref, g2_ref, lg_ref, lb_ref, o_ref, *, alpha):
    y = alpha * x1_ref[0] + g2_ref[0] * moe_ref[0]
    o_ref[0] = _layer_norm(y, lg_ref[...], lb_ref[...])


def _final(x1, moe, mod3, ln_g, ln_b, alpha, tm=512):
    bsz, s, d = x1.shape
    row = pl.BlockSpec((1, tm, d), lambda b, i: (b, i, 0))
    vec = pl.BlockSpec((1, d), lambda b, i: (0, 0))
    return pl.pallas_call(
        functools.partial(_final_kernel, alpha=alpha),
        grid=(bsz, s // tm),
        in_specs=[row, row, pl.BlockSpec((1, 1, d), lambda b, i: (6 * b + 5, 0, 0)), vec, vec],
        out_specs=row,
        out_shape=jax.ShapeDtypeStruct((bsz, s, d), F32),
        compiler_params=_params(("parallel", "arbitrary")),
        name="final",
    )(x1, moe, mod3, ln_g, ln_b)


def _rope_tables(s):
    inv = ROPE_THETA ** (-jnp.arange(0, MLA_ROPE, 2, dtype=F32) / MLA_ROPE)
    ang = jnp.arange(s, dtype=F32)[:, None] * inv[None, :]
    cos, sin = jnp.cos(ang), jnp.sin(ang)
    scale = (MLA_NOPE + MLA_ROPE) ** -0.5
    z32 = jnp.zeros((s, HEAD_SLAB - MLA_NOPE - MLA_ROPE), F32)
    z64 = jnp.zeros((s, MLA_NOPE), F32)
    cosq = jnp.concatenate([jnp.full((s, MLA_NOPE), scale, F32), cos * scale, cos * scale, z32], 1)
    sinq_scaled = jnp.concatenate([z64, sin * scale, sin * scale, z32], 1)
    cosk = jnp.concatenate([z64, cos, cos, z32], 1)
    sink = jnp.concatenate([z64, sin, sin, z32], 1)
    return cosq, sinq_scaled, cosk, sink


def _layout_weights(w_in, w_uq, w_ukv):
    d = w_in.shape[0]
    c0 = MLA_Q_LORA + MLA_KV_LORA
    kr = w_in[:, c0:c0 + MLA_ROPE]
    t1, t2 = kr[:, :HALF_ROPE], kr[:, HALF_ROPE:]
    z = lambda n: jnp.zeros((d, n), w_in.dtype)
    pad = HEAD_SLAB - MLA_NOPE - MLA_ROPE
    w_in_r = jnp.concatenate(
        [w_in[:, :c0], w_in[:, c0 + MLA_ROPE:],
         z(MLA_NOPE), t1, t2, z(pad),
         z(MLA_NOPE), -t2, t1, z(pad)], axis=1).astype(BF16)

    ql = w_uq.shape[0]
    wq = w_uq.reshape(ql, MLA_HEADS, MLA_NOPE + MLA_ROPE)
    qn, q1, q2 = wq[..., :MLA_NOPE], wq[..., MLA_NOPE:MLA_NOPE + HALF_ROPE], wq[..., MLA_NOPE + HALF_ROPE:]
    zq = lambda n: jnp.zeros((ql, MLA_HEADS, n), w_uq.dtype)
    wq_pre = jnp.concatenate([qn, q1, q2, zq(pad)], -1).reshape(ql, -1).astype(BF16)
    wq_sw = jnp.concatenate([zq(MLA_NOPE), -q2, q1, zq(pad)], -1).reshape(ql, -1).astype(BF16)

    kl = w_ukv.shape[0]
    wkv = w_ukv.reshape(kl, MLA_HEADS, MLA_NOPE + MLA_V)
    wk = jnp.concatenate([wkv[..., :MLA_NOPE],
                          jnp.zeros((kl, MLA_HEADS, HEAD_SLAB - MLA_NOPE), w_ukv.dtype)],
                         -1).reshape(kl, -1).astype(BF16)
    wv = wkv[..., MLA_NOPE:].reshape(kl, -1).astype(BF16)
    return w_in_r, wq_pre, wq_sw, wk, wv


def kernel(x, c, w_ada, b_ada, w_in, q_norm_g, w_uq, kv_norm_g, w_ukv, rel_bias, w_out, ln1_g,
           ln1_b, w_router, w_gate, w_up, w_down, ln2_g, ln2_b):
    bsz, s, d = x.shape
    depth = w_ada.shape[0]
    alpha = (2 * depth) ** 0.25
    cap = max(1, EC_CAPACITY_FACTOR * s // N_EXPERTS)
    assert s % (max(2 * Q_TILE_DIL, DIL_COPY_ROWS) * max(dl for _, dl in DIL_PATTERNS)) == 0
    assert all(win // 2 // dl == BAND_HALF for win, dl in DIL_PATTERNS)

    cosq, sinq, cosk, sink = _rope_tables(s)
    bias = _bias_tiles(rel_bias)

    for l in range(depth):
        mod3 = _ada(c, w_ada[l], b_ada[l]).reshape(bsz * 6, 1, d)
        w_in_r, wq_pre, wq_sw, wk, wv = _layout_weights(w_in[l], w_uq[l], w_ukv[l])
        q, k, v, dq, dk, dv = _proj(
            x, mod3, w_in_r, q_norm_g[l].reshape(1, -1), wq_pre, wq_sw,
            kv_norm_g[l].reshape(1, -1), wk, wv, cosq, sinq, cosk, sink)
        mla = _mla(q, k, v)
        dil_out = _dilated(dq, dk, dv, bias)
        x1, h2, aff = _mix(x, mla, dil_out, mod3, w_out[l].astype(BF16),
                           ln1_g[l].reshape(1, d), ln1_b[l].reshape(1, d),
                           w_router[l].T, alpha)
        idx, vals = _topk(aff, cap)
        idx3 = idx.reshape(bsz * N_EXPERTS, 1, cap)
        val3 = vals.reshape(bsz * N_EXPERTS, 1, cap)
        xin = _gather(idx3, h2.reshape(bsz, s, 1, d), N_EXPERTS, cap)
        xin = xin.reshape(bsz, N_EXPERTS, cap, d).astype(BF16)
        y = _ffn(xin, w_gate[l].astype(BF16), w_up[l].astype(BF16), w_down[l].astype(BF16))
        moe = _combine(idx3, val3, y.reshape(bsz, N_EXPERTS, cap, 1, d), s)
        x = _final(x1, moe.reshape(bsz, s, d), mod3, ln2_g[l].reshape(1, d),
                   ln2_b[l].reshape(1, d), alpha)
    return x
```

```python
import functools
import math

import numpy as np
import jax
import jax.numpy as jnp
from jax import lax
from jax.experimental import pallas as pl
from jax.experimental.pallas import tpu as pltpu

MLA_HEADS = 8
MLA_NOPE = 64
MLA_ROPE = 32
MLA_V = 64
MLA_Q_LORA = 384
MLA_KV_LORA = 256
ROPE_THETA = 10000.0
DIL_HEADS = 8
DIL_HEAD_DIM = 64
DIL_PATTERNS = ((128, 1), (512, 4), (2048, 16))
REL_BUCKETS = 32
REL_MAX_EXACT = 8
REL_MAX_DIST = 1024
N_EXPERTS = 16
EC_CAPACITY_FACTOR = 2
NORM_EPS = 1e-6
NEG_INF = -1e30

LANES = 128
HEAD_SLAB = 128
VMEM_LIMIT = 48 * 1024 * 1024

F32 = jnp.float32
BF16 = jnp.bfloat16
HALF_ROPE = MLA_ROPE // 2
DIL_WIDTH = DIL_HEADS * DIL_HEAD_DIM
MLA_WIDTH = MLA_HEADS * MLA_V
BAND_HALF = 64
Q_TILE_DIL = 2 * BAND_HALF
K_WIN_DIL = 4 * BAND_HALF


def _params(sem, vmem=VMEM_LIMIT):
    return pltpu.CompilerParams(dimension_semantics=sem, vmem_limit_bytes=vmem)


def _dot(a, b):
    return jnp.dot(a, b, preferred_element_type=F32)


def _dot_nt(a, b):
    return lax.dot_general(a, b, (((1,), (1,)), ((), ())), preferred_element_type=F32)


def _layer_norm(y, g, b):
    mu = jnp.mean(y, axis=-1, keepdims=True)
    d = y - mu
    var = jnp.mean(d * d, axis=-1, keepdims=True)
    return d * lax.rsqrt(var + NORM_EPS) * g + b


def _ada_kernel(c_ref, w_ref, b_ref, o_ref):
    c = c_ref[...]
    s = c * (1.0 / (1.0 + jnp.exp(-c)))
    o_ref[...] = jnp.dot(s, w_ref[...], preferred_element_type=F32,
                         precision=lax.Precision.HIGHEST) + b_ref[...]


def _ada(c, w_ada, b_ada):
    bsz, d = c.shape
    n = w_ada.shape[1]
    tn = 1024
    return pl.pallas_call(
        _ada_kernel,
        grid=(n // tn,),
        in_specs=[pl.BlockSpec((bsz, d), lambda j: (0, 0)),
                  pl.BlockSpec((d, tn), lambda j: (0, j)),
                  pl.BlockSpec((1, tn), lambda j: (0, j))],
        out_specs=pl.BlockSpec((bsz, tn), lambda j: (0, j)),
        out_shape=jax.ShapeDtypeStruct((bsz, n), F32),
        compiler_params=_params(("arbitrary",)),
        name="ada",
    )(c, w_ada, b_ada.reshape(1, n))


_C_Q = 0
_C_KV = MLA_Q_LORA
_C_DQ = MLA_Q_LORA + MLA_KV_LORA
_C_DK = _C_DQ + DIL_WIDTH
_C_DV = _C_DK + DIL_WIDTH
_C_KR = _C_DV + DIL_WIDTH
_C_KRS = _C_KR + HEAD_SLAB
_C_END = _C_KRS + HEAD_SLAB


def _proj_kernel(x_ref, sc_ref, sh_ref, win_ref, gq_ref, wqp_ref, wqs_ref, gkv_ref,
                 wk_ref, wv_ref, cosq_ref, sinq_ref, cosk_ref, sink_ref,
                 q_ref, k_ref, v_ref, dq_ref, dk_ref, dv_ref):
    h = (x_ref[0] * (1.0 + sc_ref[0]) + sh_ref[0]).astype(BF16)
    proj = _dot(h, win_ref[...])

    c_q = proj[:, _C_Q:_C_KV]
    cqn = (c_q * lax.rsqrt(jnp.mean(c_q * c_q, axis=-1, keepdims=True) + NORM_EPS)
           * gq_ref[...]).astype(BF16)
    q_pre = _dot(cqn, wqp_ref[...])
    q_sw = _dot(cqn, wqs_ref[...])
    cosq = cosq_ref[...]
    sinq = sinq_ref[...]

    c_kv = proj[:, _C_KV:_C_DQ]
    ckvn = (c_kv * lax.rsqrt(jnp.mean(c_kv * c_kv, axis=-1, keepdims=True) + NORM_EPS)
            * gkv_ref[...]).astype(BF16)
    k_nope = _dot(ckvn, wk_ref[...])
    vt = _dot_nt(wv_ref[...], ckvn)
    vrow = lax.broadcasted_iota(jnp.int32, vt.shape, 0)
    vt = jnp.where((vrow & MLA_V) != 0, 1.0, vt).astype(BF16)
    v_ref[0] = vt.reshape(MLA_HEADS, HEAD_SLAB, vt.shape[1])
    k_rope = proj[:, _C_KR:_C_KRS] * cosk_ref[...] + proj[:, _C_KRS:_C_END] * sink_ref[...]

    for hd in range(MLA_HEADS):
        sl = slice(hd * HEAD_SLAB, (hd + 1) * HEAD_SLAB)
        q_ref[0, :, sl] = (q_pre[:, sl] * cosq + q_sw[:, sl] * sinq).astype(BF16)
        k_ref[0, :, sl] = (k_nope[:, sl] + k_rope).astype(BF16)

    dq_ref[0] = proj[:, _C_DQ:_C_DK] * (DIL_HEAD_DIM ** -0.5)
    dk_ref[0] = proj[:, _C_DK:_C_DV]
    dv_ref[0] = proj[:, _C_DV:_C_KR]


def _proj(x, mod3, w_in_r, gq, wq_pre, wq_sw, gkv, wk, wv, cosq, sinq, cosk, sink, tm=512):
    bsz, s, d = x.shape
    full = lambda a: pl.BlockSpec(a.shape, lambda b, i: (0,) * a.ndim)
    tab = pl.BlockSpec((tm, LANES), lambda b, i: (i, 0))
    hq = MLA_HEADS * HEAD_SLAB
    outs = [jax.ShapeDtypeStruct((bsz, s, hq), BF16),
            jax.ShapeDtypeStruct((bsz, s, hq), BF16),
            jax.ShapeDtypeStruct((bsz, MLA_HEADS, HEAD_SLAB, s), BF16),
            jax.ShapeDtypeStruct((bsz, s, DIL_WIDTH), F32),
            jax.ShapeDtypeStruct((bsz, s, DIL_WIDTH), F32),
            jax.ShapeDtypeStruct((bsz, s, DIL_WIDTH), F32)]
    ospec = lambda w: pl.BlockSpec((1, tm, w), lambda b, i: (b, i, 0))
    return pl.pallas_call(
        _proj_kernel,
        grid=(bsz, s // tm),
        in_specs=[pl.BlockSpec((1, tm, d), lambda b, i: (b, i, 0)),
                  pl.BlockSpec((1, 1, d), lambda b, i: (6 * b + 1, 0, 0)),
                  pl.BlockSpec((1, 1, d), lambda b, i: (6 * b + 0, 0, 0)),
                  full(w_in_r), full(gq), full(wq_pre), full(wq_sw), full(gkv),
                  full(wk), full(wv), tab, tab, tab, tab],
        out_specs=[ospec(hq), ospec(hq),
                   pl.BlockSpec((1, MLA_HEADS, HEAD_SLAB, tm), lambda b, i: (b, 0, 0, i)),
                   ospec(DIL_WIDTH),
                   ospec(DIL_WIDTH), ospec(DIL_WIDTH)],
        out_shape=outs,
        compiler_params=_params(("parallel", "arbitrary")),
        name="proj",
    )(x, mod3, mod3, w_in_r, gq, wq_pre, wq_sw, gkv, wk, wv, cosq, sinq, cosk, sink)


MLA_KEY_CHUNK = 512


MLA_Q_SUB = 256


def _mla_kernel(q_ref, k_ref, vt_ref, o_ref, s_ref, p_ref):
    tq = q_ref.shape[1]
    seq = k_ref.shape[1]
    nchunk = seq // MLA_KEY_CHUNK
    units = [(a, j) for a in range(tq // MLA_Q_SUB) for j in range(2)]
    n = len(units)
    maxima = [None] * n
    heads_out = {}

    def chunk(c):
        return slice(c * MLA_KEY_CHUNK, (c + 1) * MLA_KEY_CHUNK)

    def scores(u, c):
        a, j = units[u]
        sl = slice(j * HEAD_SLAB, (j + 1) * HEAD_SLAB)
        s = _dot_nt(k_ref[0, chunk(c), sl], q_ref[0, a * MLA_Q_SUB:(a + 1) * MLA_Q_SUB, sl])
        s_ref[u % 2, chunk(c), :] = s
        mc = jnp.max(s, axis=0, keepdims=True)
        maxima[u] = mc if maxima[u] is None else jnp.maximum(maxima[u], mc)

    def probs(u, c):
        p_ref[u % 2, chunk(c), :] = jnp.exp2(s_ref[u % 2, chunk(c), :] - maxima[u]).astype(BF16)

    def values(u):
        a, j = units[u]
        acc = _dot(vt_ref[0, j], p_ref[u % 2])
        heads_out[(a, j)] = acc[0:MLA_V] / acc[MLA_V:MLA_V + 1]
        if j == 1:
            o_t = jnp.concatenate([heads_out[(a, 0)], heads_out[(a, 1)]], axis=0)
            o_ref[0, a * MLA_Q_SUB:(a + 1) * MLA_Q_SUB, :] = o_t.T.astype(o_ref.dtype)

    for stage in range(n + 2):
        if 0 <= stage - 2 < n:
            values(stage - 2)
        for c in range(nchunk):
            if stage < n:
                scores(stage, c)
            if 0 <= stage - 1 < n:
                probs(stage - 1, c)


def _mla(q, k, vt, tq=1024):
    bsz, s, _ = q.shape
    pairs = MLA_HEADS // 2
    return pl.pallas_call(
        _mla_kernel,
        grid=(bsz, pairs, s // tq),
        in_specs=[pl.BlockSpec((1, tq, 2 * HEAD_SLAB), lambda b, p, i: (b, i, p)),
                  pl.BlockSpec((1, s, 2 * HEAD_SLAB), lambda b, p, i: (b, 0, p)),
                  pl.BlockSpec((1, 2, 2 * MLA_V, s), lambda b, p, i: (b, p, 0, 0))],
        out_specs=pl.BlockSpec((1, tq, 2 * MLA_V), lambda b, p, i: (b, i, p)),
        out_shape=jax.ShapeDtypeStruct((bsz, s, MLA_WIDTH), BF16),
        scratch_shapes=[pltpu.VMEM((2, s, MLA_Q_SUB), F32),
                        pltpu.VMEM((2, s, MLA_Q_SUB), BF16)],
        compiler_params=_params(("parallel", "arbitrary", "arbitrary")),
        name="mla",
    )(q, k, vt)


def _t5_bucket(rel):
    half = REL_BUCKETS // 2
    ret = (rel > 0).astype(np.int32) * half
    n = np.abs(rel)
    large = REL_MAX_EXACT + (np.log(np.maximum(n, 1) / REL_MAX_EXACT)
                             / np.log(REL_MAX_DIST / REL_MAX_EXACT)
                             * (half - REL_MAX_EXACT)).astype(np.int32)
    large = np.minimum(large, half - 1)
    return ret + np.where(n < REL_MAX_EXACT, n, large).astype(np.int32)


def _bucket_tiles():
    a = np.arange(Q_TILE_DIL)[:, None]
    j = np.arange(K_WIN_DIL)[None, :]
    rel = j - BAND_HALF - a
    in_range = [np.ones_like(j, bool), j >= BAND_HALF, j < K_WIN_DIL - BAND_HALF]
    tiles = []
    for _, dil in DIL_PATTERNS:
        for ok in in_range:
            tiles.append(np.where((np.abs(rel) <= BAND_HALF) & ok, _t5_bucket(rel * dil), -1))
    return np.stack(tiles).astype(np.int32)


N_EDGE = 3


def _bias_kernel(rb_ref, bucket_ref, o_ref):
    bucket = bucket_ref[0]
    accs = [jnp.where(bucket < 0, NEG_INF, 0.0).astype(F32) for _ in range(DIL_HEADS)]
    for bk in range(REL_BUCKETS):
        hit = bucket == bk
        for hd in range(DIL_HEADS):
            accs[hd] = jnp.where(hit, rb_ref[bk, hd], accs[hd])
    for hd in range(DIL_HEADS):
        o_ref[0, hd] = accs[hd]


def _bias_tiles(rel_bias):
    buckets = jnp.asarray(_bucket_tiles())
    n = buckets.shape[0]
    return pl.pallas_call(
        _bias_kernel,
        grid=(n,),
        in_specs=[pl.BlockSpec(memory_space=pltpu.SMEM),
                  pl.BlockSpec((1, Q_TILE_DIL, K_WIN_DIL), lambda p: (p, 0, 0))],
        out_specs=pl.BlockSpec((1, DIL_HEADS, Q_TILE_DIL, K_WIN_DIL), lambda p: (p, 0, 0, 0)),
        out_shape=jax.ShapeDtypeStruct((n, DIL_HEADS, Q_TILE_DIL, K_WIN_DIL), F32),
        compiler_params=_params(("arbitrary",)),
        name="dil_bias",
    )(rel_bias.astype(F32), buckets)


DIL_TILES_PER_ITER = 8
DIL_COPY_ROWS = 256
DIL_MERGE_ROWS = 256


def _dil_kernel(q_ref, k_ref, v_ref, bias_ref, o_ref,
                k0_ref, k1_ref, vp_ref, op_ref, lp_ref, *, seq):
    lane = lax.broadcasted_iota(jnp.int32, (1, LANES), 1)
    low = lane < DIL_HEAD_DIM
    zeros = jnp.zeros((BAND_HALF, LANES), BF16)

    def rows(start, size, dil):
        return pl.ds(start, size) if dil == 1 else pl.ds(start, size, stride=dil)

    for p, (_, dil) in enumerate(DIL_PATTERNS):
        length = seq // dil
        ntile = length // Q_TILE_DIL
        span = length + 2 * BAND_HALF

        def deinterleave(r, carry, dil=dil, length=length, span=span):
            base = pl.multiple_of(r * span, 2 * BAND_HALF)
            for ref in (k0_ref, k1_ref, vp_ref):
                ref[pl.ds(base, BAND_HALF), :] = zeros
                ref[pl.ds(base + BAND_HALF + length, BAND_HALF), :] = zeros
            for cidx in range(length // DIL_COPY_ROWS):
                src = rows(r + dil * DIL_COPY_ROWS * cidx, DIL_COPY_ROWS, dil)
                dst = pl.ds(pl.multiple_of(base + BAND_HALF + DIL_COPY_ROWS * cidx, BAND_HALF),
                            DIL_COPY_ROWS)
                kk = k_ref[0, src, :]
                k0_ref[dst, :] = jnp.where(low, kk, 0.0).astype(BF16)
                k1_ref[dst, :] = jnp.where(low, 0.0, kk).astype(BF16)
                vp_ref[dst, :] = v_ref[0, src, :].astype(BF16)
            return carry

        lax.fori_loop(0, dil, deinterleave, 0)

        def tiles(it, carry, p=p, dil=dil, ntile=ntile, span=span):
            for u in range(DIL_TILES_PER_ITER):
                t = it * DIL_TILES_PER_ITER + u
                r = t // ntile
                i = t % ntile
                edge = jnp.where(i == 0, 1, jnp.where(i == ntile - 1, 2, 0))
                r0 = i * Q_TILE_DIL
                tok = rows(r + dil * r0, Q_TILE_DIL, dil)
                win = pl.ds(pl.multiple_of(r * span + r0, Q_TILE_DIL), K_WIN_DIL)
                q = q_ref[0, tok, :].astype(BF16)
                vw = vp_ref[win, :]
                outs, lses = [], []
                for j, kref in enumerate((k0_ref, k1_ref)):
                    s = _dot_nt(q, kref[win, :]) + bias_ref[p * N_EDGE + edge, j]
                    m = jnp.max(s, axis=-1, keepdims=True)
                    e = jnp.exp(s - m)
                    den = jnp.sum(e, axis=-1, keepdims=True)
                    outs.append(_dot(e.astype(BF16), vw) / den)
                    lses.append(m + jnp.log(den))
                op_ref[p, tok, :] = jnp.where(low, outs[0], outs[1])
                lp_ref[p, tok, :] = jnp.where(low, lses[0], lses[1])
            return carry

        lax.fori_loop(0, seq // Q_TILE_DIL // DIL_TILES_PER_ITER, tiles, 0)

    def merge(g, carry):
        sl = pl.ds(pl.multiple_of(g * DIL_MERGE_ROWS, DIL_MERGE_ROWS), DIL_MERGE_ROWS)
        ls = [lp_ref[p, sl, :] for p in range(len(DIL_PATTERNS))]
        m = functools.reduce(jnp.maximum, ls)
        es = [jnp.exp(l - m) for l in ls]
        num = functools.reduce(lambda a, b: a + b,
                               [e * op_ref[p, sl, :] for p, e in enumerate(es)])
        den = functools.reduce(lambda a, b: a + b, es)
        o_ref[0, sl, :] = (num / den).astype(o_ref.dtype)
        return carry

    lax.fori_loop(0, seq // DIL_MERGE_ROWS, merge, 0)


def _dilated(dq, dk, dv, bias):
    bsz, s, w = dq.shape
    pairs = DIL_HEADS // 2
    npat = len(DIL_PATTERNS)
    blk = pl.BlockSpec((1, s, LANES), lambda b, p: (b, 0, p))
    pad_rows = s + 2 * BAND_HALF * max(dl for _, dl in DIL_PATTERNS)
    assert (s // Q_TILE_DIL) % DIL_TILES_PER_ITER == 0
    return pl.pallas_call(
        functools.partial(_dil_kernel, seq=s),
        grid=(bsz, pairs),
        in_specs=[blk, blk, blk,
                  pl.BlockSpec((npat * N_EDGE, 2, Q_TILE_DIL, K_WIN_DIL), lambda b, p: (0, p, 0, 0))],
        out_specs=blk,
        out_shape=jax.ShapeDtypeStruct((bsz, s, w), BF16),
        scratch_shapes=[pltpu.VMEM((pad_rows, LANES), BF16),
                        pltpu.VMEM((pad_rows, LANES), BF16),
                        pltpu.VMEM((pad_rows, LANES), BF16),
                        pltpu.VMEM((npat, s, LANES), F32),
                        pltpu.VMEM((npat, s, LANES), F32)],
        compiler_params=_params(("parallel", "arbitrary")),
        name="dilated",
    )(dq, dk, dv, bias)


def _mix_kernel(x_ref, mla_ref, dil_ref,
                g1_ref, sc2_ref, sh2_ref, wout_ref, lg_ref, lb_ref, wr_ref,
                x1_ref, h2_ref, aff_ref, *, alpha):
    mix = _dot(mla_ref[0], wout_ref[0:MLA_WIDTH, :]) + _dot(dil_ref[0], wout_ref[MLA_WIDTH:, :])
    y = alpha * x_ref[0] + g1_ref[0] * mix
    x1 = _layer_norm(y, lg_ref[...], lb_ref[...])
    x1_ref[0] = x1
    h2 = x1 * (1.0 + sc2_ref[0]) + sh2_ref[0]
    h2_ref[0] = h2
    logits = lax.dot_general(wr_ref[...], h2, (((1,), (1,)), ((), ())),
                             preferred_element_type=F32,
                             precision=lax.Precision.HIGHEST)
    mx = jnp.max(logits, axis=0, keepdims=True)
    ex = jnp.exp(logits - mx)
    aff_ref[0] = ex / jnp.sum(ex, axis=0, keepdims=True)


def _mix(x, mla, dil, mod3, w_out, ln_g, ln_b, w_router_t, alpha, tm=512):
    bsz, s, d = x.shape
    row = lambda w: pl.BlockSpec((1, tm, w), lambda b, i: (b, i, 0))
    modspec = lambda k: pl.BlockSpec((1, 1, d), lambda b, i: (6 * b + k, 0, 0))
    full = lambda a: pl.BlockSpec(a.shape, lambda b, i: (0,) * a.ndim)
    return pl.pallas_call(
        functools.partial(_mix_kernel, alpha=alpha),
        grid=(bsz, s // tm),
        in_specs=[row(d), row(MLA_WIDTH), row(DIL_WIDTH)]
                 + [modspec(2), modspec(4), modspec(3),
                    full(w_out), full(ln_g), full(ln_b), full(w_router_t)],
        out_specs=[row(d), row(d),
                   pl.BlockSpec((1, N_EXPERTS, tm), lambda b, i: (b, 0, i))],
        out_shape=[jax.ShapeDtypeStruct((bsz, s, d), F32),
                   jax.ShapeDtypeStruct((bsz, s, d), F32),
                   jax.ShapeDtypeStruct((bsz, N_EXPERTS, s), F32)],
        compiler_params=_params(("parallel", "arbitrary")),
        name="mix",
    )(x, mla, dil, mod3, mod3, mod3, w_out, ln_g, ln_b, w_router_t)


def _cumsum_lanes(x, tri):
    rows, n = x.shape
    carry = jnp.zeros((rows, 1), F32)
    parts = []
    for j in range(n // LANES):
        inc = _dot(x[:, j * LANES:(j + 1) * LANES].astype(BF16), tri) + carry
        parts.append(inc)
        carry = inc[:, LANES - 1:LANES]
    return jnp.concatenate(parts, axis=1)


def _topk_kernel(aff_ref, idx_ref, val_ref, key_ref, *, cap):
    a = aff_ref[0]
    n_e, s = a.shape
    bits = pltpu.bitcast(a, jnp.int32)
    t = jnp.zeros((n_e, 1), jnp.int32)
    for bit in range(30, -1, -1):
        cand = t | (1 << bit)
        cnt = jnp.sum((bits >= cand).astype(jnp.int32), axis=1, keepdims=True)
        t = jnp.where(cnt >= cap, cand, t)
    gt = bits > t
    eq = bits == t
    n_gt = jnp.sum(gt.astype(jnp.int32), axis=1, keepdims=True)
    ri = lax.broadcasted_iota(jnp.int32, (LANES, LANES), 0)
    ci = lax.broadcasted_iota(jnp.int32, (LANES, LANES), 1)
    tri = jnp.where(ri <= ci, 1.0, 0.0).astype(BF16)
    eq_f = jnp.where(eq, 1.0, 0.0)
    rank_eq = _cumsum_lanes(eq_f, tri) - eq_f
    sel = gt | (eq & (rank_eq < (cap - n_gt).astype(F32)))
    sel_f = jnp.where(sel, 1.0, 0.0)
    pos = _cumsum_lanes(sel_f, tri) - sel_f
    key_ref[...] = jnp.where(sel, pos.astype(jnp.int32), -1)

    tok = lax.broadcasted_iota(jnp.int32, (1, s), 1)
    tok_hi = (tok >> 6).astype(F32)
    tok_lo = (tok & 63).astype(F32)
    slot = lax.broadcasted_iota(jnp.int32, (cap, 1), 0)
    rid = lax.broadcasted_iota(jnp.int32, (8, s), 0)

    def per_expert(e, carry):
        key = key_ref[pl.ds(e, 1), :]
        onehot = jnp.where(key == slot, 1.0, 0.0).astype(BF16)
        ar = aff_ref[0, pl.ds(e, 1), :]
        a_hi = ar.astype(BF16).astype(F32)
        r1 = ar - a_hi
        a_mid = r1.astype(BF16).astype(F32)
        a_lo = r1 - a_mid
        lhs = jnp.where(rid == 0, tok_hi,
              jnp.where(rid == 1, tok_lo,
              jnp.where(rid == 2, a_hi,
              jnp.where(rid == 3, a_mid,
              jnp.where(rid == 4, a_lo, 0.0))))).astype(BF16)
        res = _dot_nt(lhs, onehot)
        idx_ref[0, pl.ds(e, 1), :] = (res[0:1] * 64.0 + res[1:2]).astype(jnp.int32)
        val_ref[0, pl.ds(e, 1), :] = res[2:3] + res[3:4] + res[4:5]
        return carry

    lax.fori_loop(0, n_e, per_expert, 0)


def _topk(aff, cap):
    bsz, n_e, s = aff.shape
    return pl.pallas_call(
        functools.partial(_topk_kernel, cap=cap),
        grid=(bsz,),
        in_specs=[pl.BlockSpec((1, n_e, s), lambda b: (b, 0, 0))],
        out_specs=[pl.BlockSpec((1, n_e, cap), lambda b: (b, 0, 0)),
                   pl.BlockSpec((1, n_e, cap), lambda b: (b, 0, 0))],
        out_shape=[jax.ShapeDtypeStruct((bsz, n_e, cap), jnp.int32),
                   jax.ShapeDtypeStruct((bsz, n_e, cap), F32)],
        scratch_shapes=[pltpu.VMEM((n_e, s), jnp.int32)],
        compiler_params=_params(("parallel",)),
        name="topk",
    )(aff)


GATHER_UNROLL = 8


def _gather_kernel(idx_ref, h_ref, o_ref, *, cap):
    def chunk(g, carry):
        base = g * GATHER_UNROLL
        for u in range(GATHER_UNROLL):
            c = base + u
            o_ref[0, 0, c] = h_ref[0, idx_ref[0, 0, c]]
        return carry

    lax.fori_loop(0, cap // GATHER_UNROLL, chunk, 0)


def _gather(idx3, h4, n_e, cap):
    bsz, s, _, d = h4.shape
    return pl.pallas_call(
        functools.partial(_gather_kernel, cap=cap),
        grid=(bsz, n_e),
        in_specs=[pl.BlockSpec((1, 1, cap), lambda b, e: (b * n_e + e, 0, 0),
                               memory_space=pltpu.SMEM),
                  pl.BlockSpec((1, s, 1, d), lambda b, e: (b, 0, 0, 0))],
        out_specs=pl.BlockSpec((1, 1, cap, 1, d), lambda b, e: (b, e, 0, 0, 0)),
        out_shape=jax.ShapeDtypeStruct((bsz, n_e, cap, 1, d), F32),
        compiler_params=_params(("parallel", "arbitrary")),
        name="gather",
    )(idx3, h4)


def _ffn_kernel(x_ref, wg_ref, wu_ref, wd_ref, y_ref):
    x = x_ref[0, 0]
    gt = _dot(x, wg_ref[0])
    up = _dot(x, wu_ref[0])
    act = (gt * (1.0 / (1.0 + jnp.exp(-gt))) * up).astype(BF16)
    y_ref[0, 0] = _dot(act, wd_ref[0])


def _ffn(xin, wg, wu, wd):
    bsz, n_e, cap, d = xin.shape
    f = wg.shape[-1]
    return pl.pallas_call(
        _ffn_kernel,
        grid=(n_e, bsz),
        in_specs=[pl.BlockSpec((1, 1, cap, d), lambda e, b: (b, e, 0, 0)),
                  pl.BlockSpec((1, d, f), lambda e, b: (e, 0, 0)),
                  pl.BlockSpec((1, d, f), lambda e, b: (e, 0, 0)),
                  pl.BlockSpec((1, f, d), lambda e, b: (e, 0, 0))],
        out_specs=pl.BlockSpec((1, 1, cap, d), lambda e, b: (b, e, 0, 0)),
        out_shape=jax.ShapeDtypeStruct((bsz, n_e, cap, d), F32),
        compiler_params=_params(("parallel", "arbitrary")),
        name="ffn",
    )(xin, wg, wu, wd)


SCATTER_UNROLL = 8


def _combine_kernel(idx_ref, val_ref, y_ref, o_ref, *, cap):
    @pl.when(pl.program_id(1) == 0)
    def _():
        o_ref[...] = jnp.zeros_like(o_ref)

    def chunk(g, carry):
        base = g * SCATTER_UNROLL
        rows = []
        for u in range(SCATTER_UNROLL):
            c = base + u
            t = idx_ref[0, 0, c]
            rows.append((t, o_ref[0, t] + val_ref[0, 0, c] * y_ref[0, 0, c]))
        for t, r in rows:
            o_ref[0, t] = r
        return carry

    lax.fori_loop(0, cap // SCATTER_UNROLL, chunk, 0)


def _combine(idx3, val3, y5, s):
    bsz, n_e, cap, _, d = y5.shape
    sm = lambda: pl.BlockSpec((1, 1, cap), lambda b, e: (b * n_e + e, 0, 0),
                              memory_space=pltpu.SMEM)
    return pl.pallas_call(
        functools.partial(_combine_kernel, cap=cap),
        grid=(bsz, n_e),
        in_specs=[sm(), sm(),
                  pl.BlockSpec((1, 1, cap, 1, d), lambda b, e: (b, e, 0, 0, 0))],
        out_specs=pl.BlockSpec((1, s, 1, d), lambda b, e: (b, 0, 0, 0)),
        out_shape=jax.ShapeDtypeStruct((bsz, s, 1, d), F32),
        compiler_params=_params(("parallel", "arbitrary")),
        name="combine",
    )(idx3, val3, y5)


def _final_kernel(x1_ref, moe_ref, g2_ref, lg_ref, lb_ref, o_ref, *, alpha):
    y = alpha * x1_ref[0] + g2_ref[0] * moe_ref[0]
    o_ref[0] = _layer_norm(y, lg_ref[...], lb_ref[...])


def _final(x1, moe, mod3, ln_g, ln_b, alpha, tm=512):
    bsz, s, d = x1.shape
    row = pl.BlockSpec((1, tm, d), lambda b, i: (b, i, 0))
    vec = pl.BlockSpec((1, d), lambda b, i: (0, 0))
    return pl.pallas_call(
        functools.partial(_final_kernel, alpha=alpha),
        grid=(bsz, s // tm),
        in_specs=[row, row, pl.BlockSpec((1, 1, d), lambda b, i: (6 * b + 5, 0, 0)), vec, vec],
        out_specs=row,
        out_shape=jax.ShapeDtypeStruct((bsz, s, d), F32),
        compiler_params=_params(("parallel", "arbitrary")),
        name="final",
    )(x1, moe, mod3, ln_g, ln_b)


def _rope_tables(s):
    inv = ROPE_THETA ** (-jnp.arange(0, MLA_ROPE, 2, dtype=F32) / MLA_ROPE)
    ang = jnp.arange(s, dtype=F32)[:, None] * inv[None, :]
    cos, sin = jnp.cos(ang), jnp.sin(ang)
    scale = (MLA_NOPE + MLA_ROPE) ** -0.5 * math.log2(math.e)
    z32 = jnp.zeros((s, HEAD_SLAB - MLA_NOPE - MLA_ROPE), F32)
    z64 = jnp.zeros((s, MLA_NOPE), F32)
    cosq = jnp.concatenate([jnp.full((s, MLA_NOPE), scale, F32), cos * scale, cos * scale, z32], 1)
    sinq_scaled = jnp.concatenate([z64, sin * scale, sin * scale, z32], 1)
    cosk = jnp.concatenate([z64, cos, cos, z32], 1)
    sink = jnp.concatenate([z64, sin, sin, z32], 1)
    return cosq, sinq_scaled, cosk, sink


def _layout_weights(w_in, w_uq, w_ukv):
    d = w_in.shape[0]
    c0 = MLA_Q_LORA + MLA_KV_LORA
    kr = w_in[:, c0:c0 + MLA_ROPE]
    t1, t2 = kr[:, :HALF_ROPE], kr[:, HALF_ROPE:]
    z = lambda n: jnp.zeros((d, n), w_in.dtype)
    pad = HEAD_SLAB - MLA_NOPE - MLA_ROPE
    w_in_r = jnp.concatenate(
        [w_in[:, :c0], w_in[:, c0 + MLA_ROPE:],
         z(MLA_NOPE), t1, t2, z(pad),
         z(MLA_NOPE), -t2, t1, z(pad)], axis=1).astype(BF16)

    ql = w_uq.shape[0]
    wq = w_uq.reshape(ql, MLA_HEADS, MLA_NOPE + MLA_ROPE)
    qn, q1, q2 = wq[..., :MLA_NOPE], wq[..., MLA_NOPE:MLA_NOPE + HALF_ROPE], wq[..., MLA_NOPE + HALF_ROPE:]
    zq = lambda n: jnp.zeros((ql, MLA_HEADS, n), w_uq.dtype)
    wq_pre = jnp.concatenate([qn, q1, q2, zq(pad)], -1).reshape(ql, -1).astype(BF16)
    wq_sw = jnp.concatenate([zq(MLA_NOPE), -q2, q1, zq(pad)], -1).reshape(ql, -1).astype(BF16)

    kl = w_ukv.shape[0]
    wkv = w_ukv.reshape(kl, MLA_HEADS, MLA_NOPE + MLA_V)
    wk = jnp.concatenate([wkv[..., :MLA_NOPE],
                          jnp.zeros((kl, MLA_HEADS, HEAD_SLAB - MLA_NOPE), w_ukv.dtype)],
                         -1).reshape(kl, -1).astype(BF16)
    wv = jnp.concatenate([wkv[..., MLA_NOPE:],
                          jnp.zeros((kl, MLA_HEADS, HEAD_SLAB - MLA_V), w_ukv.dtype)],
                         -1).reshape(kl, -1).T.astype(BF16)
    return w_in_r, wq_pre, wq_sw, wk, wv


def kernel(x, c, w_ada, b_ada, w_in, q_norm_g, w_uq, kv_norm_g, w_ukv, rel_bias, w_out, ln1_g,
           ln1_b, w_router, w_gate, w_up, w_down, ln2_g, ln2_b):
    bsz, s, d = x.shape
    depth = w_ada.shape[0]
    alpha = (2 * depth) ** 0.25
    cap = max(1, EC_CAPACITY_FACTOR * s // N_EXPERTS)
    assert s % (max(2 * Q_TILE_DIL, DIL_COPY_ROWS) * max(dl for _, dl in DIL_PATTERNS)) == 0
    assert all(win // 2 // dl == BAND_HALF for win, dl in DIL_PATTERNS)

    cosq, sinq, cosk, sink = _rope_tables(s)
    bias = _bias_tiles(rel_bias)

    for l in range(depth):
        mod3 = _ada(c, w_ada[l], b_ada[l]).reshape(bsz * 6, 1, d)
        w_in_r, wq_pre, wq_sw, wk, wv = _layout_weights(w_in[l], w_uq[l], w_ukv[l])
        q, k, v, dq, dk, dv = _proj(
            x, mod3, w_in_r, q_norm_g[l].reshape(1, -1), wq_pre, wq_sw,
            kv_norm_g[l].reshape(1, -1), wk, wv, cosq, sinq, cosk, sink)
        mla = _mla(q, k, v)
        dil_out = _dilated(dq, dk, dv, bias)
        x1, h2, aff = _mix(x, mla, dil_out, mod3, w_out[l].astype(BF16),
                           ln1_g[l].reshape(1, d), ln1_b[l].reshape(1, d),
                           w_router[l].T, alpha)
        idx, vals = _topk(aff, cap)
        idx3 = idx.reshape(bsz * N_EXPERTS, 1, cap)
        val3 = vals.reshape(bsz * N_EXPERTS, 1, cap)
        xin = _gather(idx3, h2.reshape(bsz, s, 1, d), N_EXPERTS, cap)
        xin = xin.reshape(bsz, N_EXPERTS, cap, d).astype(BF16)
        y = _ffn(xin, w_gate[l].astype(BF16), w_up[l].astype(BF16), w_down[l].astype(BF16))
        moe = _combine(idx3, val3, y.reshape(bsz, N_EXPERTS, cap, 1, d), s)
        x = _final(x1, moe.reshape(bsz, s, d), mod3, ln2_g[l].reshape(1, d),
                   ln2_b[l].reshape(1, d), alpha)
    return x
```

```python
import functools
import math

import numpy as np
import jax
import jax.numpy as jnp
from jax import lax
from jax.experimental import pallas as pl
from jax.experimental.pallas import tpu as pltpu

MLA_HEADS = 8
MLA_NOPE = 64
MLA_ROPE = 32
MLA_V = 64
MLA_Q_LORA = 384
MLA_KV_LORA = 256
ROPE_THETA = 10000.0
DIL_HEADS = 8
DIL_HEAD_DIM = 64
DIL_PATTERNS = ((128, 1), (512, 4), (2048, 16))
REL_BUCKETS = 32
REL_MAX_EXACT = 8
REL_MAX_DIST = 1024
N_EXPERTS = 16
EC_CAPACITY_FACTOR = 2
NORM_EPS = 1e-6
NEG_INF = -1e30

LANES = 128
HEAD_SLAB = 128
VMEM_LIMIT = 48 * 1024 * 1024

F32 = jnp.float32
BF16 = jnp.bfloat16
HALF_ROPE = MLA_ROPE // 2
DIL_WIDTH = DIL_HEADS * DIL_HEAD_DIM
MLA_WIDTH = MLA_HEADS * MLA_V
BAND_HALF = 64
Q_TILE_DIL = 2 * BAND_HALF
K_WIN_DIL = 4 * BAND_HALF


def _params(sem, vmem=VMEM_LIMIT):
    return pltpu.CompilerParams(dimension_semantics=sem, vmem_limit_bytes=vmem)


def _dot(a, b):
    return jnp.dot(a, b, preferred_element_type=F32)


def _dot_nt(a, b):
    return lax.dot_general(a, b, (((1,), (1,)), ((), ())), preferred_element_type=F32)


def _layer_norm(y, g, b):
    mu = jnp.mean(y, axis=-1, keepdims=True)
    d = y - mu
    var = jnp.mean(d * d, axis=-1, keepdims=True)
    return d * lax.rsqrt(var + NORM_EPS) * g + b


def _ada_kernel(c_ref, w_ref, b_ref, o_ref):
    c = c_ref[...]
    s = c * (1.0 / (1.0 + jnp.exp(-c)))
    o_ref[...] = jnp.dot(s, w_ref[...], preferred_element_type=F32,
                         precision=lax.Precision.HIGHEST) + b_ref[...]


def _ada(c, w_ada, b_ada):
    bsz, d = c.shape
    n = w_ada.shape[1]
    tn = 1024
    return pl.pallas_call(
        _ada_kernel,
        grid=(n // tn,),
        in_specs=[pl.BlockSpec((bsz, d), lambda j: (0, 0)),
                  pl.BlockSpec((d, tn), lambda j: (0, j)),
                  pl.BlockSpec((1, tn), lambda j: (0, j))],
        out_specs=pl.BlockSpec((bsz, tn), lambda j: (0, j)),
        out_shape=jax.ShapeDtypeStruct((bsz, n), F32),
        compiler_params=_params(("arbitrary",)),
        name="ada",
    )(c, w_ada, b_ada.reshape(1, n))


_C_Q = 0
_C_KV = MLA_Q_LORA
_C_DQ = MLA_Q_LORA + MLA_KV_LORA
_C_DK = _C_DQ + DIL_WIDTH
_C_DV = _C_DK + DIL_WIDTH
_C_KR = _C_DV + DIL_WIDTH
_C_KRS = _C_KR + HEAD_SLAB
_C_END = _C_KRS + HEAD_SLAB


def _proj_kernel(x_ref, sc_ref, sh_ref, win_ref, gq_ref, wqp_ref, wqs_ref, gkv_ref,
                 wk_ref, wv_ref, cosq_ref, sinq_ref, cosk_ref, sink_ref,
                 q_ref, k_ref, v_ref, dq_ref, dk_ref, dv_ref):
    h = (x_ref[0] * (1.0 + sc_ref[0]) + sh_ref[0]).astype(BF16)
    proj = _dot(h, win_ref[...])

    c_q = proj[:, _C_Q:_C_KV]
    cqn = (c_q * lax.rsqrt(jnp.mean(c_q * c_q, axis=-1, keepdims=True) + NORM_EPS)
           * gq_ref[...]).astype(BF16)
    q_pre = _dot(cqn, wqp_ref[...])
    q_sw = _dot(cqn, wqs_ref[...])
    cosq = cosq_ref[...]
    sinq = sinq_ref[...]

    c_kv = proj[:, _C_KV:_C_DQ]
    ckvn = (c_kv * lax.rsqrt(jnp.mean(c_kv * c_kv, axis=-1, keepdims=True) + NORM_EPS)
            * gkv_ref[...]).astype(BF16)
    k_nope = _dot(ckvn, wk_ref[...])
    vt = _dot_nt(wv_ref[...], ckvn)
    vrow = lax.broadcasted_iota(jnp.int32, vt.shape, 0)
    vt = jnp.where((vrow & MLA_V) != 0, 1.0, vt).astype(BF16)
    v_ref[0] = vt.reshape(MLA_HEADS, HEAD_SLAB, vt.shape[1])
    k_rope = proj[:, _C_KR:_C_KRS] * cosk_ref[...] + proj[:, _C_KRS:_C_END] * sink_ref[...]

    for hd in range(MLA_HEADS):
        sl = slice(hd * HEAD_SLAB, (hd + 1) * HEAD_SLAB)
        q_ref[0, :, sl] = (q_pre[:, sl] * cosq + q_sw[:, sl] * sinq).astype(BF16)
        k_ref[0, :, sl] = (k_nope[:, sl] + k_rope).astype(BF16)

    dq_ref[0] = proj[:, _C_DQ:_C_DK] * (DIL_HEAD_DIM ** -0.5)
    dk_ref[0] = proj[:, _C_DK:_C_DV]
    dv_ref[0] = proj[:, _C_DV:_C_KR]


def _proj(x, mod3, w_in_r, gq, wq_pre, wq_sw, gkv, wk, wv, cosq, sinq, cosk, sink, tm=512):
    bsz, s, d = x.shape
    full = lambda a: pl.BlockSpec(a.shape, lambda b, i: (0,) * a.ndim)
    tab = pl.BlockSpec((tm, LANES), lambda b, i: (i, 0))
    hq = MLA_HEADS * HEAD_SLAB
    outs = [jax.ShapeDtypeStruct((bsz, s, hq), BF16),
            jax.ShapeDtypeStruct((bsz, s, hq), BF16),
            jax.ShapeDtypeStruct((bsz, MLA_HEADS, HEAD_SLAB, s), BF16),
            jax.ShapeDtypeStruct((bsz, s, DIL_WIDTH), F32),
            jax.ShapeDtypeStruct((bsz, s, DIL_WIDTH), F32),
            jax.ShapeDtypeStruct((bsz, s, DIL_WIDTH), F32)]
    ospec = lambda w: pl.BlockSpec((1, tm, w), lambda b, i: (b, i, 0))
    return pl.pallas_call(
        _proj_kernel,
        grid=(bsz, s // tm),
        in_specs=[pl.BlockSpec((1, tm, d), lambda b, i: (b, i, 0)),
                  pl.BlockSpec((1, 1, d), lambda b, i: (6 * b + 1, 0, 0)),
                  pl.BlockSpec((1, 1, d), lambda b, i: (6 * b + 0, 0, 0)),
                  full(w_in_r), full(gq), full(wq_pre), full(wq_sw), full(gkv),
                  full(wk), full(wv), tab, tab, tab, tab],
        out_specs=[ospec(hq), ospec(hq),
                   pl.BlockSpec((1, MLA_HEADS, HEAD_SLAB, tm), lambda b, i: (b, 0, 0, i)),
                   ospec(DIL_WIDTH),
                   ospec(DIL_WIDTH), ospec(DIL_WIDTH)],
        out_shape=outs,
        compiler_params=_params(("parallel", "arbitrary")),
        name="proj",
    )(x, mod3, mod3, w_in_r, gq, wq_pre, wq_sw, gkv, wk, wv, cosq, sinq, cosk, sink)


MLA_KEY_CHUNK = 512


MLA_Q_SUB = 256


def _mla_kernel(q_ref, k_ref, vt_ref, o_ref, s_ref, p_ref):
    tq = q_ref.shape[1]
    seq = k_ref.shape[1]
    nchunk = seq // MLA_KEY_CHUNK
    units = [(a, j) for a in range(tq // MLA_Q_SUB) for j in range(2)]
    n = len(units)
    maxima = [None] * n
    heads_out = {}

    def chunk(c):
        return slice(c * MLA_KEY_CHUNK, (c + 1) * MLA_KEY_CHUNK)

    def scores(u, c):
        a, j = units[u]
        sl = slice(j * HEAD_SLAB, (j + 1) * HEAD_SLAB)
        s = _dot_nt(k_ref[0, chunk(c), sl], q_ref[0, a * MLA_Q_SUB:(a + 1) * MLA_Q_SUB, sl])
        s_ref[u % 2, chunk(c), :] = s
        mc = jnp.max(s, axis=0, keepdims=True)
        maxima[u] = mc if maxima[u] is None else jnp.maximum(maxima[u], mc)

    def probs(u, c):
        p_ref[u % 2, chunk(c), :] = jnp.exp2(s_ref[u % 2, chunk(c), :] - maxima[u]).astype(BF16)

    def values(u):
        a, j = units[u]
        acc = _dot(vt_ref[0, j], p_ref[u % 2])
        heads_out[(a, j)] = acc[0:MLA_V] / acc[MLA_V:MLA_V + 1]
        if j == 1:
            o_t = jnp.concatenate([heads_out[(a, 0)], heads_out[(a, 1)]], axis=0)
            o_ref[0, a * MLA_Q_SUB:(a + 1) * MLA_Q_SUB, :] = o_t.T.astype(o_ref.dtype)

    for stage in range(n + 2):
        if 0 <= stage - 2 < n:
            values(stage - 2)
        for c in range(nchunk):
            if stage < n:
                scores(stage, c)
            if 0 <= stage - 1 < n:
                probs(stage - 1, c)


def _mla(q, k, vt, tq=1024):
    bsz, s, _ = q.shape
    pairs = MLA_HEADS // 2
    return pl.pallas_call(
        _mla_kernel,
        grid=(bsz, pairs, s // tq),
        in_specs=[pl.BlockSpec((1, tq, 2 * HEAD_SLAB), lambda b, p, i: (b, i, p)),
                  pl.BlockSpec((1, s, 2 * HEAD_SLAB), lambda b, p, i: (b, 0, p)),
                  pl.BlockSpec((1, 2, 2 * MLA_V, s), lambda b, p, i: (b, p, 0, 0))],
        out_specs=pl.BlockSpec((1, tq, 2 * MLA_V), lambda b, p, i: (b, i, p)),
        out_shape=jax.ShapeDtypeStruct((bsz, s, MLA_WIDTH), BF16),
        scratch_shapes=[pltpu.VMEM((2, s, MLA_Q_SUB), F32),
                        pltpu.VMEM((2, s, MLA_Q_SUB), BF16)],
        compiler_params=_params(("parallel", "arbitrary", "arbitrary")),
        name="mla",
    )(q, k, vt)


def _t5_bucket(rel):
    half = REL_BUCKETS // 2
    ret = (rel > 0).astype(np.int32) * half
    n = np.abs(rel)
    large = REL_MAX_EXACT + (np.log(np.maximum(n, 1) / REL_MAX_EXACT)
                             / np.log(REL_MAX_DIST / REL_MAX_EXACT)
                             * (half - REL_MAX_EXACT)).astype(np.int32)
    large = np.minimum(large, half - 1)
    return ret + np.where(n < REL_MAX_EXACT, n, large).astype(np.int32)


def _bucket_tiles():
    a = np.arange(Q_TILE_DIL)[:, None]
    j = np.arange(K_WIN_DIL)[None, :]
    rel = j - BAND_HALF - a
    in_range = [np.ones_like(j, bool), j >= BAND_HALF, j < K_WIN_DIL - BAND_HALF]
    tiles = []
    for _, dil in DIL_PATTERNS:
        for ok in in_range:
            tiles.append(np.where((np.abs(rel) <= BAND_HALF) & ok, _t5_bucket(rel * dil), -1))
    return np.stack(tiles).astype(np.int32)


N_EDGE = 3


def _bias_kernel(rb_ref, bucket_ref, o_ref):
    bucket = bucket_ref[0]
    accs = [jnp.where(bucket < 0, NEG_INF, 0.0).astype(F32) for _ in range(DIL_HEADS)]
    for bk in range(REL_BUCKETS):
        hit = bucket == bk
        for hd in range(DIL_HEADS):
            accs[hd] = jnp.where(hit, rb_ref[bk, hd], accs[hd])
    for hd in range(DIL_HEADS):
        o_ref[0, hd] = accs[hd]


def _bias_tiles(rel_bias):
    buckets = jnp.asarray(_bucket_tiles())
    n = buckets.shape[0]
    return pl.pallas_call(
        _bias_kernel,
        grid=(n,),
        in_specs=[pl.BlockSpec(memory_space=pltpu.SMEM),
                  pl.BlockSpec((1, Q_TILE_DIL, K_WIN_DIL), lambda p: (p, 0, 0))],
        out_specs=pl.BlockSpec((1, DIL_HEADS, Q_TILE_DIL, K_WIN_DIL), lambda p: (p, 0, 0, 0)),
        out_shape=jax.ShapeDtypeStruct((n, DIL_HEADS, Q_TILE_DIL, K_WIN_DIL), F32),
        compiler_params=_params(("arbitrary",)),
        name="dil_bias",
    )(rel_bias.astype(F32), buckets)


DIL_TILES_PER_ITER = 8
DIL_COPY_ROWS = 256
DIL_MERGE_ROWS = 256


def _dil_kernel(q_ref, k_ref, v_ref, bias_ref, o_ref,
                k0_ref, k1_ref, vp_ref, op_ref, lp_ref, *, seq):
    lane = lax.broadcasted_iota(jnp.int32, (1, LANES), 1)
    low = lane < DIL_HEAD_DIM
    zeros = jnp.zeros((BAND_HALF, LANES), BF16)

    def rows(start, size, dil):
        return pl.ds(start, size) if dil == 1 else pl.ds(start, size, stride=dil)

    for p, (_, dil) in enumerate(DIL_PATTERNS):
        length = seq // dil
        ntile = length // Q_TILE_DIL
        span = length + 2 * BAND_HALF

        def deinterleave(r, carry, dil=dil, length=length, span=span):
            base = pl.multiple_of(r * span, 2 * BAND_HALF)
            for ref in (k0_ref, k1_ref, vp_ref):
                ref[pl.ds(base, BAND_HALF), :] = zeros
                ref[pl.ds(base + BAND_HALF + length, BAND_HALF), :] = zeros
            for cidx in range(length // DIL_COPY_ROWS):
                src = rows(r + dil * DIL_COPY_ROWS * cidx, DIL_COPY_ROWS, dil)
                dst = pl.ds(pl.multiple_of(base + BAND_HALF + DIL_COPY_ROWS * cidx, BAND_HALF),
                            DIL_COPY_ROWS)
                kk = k_ref[0, src, :]
                k0_ref[dst, :] = jnp.where(low, kk, 0.0).astype(BF16)
                k1_ref[dst, :] = jnp.where(low, 0.0, kk).astype(BF16)
                vp_ref[dst, :] = v_ref[0, src, :].astype(BF16)
            return carry

        lax.fori_loop(0, dil, deinterleave, 0)

        def tiles(it, carry, p=p, dil=dil, ntile=ntile, span=span):
            for u in range(DIL_TILES_PER_ITER):
                t = it * DIL_TILES_PER_ITER + u
                r = t // ntile
                i = t % ntile
                edge = jnp.where(i == 0, 1, jnp.where(i == ntile - 1, 2, 0))
                r0 = i * Q_TILE_DIL
                tok = rows(r + dil * r0, Q_TILE_DIL, dil)
                win = pl.ds(pl.multiple_of(r * span + r0, Q_TILE_DIL), K_WIN_DIL)
                q = q_ref[0, tok, :].astype(BF16)
                vw = vp_ref[win, :]
                outs, lses = [], []
                for j, kref in enumerate((k0_ref, k1_ref)):
                    s = _dot_nt(q, kref[win, :]) + bias_ref[p * N_EDGE + edge, j]
                    m = jnp.max(s, axis=-1, keepdims=True)
                    e = jnp.exp(s - m)
                    den = jnp.sum(e, axis=-1, keepdims=True)
                    outs.append(_dot(e.astype(BF16), vw) / den)
                    lses.append(m + jnp.log(den))
                op_ref[p, tok, :] = jnp.where(low, outs[0], outs[1])
                lp_ref[p, tok, :] = jnp.where(low, lses[0], lses[1])
            return carry

        lax.fori_loop(0, seq // Q_TILE_DIL // DIL_TILES_PER_ITER, tiles, 0)

    def merge(g, carry):
        sl = pl.ds(pl.multiple_of(g * DIL_MERGE_ROWS, DIL_MERGE_ROWS), DIL_MERGE_ROWS)
        ls = [lp_ref[p, sl, :] for p in range(len(DIL_PATTERNS))]
        m = functools.reduce(jnp.maximum, ls)
        es = [jnp.exp(l - m) for l in ls]
        num = functools.reduce(lambda a, b: a + b,
                               [e * op_ref[p, sl, :] for p, e in enumerate(es)])
        den = functools.reduce(lambda a, b: a + b, es)
        o_ref[0, sl, :] = (num / den).astype(o_ref.dtype)
        return carry

    lax.fori_loop(0, seq // DIL_MERGE_ROWS, merge, 0)


def _dilated(dq, dk, dv, bias):
    bsz, s, w = dq.shape
    pairs = DIL_HEADS // 2
    npat = len(DIL_PATTERNS)
    blk = pl.BlockSpec((1, s, LANES), lambda b, p: (b, 0, p))
    pad_rows = s + 2 * BAND_HALF * max(dl for _, dl in DIL_PATTERNS)
    assert (s // Q_TILE_DIL) % DIL_TILES_PER_ITER == 0
    return pl.pallas_call(
        functools.partial(_dil_kernel, seq=s),
        grid=(bsz, pairs),
        in_specs=[blk, blk, blk,
                  pl.BlockSpec((npat * N_EDGE, 2, Q_TILE_DIL, K_WIN_DIL), lambda b, p: (0, p, 0, 0))],
        out_specs=blk,
        out_shape=jax.ShapeDtypeStruct((bsz, s, w), BF16),
        scratch_shapes=[pltpu.VMEM((pad_rows, LANES), BF16),
                        pltpu.VMEM((pad_rows, LANES), BF16),
                        pltpu.VMEM((pad_rows, LANES), BF16),
                        pltpu.VMEM((npat, s, LANES), F32),
                        pltpu.VMEM((npat, s, LANES), F32)],
        compiler_params=_params(("parallel", "arbitrary")),
        name="dilated",
    )(dq, dk, dv, bias)


def _mix_kernel(x_ref, mla_ref, dil_ref,
                g1_ref, sc2_ref, sh2_ref, wout_ref, lg_ref, lb_ref, wr_ref,
                x1_ref, h2_ref, aff_ref, *, alpha):
    mix = _dot(mla_ref[0], wout_ref[0:MLA_WIDTH, :]) + _dot(dil_ref[0], wout_ref[MLA_WIDTH:, :])
    y = alpha * x_ref[0] + g1_ref[0] * mix
    x1 = _layer_norm(y, lg_ref[...], lb_ref[...])
    x1_ref[0] = x1
    h2 = x1 * (1.0 + sc2_ref[0]) + sh2_ref[0]
    h2_ref[0] = h2.reshape(h2.shape[0], 1, h2.shape[1])
    logits = lax.dot_general(wr_ref[...], h2, (((1,), (1,)), ((), ())),
                             preferred_element_type=F32,
                             precision=lax.Precision.HIGHEST)
    mx = jnp.max(logits, axis=0, keepdims=True)
    ex = jnp.exp(logits - mx)
    aff_ref[0] = ex / jnp.sum(ex, axis=0, keepdims=True)


def _mix(x, mla, dil, mod3, w_out, ln_g, ln_b, w_router_t, alpha, tm=512):
    bsz, s, d = x.shape
    row = lambda w: pl.BlockSpec((1, tm, w), lambda b, i: (b, i, 0))
    modspec = lambda k: pl.BlockSpec((1, 1, d), lambda b, i: (6 * b + k, 0, 0))
    full = lambda a: pl.BlockSpec(a.shape, lambda b, i: (0,) * a.ndim)
    return pl.pallas_call(
        functools.partial(_mix_kernel, alpha=alpha),
        grid=(bsz, s // tm),
        in_specs=[row(d), row(MLA_WIDTH), row(DIL_WIDTH)]
                 + [modspec(2), modspec(4), modspec(3),
                    full(w_out), full(ln_g), full(ln_b), full(w_router_t)],
        out_specs=[row(d), pl.BlockSpec((1, tm, 1, d), lambda b, i: (b, i, 0, 0)),
                   pl.BlockSpec((1, N_EXPERTS, tm), lambda b, i: (b, 0, i))],
        out_shape=[jax.ShapeDtypeStruct((bsz, s, d), F32),
                   jax.ShapeDtypeStruct((bsz, s, 1, d), F32),
                   jax.ShapeDtypeStruct((bsz, N_EXPERTS, s), F32)],
        compiler_params=_params(("parallel", "arbitrary")),
        name="mix",
    )(x, mla, dil, mod3, mod3, mod3, w_out, ln_g, ln_b, w_router_t)


def _cumsum_lanes(x, tri):
    rows, n = x.shape
    carry = jnp.zeros((rows, 1), F32)
    parts = []
    for j in range(n // LANES):
        inc = _dot(x[:, j * LANES:(j + 1) * LANES].astype(BF16), tri) + carry
        parts.append(inc)
        carry = inc[:, LANES - 1:LANES]
    return jnp.concatenate(parts, axis=1)


def _topk_kernel(aff_ref, idx_ref, val_ref, key_ref, *, cap):
    a = aff_ref[0]
    n_e, s = a.shape
    bits = pltpu.bitcast(a, jnp.int32)
    t = jnp.zeros((n_e, 1), jnp.int32)
    for bit in range(30, -1, -1):
        cand = t | (1 << bit)
        cnt = jnp.sum((bits >= cand).astype(jnp.int32), axis=1, keepdims=True)
        t = jnp.where(cnt >= cap, cand, t)
    gt = bits > t
    eq = bits == t
    n_gt = jnp.sum(gt.astype(jnp.int32), axis=1, keepdims=True)
    ri = lax.broadcasted_iota(jnp.int32, (LANES, LANES), 0)
    ci = lax.broadcasted_iota(jnp.int32, (LANES, LANES), 1)
    tri = jnp.where(ri <= ci, 1.0, 0.0).astype(BF16)
    eq_f = jnp.where(eq, 1.0, 0.0)
    rank_eq = _cumsum_lanes(eq_f, tri) - eq_f
    sel = gt | (eq & (rank_eq < (cap - n_gt).astype(F32)))
    sel_f = jnp.where(sel, 1.0, 0.0)
    pos = _cumsum_lanes(sel_f, tri) - sel_f
    key_ref[...] = jnp.where(sel, pos.astype(jnp.int32), -1)

    tok = lax.broadcasted_iota(jnp.int32, (1, s), 1)
    tok_hi = (tok >> 6).astype(F32)
    tok_lo = (tok & 63).astype(F32)
    slot = lax.broadcasted_iota(jnp.int32, (cap, 1), 0)
    rid = lax.broadcasted_iota(jnp.int32, (8, s), 0)

    def per_expert(e, carry):
        key = key_ref[pl.ds(e, 1), :]
        onehot = jnp.where(key == slot, 1.0, 0.0).astype(BF16)
        ar = aff_ref[0, pl.ds(e, 1), :]
        a_hi = ar.astype(BF16).astype(F32)
        r1 = ar - a_hi
        a_mid = r1.astype(BF16).astype(F32)
        a_lo = r1 - a_mid
        lhs = jnp.where(rid == 0, tok_hi,
              jnp.where(rid == 1, tok_lo,
              jnp.where(rid == 2, a_hi,
              jnp.where(rid == 3, a_mid,
              jnp.where(rid == 4, a_lo, 0.0))))).astype(BF16)
        res = _dot_nt(lhs, onehot)
        idx_ref[0, pl.ds(e, 1), :] = (res[0:1] * 64.0 + res[1:2]).astype(jnp.int32)
        val_ref[0, pl.ds(e, 1), :] = res[2:3] + res[3:4] + res[4:5]
        return carry

    lax.fori_loop(0, n_e, per_expert, 0)


def _topk(aff, cap):
    bsz, n_e, s = aff.shape
    return pl.pallas_call(
        functools.partial(_topk_kernel, cap=cap),
        grid=(bsz,),
        in_specs=[pl.BlockSpec((1, n_e, s), lambda b: (b, 0, 0))],
        out_specs=[pl.BlockSpec((1, n_e, cap), lambda b: (b, 0, 0)),
                   pl.BlockSpec((1, n_e, cap), lambda b: (b, 0, 0))],
        out_shape=[jax.ShapeDtypeStruct((bsz, n_e, cap), jnp.int32),
                   jax.ShapeDtypeStruct((bsz, n_e, cap), F32)],
        scratch_shapes=[pltpu.VMEM((n_e, s), jnp.int32)],
        compiler_params=_params(("parallel",)),
        name="topk",
    )(aff)


GATHER_UNROLL = 8


def _gather_kernel(idx_ref, h_ref, o_ref, *, cap):
    def chunk(g, carry):
        base = g * GATHER_UNROLL
        for u in range(GATHER_UNROLL):
            c = base + u
            o_ref[0, 0, c] = h_ref[0, idx_ref[0, 0, c]]
        return carry

    lax.fori_loop(0, cap // GATHER_UNROLL, chunk, 0)


def _gather(idx3, h4, n_e, cap):
    bsz, s, _, d = h4.shape
    return pl.pallas_call(
        functools.partial(_gather_kernel, cap=cap),
        grid=(bsz, n_e),
        in_specs=[pl.BlockSpec((1, 1, cap), lambda b, e: (b * n_e + e, 0, 0),
                               memory_space=pltpu.SMEM),
                  pl.BlockSpec((1, s, 1, d), lambda b, e: (b, 0, 0, 0))],
        out_specs=pl.BlockSpec((1, 1, cap, 1, d), lambda b, e: (b, e, 0, 0, 0)),
        out_shape=jax.ShapeDtypeStruct((bsz, n_e, cap, 1, d), F32),
        compiler_params=_params(("parallel", "arbitrary")),
        name="gather",
    )(idx3, h4)


def _ffn_kernel(x_ref, wg_ref, wu_ref, wd_ref, y_ref):
    x = x_ref[0, 0]
    gt = _dot(x, wg_ref[0])
    up = _dot(x, wu_ref[0])
    act = (gt * (1.0 / (1.0 + jnp.exp(-gt))) * up).astype(BF16)
    y = _dot(act, wd_ref[0])
    y_ref[0, 0] = y.reshape(y.shape[0], 1, y.shape[1])


def _ffn(xin, wg, wu, wd):
    bsz, n_e, cap, d = xin.shape
    f = wg.shape[-1]
    return pl.pallas_call(
        _ffn_kernel,
        grid=(n_e, bsz),
        in_specs=[pl.BlockSpec((1, 1, cap, d), lambda e, b: (b, e, 0, 0)),
                  pl.BlockSpec((1, d, f), lambda e, b: (e, 0, 0)),
                  pl.BlockSpec((1, d, f), lambda e, b: (e, 0, 0)),
                  pl.BlockSpec((1, f, d), lambda e, b: (e, 0, 0))],
        out_specs=pl.BlockSpec((1, 1, cap, 1, d), lambda e, b: (b, e, 0, 0, 0)),
        out_shape=jax.ShapeDtypeStruct((bsz, n_e, cap, 1, d), F32),
        compiler_params=_params(("parallel", "arbitrary")),
        name="ffn",
    )(xin, wg, wu, wd)


SCATTER_UNROLL = 8


def _combine_kernel(idx_ref, val_ref, y_ref, o_ref, *, cap):
    @pl.when(pl.program_id(1) == 0)
    def _():
        o_ref[...] = jnp.zeros_like(o_ref)

    def chunk(g, carry):
        base = g * SCATTER_UNROLL
        rows = []
        for u in range(SCATTER_UNROLL):
            c = base + u
            t = idx_ref[0, 0, c]
            rows.append((t, o_ref[0, t] + val_ref[0, 0, c] * y_ref[0, 0, c]))
        for t, r in rows:
            o_ref[0, t] = r
        return carry

    lax.fori_loop(0, cap // SCATTER_UNROLL, chunk, 0)


def _combine(idx3, val3, y5, s):
    bsz, n_e, cap, _, d = y5.shape
    sm = lambda: pl.BlockSpec((1, 1, cap), lambda b, e: (b * n_e + e, 0, 0),
                              memory_space=pltpu.SMEM)
    return pl.pallas_call(
        functools.partial(_combine_kernel, cap=cap),
        grid=(bsz, n_e),
        in_specs=[sm(), sm(),
                  pl.BlockSpec((1, 1, cap, 1, d), lambda b, e: (b, e, 0, 0, 0))],
        out_specs=pl.BlockSpec((1, s, 1, d), lambda b, e: (b, 0, 0, 0)),
        out_shape=jax.ShapeDtypeStruct((bsz, s, 1, d), F32),
        compiler_params=_params(("parallel", "arbitrary")),
        name="combine",
    )(idx3, val3, y5)


def _final_kernel(x1_ref, moe_ref, g2_ref, lg_ref, lb_ref, o_ref, *, alpha):
    y = alpha * x1_ref[0] + g2_ref[0] * moe_ref[0]
    o_ref[0] = _layer_norm(y, lg_ref[...], lb_ref[...])


def _final(x1, moe, mod3, ln_g, ln_b, alpha, tm=512):
    bsz, s, d = x1.shape
    row = pl.BlockSpec((1, tm, d), lambda b, i: (b, i, 0))
    vec = pl.BlockSpec((1, d), lambda b, i: (0, 0))
    return pl.pallas_call(
        functools.partial(_final_kernel, alpha=alpha),
        grid=(bsz, s // tm),
        in_specs=[row, row, pl.BlockSpec((1, 1, d), lambda b, i: (6 * b + 5, 0, 0)), vec, vec],
        out_specs=row,
        out_shape=jax.ShapeDtypeStruct((bsz, s, d), F32),
        compiler_params=_params(("parallel", "arbitrary")),
        name="final",
    )(x1, moe, mod3, ln_g, ln_b)


def _rope_tables(s):
    inv = ROPE_THETA ** (-jnp.arange(0, MLA_ROPE, 2, dtype=F32) / MLA_ROPE)
    ang = jnp.arange(s, dtype=F32)[:, None] * inv[None, :]
    cos, sin = jnp.cos(ang), jnp.sin(ang)
    scale = (MLA_NOPE + MLA_ROPE) ** -0.5 * math.log2(math.e)
    z32 = jnp.zeros((s, HEAD_SLAB - MLA_NOPE - MLA_ROPE), F32)
    z64 = jnp.zeros((s, MLA_NOPE), F32)
    cosq = jnp.concatenate([jnp.full((s, MLA_NOPE), scale, F32), cos * scale, cos * scale, z32], 1)
    sinq_scaled = jnp.concatenate([z64, sin * scale, sin * scale, z32], 1)
    cosk = jnp.concatenate([z64, cos, cos, z32], 1)
    sink = jnp.concatenate([z64, sin, sin, z32], 1)
    return cosq, sinq_scaled, cosk, sink


def _layout_weights(w_in, w_uq, w_ukv):
    d = w_in.shape[0]
    c0 = MLA_Q_LORA + MLA_KV_LORA
    kr = w_in[:, c0:c0 + MLA_ROPE]
    t1, t2 = kr[:, :HALF_ROPE], kr[:, HALF_ROPE:]
    z = lambda n: jnp.zeros((d, n), w_in.dtype)
    pad = HEAD_SLAB - MLA_NOPE - MLA_ROPE
    w_in_r = jnp.concatenate(
        [w_in[:, :c0], w_in[:, c0 + MLA_ROPE:],
         z(MLA_NOPE), t1, t2, z(pad),
         z(MLA_NOPE), -t2, t1, z(pad)], axis=1).astype(BF16)

    ql = w_uq.shape[0]
    wq = w_uq.reshape(ql, MLA_HEADS, MLA_NOPE + MLA_ROPE)
    qn, q1, q2 = wq[..., :MLA_NOPE], wq[..., MLA_NOPE:MLA_NOPE + HALF_ROPE], wq[..., MLA_NOPE + HALF_ROPE:]
    zq = lambda n: jnp.zeros((ql, MLA_HEADS, n), w_uq.dtype)
    wq_pre = jnp.concatenate([qn, q1, q2, zq(pad)], -1).reshape(ql, -1).astype(BF16)
    wq_sw = jnp.concatenate([zq(MLA_NOPE), -q2, q1, zq(pad)], -1).reshape(ql, -1).astype(BF16)

    kl = w_ukv.shape[0]
    wkv = w_ukv.reshape(kl, MLA_HEADS, MLA_NOPE + MLA_V)
    wk = jnp.concatenate([wkv[..., :MLA_NOPE],
                          jnp.zeros((kl, MLA_HEADS, HEAD_SLAB - MLA_NOPE), w_ukv.dtype)],
                         -1).reshape(kl, -1).astype(BF16)
    wv = jnp.concatenate([wkv[..., MLA_NOPE:],
                          jnp.zeros((kl, MLA_HEADS, HEAD_SLAB - MLA_V), w_ukv.dtype)],
                         -1).reshape(kl, -1).T.astype(BF16)
    return w_in_r, wq_pre, wq_sw, wk, wv


def kernel(x, c, w_ada, b_ada, w_in, q_norm_g, w_uq, kv_norm_g, w_ukv, rel_bias, w_out, ln1_g,
           ln1_b, w_router, w_gate, w_up, w_down, ln2_g, ln2_b):
    bsz, s, d = x.shape
    depth = w_ada.shape[0]
    alpha = (2 * depth) ** 0.25
    cap = max(1, EC_CAPACITY_FACTOR * s // N_EXPERTS)
    assert s % (max(2 * Q_TILE_DIL, DIL_COPY_ROWS) * max(dl for _, dl in DIL_PATTERNS)) == 0
    assert all(win // 2 // dl == BAND_HALF for win, dl in DIL_PATTERNS)

    cosq, sinq, cosk, sink = _rope_tables(s)
    bias = _bias_tiles(rel_bias)

    for l in range(depth):
        mod3 = _ada(c, w_ada[l], b_ada[l]).reshape(bsz * 6, 1, d)
        w_in_r, wq_pre, wq_sw, wk, wv = _layout_weights(w_in[l], w_uq[l], w_ukv[l])
        q, k, v, dq, dk, dv = _proj(
            x, mod3, w_in_r, q_norm_g[l].reshape(1, -1), wq_pre, wq_sw,
            kv_norm_g[l].reshape(1, -1), wk, wv, cosq, sinq, cosk, sink)
        mla = _mla(q, k, v)
        dil_out = _dilated(dq, dk, dv, bias)
        x1, h2, aff = _mix(x, mla, dil_out, mod3, w_out[l].astype(BF16),
                           ln1_g[l].reshape(1, d), ln1_b[l].reshape(1, d),
                           w_router[l].T, alpha)
        idx, vals = _topk(aff, cap)
        idx3 = idx.reshape(bsz * N_EXPERTS, 1, cap)
        val3 = vals.reshape(bsz * N_EXPERTS, 1, cap)
        xin = _gather(idx3, h2, N_EXPERTS, cap)
        xin = xin.reshape(bsz, N_EXPERTS, cap, d).astype(BF16)
        y = _ffn(xin, w_gate[l].astype(BF16), w_up[l].astype(BF16), w_down[l].astype(BF16))
        moe = _combine(idx3, val3, y, s)
        x = _final(x1, moe.reshape(bsz, s, d), mod3, ln2_g[l].reshape(1, d),
                   ln2_b[l].reshape(1, d), alpha)
    return x
```

```python
import functools
import math

import numpy as np
import jax
import jax.numpy as jnp
from jax import lax
from jax.experimental import pallas as pl
from jax.experimental.pallas import tpu as pltpu

MLA_HEADS = 8
MLA_NOPE = 64
MLA_ROPE = 32
MLA_V = 64
MLA_Q_LORA = 384
MLA_KV_LORA = 256
ROPE_THETA = 10000.0
DIL_HEADS = 8
DIL_HEAD_DIM = 64
DIL_PATTERNS = ((128, 1), (512, 4), (2048, 16))
REL_BUCKETS = 32
REL_MAX_EXACT = 8
REL_MAX_DIST = 1024
N_EXPERTS = 16
EC_CAPACITY_FACTOR = 2
NORM_EPS = 1e-6
NEG_INF = -1e30

LANES = 128
HEAD_SLAB = 128
VMEM_LIMIT = 48 * 1024 * 1024
F32 = jnp.float32
BF16 = jnp.bfloat16
HALF_ROPE = MLA_ROPE // 2
DIL_WIDTH = DIL_HEADS * DIL_HEAD_DIM
MLA_WIDTH = MLA_HEADS * MLA_V
BAND_HALF = 64
Q_TILE_DIL = 2 * BAND_HALF
K_WIN_DIL = 4 * BAND_HALF


def _params(sem, vmem=VMEM_LIMIT):
    return pltpu.CompilerParams(dimension_semantics=sem, vmem_limit_bytes=vmem)


def _dot(a, b):
    return jnp.dot(a, b, preferred_element_type=F32)


def _dot_nt(a, b):
    return lax.dot_general(a, b, (((1,), (1,)), ((), ())), preferred_element_type=F32)


def _layer_norm(y, g, b):
    mu = jnp.mean(y, axis=-1, keepdims=True)
    d = y - mu
    var = jnp.mean(d * d, axis=-1, keepdims=True)
    return d * lax.rsqrt(var + NORM_EPS) * g + b


def _ada_kernel(c_ref, w_ref, b_ref, o_ref):
    c = c_ref[...]
    s = c * (1.0 / (1.0 + jnp.exp(-c)))
    o_ref[...] = jnp.dot(s, w_ref[...], preferred_element_type=F32,
                         precision=lax.Precision.HIGHEST) + b_ref[...]


def _ada(c, w_ada, b_ada):
    bsz, d = c.shape
    n = w_ada.shape[1]
    tn = 1024
    return pl.pallas_call(
        _ada_kernel,
        grid=(n // tn,),
        in_specs=[pl.BlockSpec((bsz, d), lambda j: (0, 0)),
                  pl.BlockSpec((d, tn), lambda j: (0, j)),
                  pl.BlockSpec((1, tn), lambda j: (0, j))],
        out_specs=pl.BlockSpec((bsz, tn), lambda j: (0, j)),
        out_shape=jax.ShapeDtypeStruct((bsz, n), F32),
        compiler_params=_params(("arbitrary",)),
        name="ada",
    )(c, w_ada, b_ada.reshape(1, n))


_C_Q = 0
_C_KV = MLA_Q_LORA
_C_DQ = MLA_Q_LORA + MLA_KV_LORA
_C_DK = _C_DQ + DIL_WIDTH
_C_DV = _C_DK + DIL_WIDTH
_C_KR = _C_DV + DIL_WIDTH
_C_KRS = _C_KR + HEAD_SLAB
_C_END = _C_KRS + HEAD_SLAB


def _proj_kernel(x_ref, sc_ref, sh_ref, win_ref, gq_ref, wqp_ref, wqs_ref, gkv_ref,
                 wk_ref, wv_ref, cosq_ref, sinq_ref, cosk_ref, sink_ref,
                 q_ref, k_ref, v_ref, dq_ref, dk_ref, dv_ref):
    h = (x_ref[0] * (1.0 + sc_ref[0]) + sh_ref[0]).astype(BF16)
    proj = _dot(h, win_ref[...])

    c_q = proj[:, _C_Q:_C_KV]
    cqn = (c_q * lax.rsqrt(jnp.mean(c_q * c_q, axis=-1, keepdims=True) + NORM_EPS)
           * gq_ref[...]).astype(BF16)
    q_pre = _dot(cqn, wqp_ref[...])
    q_sw = _dot(cqn, wqs_ref[...])
    cosq = cosq_ref[...]
    sinq = sinq_ref[...]

    c_kv = proj[:, _C_KV:_C_DQ]
    ckvn = (c_kv * lax.rsqrt(jnp.mean(c_kv * c_kv, axis=-1, keepdims=True) + NORM_EPS)
            * gkv_ref[...]).astype(BF16)
    k_nope = _dot(ckvn, wk_ref[...])
    vt = _dot_nt(wv_ref[...], ckvn)
    vrow = lax.broadcasted_iota(jnp.int32, vt.shape, 0)
    vt = jnp.where((vrow & MLA_V) != 0, 1.0, vt).astype(BF16)
    v_ref[0] = vt.reshape(MLA_HEADS, HEAD_SLAB, vt.shape[1])
    k_rope = proj[:, _C_KR:_C_KRS] * cosk_ref[...] + proj[:, _C_KRS:_C_END] * sink_ref[...]

    for hd in range(MLA_HEADS):
        sl = slice(hd * HEAD_SLAB, (hd + 1) * HEAD_SLAB)
        q_ref[0, :, sl] = (q_pre[:, sl] * cosq + q_sw[:, sl] * sinq).astype(BF16)
        k_ref[0, :, sl] = (k_nope[:, sl] + k_rope).astype(BF16)

    dq_ref[0] = proj[:, _C_DQ:_C_DK] * (DIL_HEAD_DIM ** -0.5)
    dk_ref[0] = proj[:, _C_DK:_C_DV]
    dv_ref[0] = proj[:, _C_DV:_C_KR]


def _proj(x, mod3, w_in_r, gq, wq_pre, wq_sw, gkv, wk, wv, cosq, sinq, cosk, sink, tm=512):
    bsz, s, d = x.shape
    full = lambda a: pl.BlockSpec(a.shape, lambda b, i: (0,) * a.ndim)
    tab = pl.BlockSpec((tm, LANES), lambda b, i: (i, 0))
    hq = MLA_HEADS * HEAD_SLAB
    outs = [jax.ShapeDtypeStruct((bsz, s, hq), BF16),
            jax.ShapeDtypeStruct((bsz, s, hq), BF16),
            jax.ShapeDtypeStruct((bsz, MLA_HEADS, HEAD_SLAB, s), BF16),
            jax.ShapeDtypeStruct((bsz, s, DIL_WIDTH), F32),
            jax.ShapeDtypeStruct((bsz, s, DIL_WIDTH), F32),
            jax.ShapeDtypeStruct((bsz, s, DIL_WIDTH), F32)]
    ospec = lambda w: pl.BlockSpec((1, tm, w), lambda b, i: (b, i, 0))
    return pl.pallas_call(
        _proj_kernel,
        grid=(bsz, s // tm),
        in_specs=[pl.BlockSpec((1, tm, d), lambda b, i: (b, i, 0)),
                  pl.BlockSpec((1, 1, d), lambda b, i: (6 * b + 1, 0, 0)),
                  pl.BlockSpec((1, 1, d), lambda b, i: (6 * b + 0, 0, 0)),
                  full(w_in_r), full(gq), full(wq_pre), full(wq_sw), full(gkv),
                  full(wk), full(wv), tab, tab, tab, tab],
        out_specs=[ospec(hq), ospec(hq),
                   pl.BlockSpec((1, MLA_HEADS, HEAD_SLAB, tm), lambda b, i: (b, 0, 0, i)),
                   ospec(DIL_WIDTH),
                   ospec(DIL_WIDTH), ospec(DIL_WIDTH)],
        out_shape=outs,
        compiler_params=_params(("parallel", "arbitrary")),
        name="proj",
    )(x, mod3, mod3, w_in_r, gq, wq_pre, wq_sw, gkv, wk, wv, cosq, sinq, cosk, sink)


MLA_KEY_CHUNK = 512


MLA_Q_SUB = 256


def _mla_kernel(q_ref, k_ref, vt_ref, o_ref, s_ref, p_ref):
    tq = q_ref.shape[1]
    seq = k_ref.shape[1]
    nchunk = seq // MLA_KEY_CHUNK
    units = [(a, j) for a in range(tq // MLA_Q_SUB) for j in range(2)]
    n = len(units)
    maxima = [None] * n
    heads_out = {}

    def chunk(c):
        return slice(c * MLA_KEY_CHUNK, (c + 1) * MLA_KEY_CHUNK)

    def scores(u, c):
        a, j = units[u]
        sl = slice(j * HEAD_SLAB, (j + 1) * HEAD_SLAB)
        s = _dot_nt(k_ref[0, chunk(c), sl], q_ref[0, a * MLA_Q_SUB:(a + 1) * MLA_Q_SUB, sl])
        s_ref[u % 2, chunk(c), :] = s
        mc = jnp.max(s, axis=0, keepdims=True)
        maxima[u] = mc if maxima[u] is None else jnp.maximum(maxima[u], mc)

    def probs(u, c):
        p_ref[u % 2, chunk(c), :] = jnp.exp2(s_ref[u % 2, chunk(c), :] - maxima[u]).astype(BF16)

    def values(u):
        a, j = units[u]
        acc = _dot(vt_ref[0, j], p_ref[u % 2])
        heads_out[(a, j)] = acc[0:MLA_V] / acc[MLA_V:MLA_V + 1]
        if j == 1:
            o_t = jnp.concatenate([heads_out[(a, 0)], heads_out[(a, 1)]], axis=0)
            o_ref[0, a * MLA_Q_SUB:(a + 1) * MLA_Q_SUB, :] = o_t.T.astype(o_ref.dtype)

    for stage in range(n + 2):
        if 0 <= stage - 2 < n:
            values(stage - 2)
        for c in range(nchunk):
            if stage < n:
                scores(stage, c)
            if 0 <= stage - 1 < n:
                probs(stage - 1, c)


def _mla(q, k, vt, tq=1024):
    bsz, s, _ = q.shape
    pairs = MLA_HEADS // 2
    return pl.pallas_call(
        _mla_kernel,
        grid=(bsz, pairs, s // tq),
        in_specs=[pl.BlockSpec((1, tq, 2 * HEAD_SLAB), lambda b, p, i: (b, i, p)),
                  pl.BlockSpec((1, s, 2 * HEAD_SLAB), lambda b, p, i: (b, 0, p)),
                  pl.BlockSpec((1, 2, 2 * MLA_V, s), lambda b, p, i: (b, p, 0, 0))],
        out_specs=pl.BlockSpec((1, tq, 2 * MLA_V), lambda b, p, i: (b, i, p)),
        out_shape=jax.ShapeDtypeStruct((bsz, s, MLA_WIDTH), BF16),
        scratch_shapes=[pltpu.VMEM((2, s, MLA_Q_SUB), F32),
                        pltpu.VMEM((2, s, MLA_Q_SUB), BF16)],
        compiler_params=_params(("parallel", "arbitrary", "arbitrary")),
        name="mla",
    )(q, k, vt)


def _t5_bucket(rel):
    half = REL_BUCKETS // 2
    ret = (rel > 0).astype(np.int32) * half
    n = np.abs(rel)
    large = REL_MAX_EXACT + (np.log(np.maximum(n, 1) / REL_MAX_EXACT)
                             / np.log(REL_MAX_DIST / REL_MAX_EXACT)
                             * (half - REL_MAX_EXACT)).astype(np.int32)
    large = np.minimum(large, half - 1)
    return ret + np.where(n < REL_MAX_EXACT, n, large).astype(np.int32)


def _bucket_tiles():
    a = np.arange(Q_TILE_DIL)[:, None]
    j = np.arange(K_WIN_DIL)[None, :]
    rel = j - BAND_HALF - a
    in_range = [np.ones_like(j, bool), j >= BAND_HALF, j < K_WIN_DIL - BAND_HALF]
    tiles = []
    for _, dil in DIL_PATTERNS:
        for ok in in_range:
            tiles.append(np.where((np.abs(rel) <= BAND_HALF) & ok, _t5_bucket(rel * dil), -1))
    return np.stack(tiles).astype(np.int32)


N_EDGE = 3


def _bias_kernel(rb_ref, bucket_ref, o_ref):
    bucket = bucket_ref[0]
    accs = [jnp.where(bucket < 0, NEG_INF, 0.0).astype(F32) for _ in range(DIL_HEADS)]
    for bk in range(REL_BUCKETS):
        hit = bucket == bk
        for hd in range(DIL_HEADS):
            accs[hd] = jnp.where(hit, rb_ref[bk, hd], accs[hd])
    for hd in range(DIL_HEADS):
        o_ref[0, hd] = accs[hd]


def _bias_tiles(rel_bias):
    buckets = jnp.asarray(_bucket_tiles())
    n = buckets.shape[0]
    return pl.pallas_call(
        _bias_kernel,
        grid=(n,),
        in_specs=[pl.BlockSpec(memory_space=pltpu.SMEM),
                  pl.BlockSpec((1, Q_TILE_DIL, K_WIN_DIL), lambda p: (p, 0, 0))],
        out_specs=pl.BlockSpec((1, DIL_HEADS, Q_TILE_DIL, K_WIN_DIL), lambda p: (p, 0, 0, 0)),
        out_shape=jax.ShapeDtypeStruct((n, DIL_HEADS, Q_TILE_DIL, K_WIN_DIL), F32),
        compiler_params=_params(("arbitrary",)),
        name="dil_bias",
    )(rel_bias.astype(F32), buckets)


DIL_TILES_PER_ITER = 8
DIL_COPY_ROWS = 256
DIL_MERGE_ROWS = 256


def _dil_kernel(q_ref, k_ref, v_ref, bias_ref, o_ref,
                k0_ref, k1_ref, vp_ref, op_ref, lp_ref, *, seq):
    lane = lax.broadcasted_iota(jnp.int32, (1, LANES), 1)
    low = lane < DIL_HEAD_DIM
    zeros = jnp.zeros((BAND_HALF, LANES), BF16)

    def rows(start, size, dil):
        return pl.ds(start, size) if dil == 1 else pl.ds(start, size, stride=dil)

    for p, (_, dil) in enumerate(DIL_PATTERNS):
        length = seq // dil
        ntile = length // Q_TILE_DIL
        span = length + 2 * BAND_HALF

        def deinterleave(r, carry, dil=dil, length=length, span=span):
            base = pl.multiple_of(r * span, 2 * BAND_HALF)
            for ref in (k0_ref, k1_ref, vp_ref):
                ref[pl.ds(base, BAND_HALF), :] = zeros
                ref[pl.ds(base + BAND_HALF + length, BAND_HALF), :] = zeros
            for cidx in range(length // DIL_COPY_ROWS):
                src = rows(r + dil * DIL_COPY_ROWS * cidx, DIL_COPY_ROWS, dil)
                dst = pl.ds(pl.multiple_of(base + BAND_HALF + DIL_COPY_ROWS * cidx, BAND_HALF),
                            DIL_COPY_ROWS)
                kk = k_ref[0, src, :]
                k0_ref[dst, :] = jnp.where(low, kk, 0.0).astype(BF16)
                k1_ref[dst, :] = jnp.where(low, 0.0, kk).astype(BF16)
                vp_ref[dst, :] = v_ref[0, src, :].astype(BF16)
            return carry

        lax.fori_loop(0, dil, deinterleave, 0)

        def tiles(it, carry, p=p, dil=dil, ntile=ntile, span=span):
            for u in range(DIL_TILES_PER_ITER):
                t = it * DIL_TILES_PER_ITER + u
                r = t // ntile
                i = t % ntile
                edge = jnp.where(i == 0, 1, jnp.where(i == ntile - 1, 2, 0))
                r0 = i * Q_TILE_DIL
                tok = rows(r + dil * r0, Q_TILE_DIL, dil)
                win = pl.ds(pl.multiple_of(r * span + r0, Q_TILE_DIL), K_WIN_DIL)
                q = q_ref[0, tok, :].astype(BF16)
                vw = vp_ref[win, :]
                outs, lses = [], []
                for j, kref in enumerate((k0_ref, k1_ref)):
                    s = _dot_nt(q, kref[win, :]) + bias_ref[p * N_EDGE + edge, j]
                    m = jnp.max(s, axis=-1, keepdims=True)
                    e = jnp.exp(s - m)
                    den = jnp.sum(e, axis=-1, keepdims=True)
                    outs.append(_dot(e.astype(BF16), vw) / den)
                    lses.append(m + jnp.log(den))
                op_ref[p, tok, :] = jnp.where(low, outs[0], outs[1])
                lp_ref[p, tok, :] = jnp.where(low, lses[0], lses[1])
            return carry

        lax.fori_loop(0, seq // Q_TILE_DIL // DIL_TILES_PER_ITER, tiles, 0)

    def merge(g, carry):
        sl = pl.ds(pl.multiple_of(g * DIL_MERGE_ROWS, DIL_MERGE_ROWS), DIL_MERGE_ROWS)
        ls = [lp_ref[p, sl, :] for p in range(len(DIL_PATTERNS))]
        m = functools.reduce(jnp.maximum, ls)
        es = [jnp.exp(l - m) for l in ls]
        num = functools.reduce(lambda a, b: a + b,
                               [e * op_ref[p, sl, :] for p, e in enumerate(es)])
        den = functools.reduce(lambda a, b: a + b, es)
        o_ref[0, sl, :] = (num / den).astype(o_ref.dtype)
        return carry

    lax.fori_loop(0, seq // DIL_MERGE_ROWS, merge, 0)


def _dilated(dq, dk, dv, bias):
    bsz, s, w = dq.shape
    pairs = DIL_HEADS // 2
    npat = len(DIL_PATTERNS)
    blk = pl.BlockSpec((1, s, LANES), lambda b, p: (b, 0, p))
    pad_rows = s + 2 * BAND_HALF * max(dl for _, dl in DIL_PATTERNS)
    assert (s // Q_TILE_DIL) % DIL_TILES_PER_ITER == 0
    return pl.pallas_call(
        functools.partial(_dil_kernel, seq=s),
        grid=(bsz, pairs),
        in_specs=[blk, blk, blk,
                  pl.BlockSpec((npat * N_EDGE, 2, Q_TILE_DIL, K_WIN_DIL), lambda b, p: (0, p, 0, 0))],
        out_specs=blk,
        out_shape=jax.ShapeDtypeStruct((bsz, s, w), BF16),
        scratch_shapes=[pltpu.VMEM((pad_rows, LANES), BF16)] * 3
                       + [pltpu.VMEM((npat, s, LANES), F32)] * 2,
        compiler_params=_params(("parallel", "arbitrary")),
        name="dilated",
    )(dq, dk, dv, bias)


def _mix_kernel(x_ref, mla_ref, dil_ref,
                g1_ref, sc2_ref, sh2_ref, wout_ref, lg_ref, lb_ref, wr_ref,
                x1_ref, h2_ref, aff_ref, *, alpha):
    mix = _dot(mla_ref[0], wout_ref[0:MLA_WIDTH, :]) + _dot(dil_ref[0], wout_ref[MLA_WIDTH:, :])
    y = alpha * x_ref[0] + g1_ref[0] * mix
    x1 = _layer_norm(y, lg_ref[...], lb_ref[...])
    x1_ref[0] = x1
    h2 = x1 * (1.0 + sc2_ref[0]) + sh2_ref[0]
    h2_ref[0] = h2.reshape(h2.shape[0], 1, h2.shape[1])
    logits = lax.dot_general(wr_ref[...], h2, (((1,), (1,)), ((), ())),
                             preferred_element_type=F32,
                             precision=lax.Precision.HIGHEST)
    mx = jnp.max(logits, axis=0, keepdims=True)
    ex = jnp.exp(logits - mx)
    aff_ref[0] = ex / jnp.sum(ex, axis=0, keepdims=True)


def _mix(x, mla, dil, mod3, w_out, ln_g, ln_b, w_router_t, alpha, tm=512):
    bsz, s, d = x.shape
    row = lambda w: pl.BlockSpec((1, tm, w), lambda b, i: (b, i, 0))
    modspec = lambda k: pl.BlockSpec((1, 1, d), lambda b, i: (6 * b + k, 0, 0))
    full = lambda a: pl.BlockSpec(a.shape, lambda b, i: (0,) * a.ndim)
    return pl.pallas_call(
        functools.partial(_mix_kernel, alpha=alpha),
        grid=(bsz, s // tm),
        in_specs=[row(d), row(MLA_WIDTH), row(DIL_WIDTH)]
                 + [modspec(2), modspec(4), modspec(3),
                    full(w_out), full(ln_g), full(ln_b), full(w_router_t)],
        out_specs=[row(d), pl.BlockSpec((1, tm, 1, d), lambda b, i: (b, i, 0, 0)),
                   pl.BlockSpec((1, N_EXPERTS, tm), lambda b, i: (b, 0, i))],
        out_shape=[jax.ShapeDtypeStruct((bsz, s, d), F32),
                   jax.ShapeDtypeStruct((bsz, s, 1, d), F32),
                   jax.ShapeDtypeStruct((bsz, N_EXPERTS, s), F32)],
        compiler_params=_params(("parallel", "arbitrary")),
        name="mix",
    )(x, mla, dil, mod3, mod3, mod3, w_out, ln_g, ln_b, w_router_t)


def _cumsum_lanes(x, tri):
    rows, n = x.shape
    carry = jnp.zeros((rows, 1), F32)
    parts = []
    for j in range(n // LANES):
        inc = _dot(x[:, j * LANES:(j + 1) * LANES].astype(BF16), tri) + carry
        parts.append(inc)
        carry = inc[:, LANES - 1:LANES]
    return jnp.concatenate(parts, axis=1)


TOPK_EXPERTS_PER_ITER = 2


def _topk_kernel(aff_ref, idx_ref, val_ref, key_ref, *, cap):
    a = aff_ref[0]
    n_e, s = a.shape
    bits = pltpu.bitcast(a, jnp.int32)
    t = jnp.zeros((n_e, 1), jnp.int32)
    for bit in range(30, -1, -1):
        cand = t | (1 << bit)
        cnt = jnp.sum((bits >= cand).astype(jnp.int32), axis=1, keepdims=True)
        t = jnp.where(cnt >= cap, cand, t)
    gt = bits > t
    eq = bits == t
    n_gt = jnp.sum(gt.astype(jnp.int32), axis=1, keepdims=True)
    ri = lax.broadcasted_iota(jnp.int32, (LANES, LANES), 0)
    ci = lax.broadcasted_iota(jnp.int32, (LANES, LANES), 1)
    tri = jnp.where(ri <= ci, 1.0, 0.0).astype(BF16)
    eq_f = jnp.where(eq, 1.0, 0.0)
    rank_eq = _cumsum_lanes(eq_f, tri) - eq_f
    sel = gt | (eq & (rank_eq < (cap - n_gt).astype(F32)))
    sel_f = jnp.where(sel, 1.0, 0.0)
    pos = _cumsum_lanes(sel_f, tri) - sel_f
    key_ref[...] = jnp.where(sel, pos.astype(jnp.int32), -1)

    tok = lax.broadcasted_iota(jnp.int32, (1, s), 1)
    tok_hi = (tok >> 6).astype(F32)
    tok_lo = (tok & 63).astype(F32)
    slot = lax.broadcasted_iota(jnp.int32, (cap, 1), 0)
    rid = lax.broadcasted_iota(jnp.int32, (8, s), 0)

    def per_expert(e):
        key = key_ref[pl.ds(e, 1), :]
        onehot = jnp.where(key == slot, 1.0, 0.0).astype(BF16)
        ar = aff_ref[0, pl.ds(e, 1), :]
        a_hi = ar.astype(BF16).astype(F32)
        r1 = ar - a_hi
        a_mid = r1.astype(BF16).astype(F32)
        a_lo = r1 - a_mid
        lhs = jnp.where(rid == 0, tok_hi,
              jnp.where(rid == 1, tok_lo,
              jnp.where(rid == 2, a_hi,
              jnp.where(rid == 3, a_mid,
              jnp.where(rid == 4, a_lo, 0.0))))).astype(BF16)
        res = _dot_nt(lhs, onehot)
        idx_ref[0, pl.ds(e, 1), :] = (res[0:1] * 64.0 + res[1:2]).astype(jnp.int32)
        val_ref[0, pl.ds(e, 1), :] = res[2:3] + res[3:4] + res[4:5]

    def expert_group(g, carry):
        for u in range(TOPK_EXPERTS_PER_ITER):
            per_expert(g * TOPK_EXPERTS_PER_ITER + u)
        return carry

    lax.fori_loop(0, n_e // TOPK_EXPERTS_PER_ITER, expert_group, 0)


def _topk(aff, cap):
    bsz, n_e, s = aff.shape
    return pl.pallas_call(
        functools.partial(_topk_kernel, cap=cap),
        grid=(bsz,),
        in_specs=[pl.BlockSpec((1, n_e, s), lambda b: (b, 0, 0))],
        out_specs=[pl.BlockSpec((1, n_e, cap), lambda b: (b, 0, 0)),
                   pl.BlockSpec((1, n_e, cap), lambda b: (b, 0, 0))],
        out_shape=[jax.ShapeDtypeStruct((bsz, n_e, cap), jnp.int32),
                   jax.ShapeDtypeStruct((bsz, n_e, cap), F32)],
        scratch_shapes=[pltpu.VMEM((n_e, s), jnp.int32)],
        compiler_params=_params(("parallel",)),
        name="topk",
    )(aff)


GATHER_UNROLL = 8


def _gather_kernel(idx_ref, h_ref, o_ref, *, cap):
    def chunk(g, carry):
        base = g * GATHER_UNROLL
        for u in range(GATHER_UNROLL):
            c = base + u
            o_ref[0, 0, c] = h_ref[0, idx_ref[0, 0, c]]
        return carry

    lax.fori_loop(0, cap // GATHER_UNROLL, chunk, 0)


def _gather(idx3, h4, n_e, cap):
    bsz, s, _, d = h4.shape
    return pl.pallas_call(
        functools.partial(_gather_kernel, cap=cap),
        grid=(bsz, n_e),
        in_specs=[pl.BlockSpec((1, 1, cap), lambda b, e: (b * n_e + e, 0, 0),
                               memory_space=pltpu.SMEM),
                  pl.BlockSpec((1, s, 1, d), lambda b, e: (b, 0, 0, 0))],
        out_specs=pl.BlockSpec((1, 1, cap, 1, d), lambda b, e: (b, e, 0, 0, 0)),
        out_shape=jax.ShapeDtypeStruct((bsz, n_e, cap, 1, d), F32),
        compiler_params=_params(("parallel", "arbitrary")),
        name="gather",
    )(idx3, h4)


def _ffn_kernel(x_ref, wg_ref, wu_ref, wd_ref, y_ref, wg_bf, wu_bf, wd_bf):
    @pl.when(pl.program_id(1) == 0)
    def _():
        wg_bf[...] = wg_ref[0].astype(BF16)
        wu_bf[...] = wu_ref[0].astype(BF16)
        wd_bf[...] = wd_ref[0].astype(BF16)

    x = x_ref[0, 0]
    gt = _dot(x, wg_bf[...])
    up = _dot(x, wu_bf[...])
    act = (gt * (1.0 / (1.0 + jnp.exp(-gt))) * up).astype(BF16)
    y = _dot(act, wd_bf[...])
    y_ref[0, 0] = y.reshape(y.shape[0], 1, y.shape[1])


def _ffn(xin, wg, wu, wd):
    bsz, n_e, cap, d = xin.shape
    f = wg.shape[-1]
    wspec = lambda r, c: pl.BlockSpec((1, r, c), lambda e, b: (e, 0, 0), pipeline_mode=pl.Buffered(1))
    return pl.pallas_call(
        _ffn_kernel,
        grid=(n_e, bsz),
        in_specs=[pl.BlockSpec((1, 1, cap, d), lambda e, b: (b, e, 0, 0)),
                  wspec(d, f), wspec(d, f), wspec(f, d)],
        out_specs=pl.BlockSpec((1, 1, cap, 1, d), lambda e, b: (b, e, 0, 0, 0)),
        out_shape=jax.ShapeDtypeStruct((bsz, n_e, cap, 1, d), F32),
        scratch_shapes=[pltpu.VMEM((d, f), BF16), pltpu.VMEM((d, f), BF16), pltpu.VMEM((f, d), BF16)],
        compiler_params=_params(("arbitrary", "arbitrary")),
        name="ffn",
    )(xin, wg, wu, wd)


SCATTER_UNROLL = 8


def _combine_kernel(idx_ref, val_ref, y_ref, o_ref, *, cap):
    @pl.when(pl.program_id(1) == 0)
    def _():
        o_ref[...] = jnp.zeros_like(o_ref)

    def chunk(g, carry):
        base = g * SCATTER_UNROLL
        rows = []
        for u in range(SCATTER_UNROLL):
            c = base + u
            t = idx_ref[0, 0, c]
            rows.append((t, o_ref[0, t] + val_ref[0, 0, c] * y_ref[0, 0, c]))
        for t, r in rows:
            o_ref[0, t] = r
        return carry

    lax.fori_loop(0, cap // SCATTER_UNROLL, chunk, 0)


def _combine(idx3, val3, y5, s):
    bsz, n_e, cap, _, d = y5.shape
    sm = lambda: pl.BlockSpec((1, 1, cap), lambda b, e: (b * n_e + e, 0, 0),
                              memory_space=pltpu.SMEM)
    return pl.pallas_call(
        functools.partial(_combine_kernel, cap=cap),
        grid=(bsz, n_e),
        in_specs=[sm(), sm(),
                  pl.BlockSpec((1, 1, cap, 1, d), lambda b, e: (b, e, 0, 0, 0))],
        out_specs=pl.BlockSpec((1, s, 1, d), lambda b, e: (b, 0, 0, 0)),
        out_shape=jax.ShapeDtypeStruct((bsz, s, 1, d), F32),
        compiler_params=_params(("parallel", "arbitrary")),
        name="combine",
    )(idx3, val3, y5)


def _final_kernel(x1_ref, moe_ref, g2_ref, lg_ref, lb_ref, o_ref, *, alpha):
    y = alpha * x1_ref[0] + g2_ref[0] * moe_ref[0]
    o_ref[0] = _layer_norm(y, lg_ref[...], lb_ref[...])


def _final(x1, moe, mod3, ln_g, ln_b, alpha, tm=512):
    bsz, s, d = x1.shape
    row = pl.BlockSpec((1, tm, d), lambda b, i: (b, i, 0))
    vec = pl.BlockSpec((1, d), lambda b, i: (0, 0))
    return pl.pallas_call(
        functools.partial(_final_kernel, alpha=alpha),
        grid=(bsz, s // tm),
        in_specs=[row, row, pl.BlockSpec((1, 1, d), lambda b, i: (6 * b + 5, 0, 0)), vec, vec],
        out_specs=row,
        out_shape=jax.ShapeDtypeStruct((bsz, s, d), F32),
        compiler_params=_params(("parallel", "arbitrary")),
        name="final",
    )(x1, moe, mod3, ln_g, ln_b)


def _rope_tables(s):
    inv = ROPE_THETA ** (-jnp.arange(0, MLA_ROPE, 2, dtype=F32) / MLA_ROPE)
    ang = jnp.arange(s, dtype=F32)[:, None] * inv[None, :]
    cos, sin = jnp.cos(ang), jnp.sin(ang)
    scale = (MLA_NOPE + MLA_ROPE) ** -0.5 * math.log2(math.e)
    z32 = jnp.zeros((s, HEAD_SLAB - MLA_NOPE - MLA_ROPE), F32)
    z64 = jnp.zeros((s, MLA_NOPE), F32)
    cosq = jnp.concatenate([jnp.full((s, MLA_NOPE), scale, F32), cos * scale, cos * scale, z32], 1)
    sinq_scaled = jnp.concatenate([z64, sin * scale, sin * scale, z32], 1)
    cosk = jnp.concatenate([z64, cos, cos, z32], 1)
    sink = jnp.concatenate([z64, sin, sin, z32], 1)
    return cosq, sinq_scaled, cosk, sink


def _layout_weights(w_in, w_uq, w_ukv):
    d = w_in.shape[0]
    c0 = MLA_Q_LORA + MLA_KV_LORA
    kr = w_in[:, c0:c0 + MLA_ROPE]
    t1, t2 = kr[:, :HALF_ROPE], kr[:, HALF_ROPE:]
    z = lambda n: jnp.zeros((d, n), w_in.dtype)
    pad = HEAD_SLAB - MLA_NOPE - MLA_ROPE
    w_in_r = jnp.concatenate(
        [w_in[:, :c0], w_in[:, c0 + MLA_ROPE:],
         z(MLA_NOPE), t1, t2, z(pad),
         z(MLA_NOPE), -t2, t1, z(pad)], axis=1).astype(BF16)

    ql = w_uq.shape[0]
    wq = w_uq.reshape(ql, MLA_HEADS, MLA_NOPE + MLA_ROPE)
    qn, q1, q2 = wq[..., :MLA_NOPE], wq[..., MLA_NOPE:MLA_NOPE + HALF_ROPE], wq[..., MLA_NOPE + HALF_ROPE:]
    zq = lambda n: jnp.zeros((ql, MLA_HEADS, n), w_uq.dtype)
    wq_pre = jnp.concatenate([qn, q1, q2, zq(pad)], -1).reshape(ql, -1).astype(BF16)
    wq_sw = jnp.concatenate([zq(MLA_NOPE), -q2, q1, zq(pad)], -1).reshape(ql, -1).astype(BF16)

    kl = w_ukv.shape[0]
    wkv = w_ukv.reshape(kl, MLA_HEADS, MLA_NOPE + MLA_V)
    wk = jnp.concatenate([wkv[..., :MLA_NOPE],
                          jnp.zeros((kl, MLA_HEADS, HEAD_SLAB - MLA_NOPE), w_ukv.dtype)],
                         -1).reshape(kl, -1).astype(BF16)
    wv = jnp.concatenate([wkv[..., MLA_NOPE:],
                          jnp.zeros((kl, MLA_HEADS, HEAD_SLAB - MLA_V), w_ukv.dtype)],
                         -1).reshape(kl, -1).T.astype(BF16)
    return w_in_r, wq_pre, wq_sw, wk, wv


def kernel(x, c, w_ada, b_ada, w_in, q_norm_g, w_uq, kv_norm_g, w_ukv, rel_bias, w_out, ln1_g,
           ln1_b, w_router, w_gate, w_up, w_down, ln2_g, ln2_b):
    bsz, s, d = x.shape
    depth = w_ada.shape[0]
    alpha = (2 * depth) ** 0.25
    cap = max(1, EC_CAPACITY_FACTOR * s // N_EXPERTS)
    assert s % (max(2 * Q_TILE_DIL, DIL_COPY_ROWS) * max(dl for _, dl in DIL_PATTERNS)) == 0
    assert all(win // 2 // dl == BAND_HALF for win, dl in DIL_PATTERNS)

    cosq, sinq, cosk, sink = _rope_tables(s)
    bias = _bias_tiles(rel_bias)

    for l in range(depth):
        mod3 = _ada(c, w_ada[l], b_ada[l]).reshape(bsz * 6, 1, d)
        w_in_r, wq_pre, wq_sw, wk, wv = _layout_weights(w_in[l], w_uq[l], w_ukv[l])
        q, k, v, dq, dk, dv = _proj(
            x, mod3, w_in_r, q_norm_g[l].reshape(1, -1), wq_pre, wq_sw,
            kv_norm_g[l].reshape(1, -1), wk, wv, cosq, sinq, cosk, sink)
        mla = _mla(q, k, v)
        dil_out = _dilated(dq, dk, dv, bias)
        x1, h2, aff = _mix(x, mla, dil_out, mod3, w_out[l].astype(BF16),
                           ln1_g[l].reshape(1, d), ln1_b[l].reshape(1, d),
                           w_router[l].T, alpha)
        idx, vals = _topk(aff, cap)
        idx3 = idx.reshape(bsz * N_EXPERTS, 1, cap)
        val3 = vals.reshape(bsz * N_EXPERTS, 1, cap)
        xin = _gather(idx3, h2, N_EXPERTS, cap)
        xin = xin.reshape(bsz, N_EXPERTS, cap, d).astype(BF16)
        y = _ffn(xin, w_gate[l], w_up[l], w_down[l])
        moe = _combine(idx3, val3, y, s)
        x = _final(x1, moe.reshape(bsz, s, d), mod3, ln2_g[l].reshape(1, d),
                   ln2_b[l].reshape(1, d), alpha)
    return x
```

```python
import functools
import math

import numpy as np
import jax
import jax.numpy as jnp
from jax import lax
from jax.experimental import pallas as pl
from jax.experimental.pallas import tpu as pltpu

MLA_HEADS = 8
MLA_NOPE = 64
MLA_ROPE = 32
MLA_V = 64
MLA_Q_LORA = 384
MLA_KV_LORA = 256
ROPE_THETA = 10000.0
DIL_HEADS = 8
DIL_HEAD_DIM = 64
DIL_PATTERNS = ((128, 1), (512, 4), (2048, 16))
REL_BUCKETS = 32
REL_MAX_EXACT = 8
REL_MAX_DIST = 1024
N_EXPERTS = 16
EC_CAPACITY_FACTOR = 2
NORM_EPS = 1e-6
NEG_INF = -1e30

LANES = 128
HEAD_SLAB = 128
VMEM_LIMIT = 48 * 1024 * 1024
F32 = jnp.float32
BF16 = jnp.bfloat16
HALF_ROPE = MLA_ROPE // 2
DIL_WIDTH = DIL_HEADS * DIL_HEAD_DIM
MLA_WIDTH = MLA_HEADS * MLA_V
BAND_HALF = 64
Q_TILE_DIL = 2 * BAND_HALF
K_WIN_DIL = 4 * BAND_HALF


def _params(sem, vmem=VMEM_LIMIT):
    return pltpu.CompilerParams(dimension_semantics=sem, vmem_limit_bytes=vmem)


def _dot(a, b):
    return jnp.dot(a, b, preferred_element_type=F32)


def _dot_nt(a, b):
    return lax.dot_general(a, b, (((1,), (1,)), ((), ())), preferred_element_type=F32)


def _layer_norm(y, g, b):
    mu = jnp.mean(y, axis=-1, keepdims=True)
    d = y - mu
    var = jnp.mean(d * d, axis=-1, keepdims=True)
    return d * lax.rsqrt(var + NORM_EPS) * g + b


def _ada_kernel(c_ref, w_ref, b_ref, o_ref):
    c = c_ref[...]
    s = c * (1.0 / (1.0 + jnp.exp(-c)))
    o_ref[...] = jnp.dot(s, w_ref[...], preferred_element_type=F32,
                         precision=lax.Precision.HIGHEST) + b_ref[...]


def _ada(c, w_ada, b_ada):
    bsz, d = c.shape
    n = w_ada.shape[1]
    tn = 1024
    return pl.pallas_call(
        _ada_kernel,
        grid=(n // tn,),
        in_specs=[pl.BlockSpec((bsz, d), lambda j: (0, 0)),
                  pl.BlockSpec((d, tn), lambda j: (0, j)),
                  pl.BlockSpec((1, tn), lambda j: (0, j))],
        out_specs=pl.BlockSpec((bsz, tn), lambda j: (0, j)),
        out_shape=jax.ShapeDtypeStruct((bsz, n), F32),
        compiler_params=_params(("arbitrary",)),
        name="ada",
    )(c, w_ada, b_ada.reshape(1, n))


_C_Q = 0
_C_KV = MLA_Q_LORA
_C_DQ = MLA_Q_LORA + MLA_KV_LORA
_C_DK = _C_DQ + DIL_WIDTH
_C_DV = _C_DK + DIL_WIDTH
_C_KR = _C_DV + DIL_WIDTH
_C_KRS = _C_KR + HEAD_SLAB
_C_END = _C_KRS + HEAD_SLAB


def _proj_kernel(x_ref, sc_ref, sh_ref, win_ref, gq_ref, wqp_ref, wqs_ref, gkv_ref,
                 wk_ref, wv_ref, cosq_ref, sinq_ref, cosk_ref, sink_ref,
                 q_ref, k_ref, v_ref, dq_ref, dk_ref, dv_ref):
    h = (x_ref[0] * (1.0 + sc_ref[0]) + sh_ref[0]).astype(BF16)
    proj = _dot(h, win_ref[...])

    c_q = proj[:, _C_Q:_C_KV]
    cqn = (c_q * lax.rsqrt(jnp.mean(c_q * c_q, axis=-1, keepdims=True) + NORM_EPS)
           * gq_ref[...]).astype(BF16)
    q_pre = _dot(cqn, wqp_ref[...])
    q_sw = _dot(cqn, wqs_ref[...])
    cosq = cosq_ref[...]
    sinq = sinq_ref[...]

    c_kv = proj[:, _C_KV:_C_DQ]
    ckvn = (c_kv * lax.rsqrt(jnp.mean(c_kv * c_kv, axis=-1, keepdims=True) + NORM_EPS)
            * gkv_ref[...]).astype(BF16)
    k_nope = _dot(ckvn, wk_ref[...])
    vt = _dot_nt(wv_ref[...], ckvn)
    vrow = lax.broadcasted_iota(jnp.int32, vt.shape, 0)
    vt = jnp.where((vrow & MLA_V) != 0, 1.0, vt).astype(BF16)
    v_ref[0] = vt.reshape(MLA_HEADS, HEAD_SLAB, vt.shape[1])
    k_rope = proj[:, _C_KR:_C_KRS] * cosk_ref[...] + proj[:, _C_KRS:_C_END] * sink_ref[...]

    for hd in range(MLA_HEADS):
        sl = slice(hd * HEAD_SLAB, (hd + 1) * HEAD_SLAB)
        q_ref[0, :, sl] = (q_pre[:, sl] * cosq + q_sw[:, sl] * sinq).astype(BF16)
        k_ref[0, :, sl] = (k_nope[:, sl] + k_rope).astype(BF16)

    dq_ref[0] = proj[:, _C_DQ:_C_DK] * (DIL_HEAD_DIM ** -0.5)
    dk_ref[0] = proj[:, _C_DK:_C_DV]
    dv_ref[0] = proj[:, _C_DV:_C_KR]


def _proj(x, mod3, w_in_r, gq, wq_pre, wq_sw, gkv, wk, wv, cosq, sinq, cosk, sink, tm=512):
    bsz, s, d = x.shape
    full = lambda a: pl.BlockSpec(a.shape, lambda b, i: (0,) * a.ndim)
    tab = pl.BlockSpec((tm, LANES), lambda b, i: (i, 0))
    hq = MLA_HEADS * HEAD_SLAB
    outs = [jax.ShapeDtypeStruct((bsz, s, hq), BF16),
            jax.ShapeDtypeStruct((bsz, s, hq), BF16),
            jax.ShapeDtypeStruct((bsz, MLA_HEADS, HEAD_SLAB, s), BF16),
            jax.ShapeDtypeStruct((bsz, s, DIL_WIDTH), F32),
            jax.ShapeDtypeStruct((bsz, s, DIL_WIDTH), F32),
            jax.ShapeDtypeStruct((bsz, s, DIL_WIDTH), F32)]
    ospec = lambda w: pl.BlockSpec((1, tm, w), lambda b, i: (b, i, 0))
    return pl.pallas_call(
        _proj_kernel,
        grid=(bsz, s // tm),
        in_specs=[pl.BlockSpec((1, tm, d), lambda b, i: (b, i, 0)),
                  pl.BlockSpec((1, 1, d), lambda b, i: (6 * b + 1, 0, 0)),
                  pl.BlockSpec((1, 1, d), lambda b, i: (6 * b + 0, 0, 0)),
                  full(w_in_r), full(gq), full(wq_pre), full(wq_sw), full(gkv),
                  full(wk), full(wv), tab, tab, tab, tab],
        out_specs=[ospec(hq), ospec(hq),
                   pl.BlockSpec((1, MLA_HEADS, HEAD_SLAB, tm), lambda b, i: (b, 0, 0, i)),
                   ospec(DIL_WIDTH),
                   ospec(DIL_WIDTH), ospec(DIL_WIDTH)],
        out_shape=outs,
        compiler_params=_params(("parallel", "arbitrary")),
        name="proj",
    )(x, mod3, mod3, w_in_r, gq, wq_pre, wq_sw, gkv, wk, wv, cosq, sinq, cosk, sink)


MLA_KEY_CHUNK = 512


MLA_Q_SUB = 256


def _mla_kernel(q_ref, k_ref, vt_ref, o_ref, s_ref, p_ref):
    tq = q_ref.shape[1]
    seq = k_ref.shape[1]
    nchunk = seq // MLA_KEY_CHUNK
    units = [(a, j) for a in range(tq // MLA_Q_SUB) for j in range(2)]
    n = len(units)
    maxima = [None] * n
    heads_out = {}

    def chunk(c):
        return slice(c * MLA_KEY_CHUNK, (c + 1) * MLA_KEY_CHUNK)

    def scores(u, c):
        a, j = units[u]
        sl = slice(j * HEAD_SLAB, (j + 1) * HEAD_SLAB)
        s = _dot_nt(k_ref[0, chunk(c), sl], q_ref[0, a * MLA_Q_SUB:(a + 1) * MLA_Q_SUB, sl])
        s_ref[u % 2, chunk(c), :] = s
        mc = jnp.max(s, axis=0, keepdims=True)
        maxima[u] = mc if maxima[u] is None else jnp.maximum(maxima[u], mc)

    def probs(u, c):
        p_ref[u % 2, chunk(c), :] = jnp.exp2(s_ref[u % 2, chunk(c), :] - maxima[u]).astype(BF16)

    def values(u):
        a, j = units[u]
        acc = _dot(vt_ref[0, j], p_ref[u % 2])
        heads_out[(a, j)] = acc[0:MLA_V] / acc[MLA_V:MLA_V + 1]
        if j == 1:
            o_t = jnp.concatenate([heads_out[(a, 0)], heads_out[(a, 1)]], axis=0)
            o_ref[0, a * MLA_Q_SUB:(a + 1) * MLA_Q_SUB, :] = o_t.T.astype(o_ref.dtype)

    for stage in range(n + 2):
        if 0 <= stage - 2 < n:
            values(stage - 2)
        for c in range(nchunk):
            if stage < n:
                scores(stage, c)
            if 0 <= stage - 1 < n:
                probs(stage - 1, c)


def _mla(q, k, vt, tq=1024):
    bsz, s, _ = q.shape
    pairs = MLA_HEADS // 2
    return pl.pallas_call(
        _mla_kernel,
        grid=(bsz, pairs, s // tq),
        in_specs=[pl.BlockSpec((1, tq, 2 * HEAD_SLAB), lambda b, p, i: (b, i, p)),
                  pl.BlockSpec((1, s, 2 * HEAD_SLAB), lambda b, p, i: (b, 0, p)),
                  pl.BlockSpec((1, 2, 2 * MLA_V, s), lambda b, p, i: (b, p, 0, 0))],
        out_specs=pl.BlockSpec((1, tq, 2 * MLA_V), lambda b, p, i: (b, i, p)),
        out_shape=jax.ShapeDtypeStruct((bsz, s, MLA_WIDTH), BF16),
        scratch_shapes=[pltpu.VMEM((2, s, MLA_Q_SUB), F32),
                        pltpu.VMEM((2, s, MLA_Q_SUB), BF16)],
        compiler_params=_params(("parallel", "arbitrary", "arbitrary")),
        name="mla",
    )(q, k, vt)


def _t5_bucket(rel):
    half = REL_BUCKETS // 2
    ret = (rel > 0).astype(np.int32) * half
    n = np.abs(rel)
    large = REL_MAX_EXACT + (np.log(np.maximum(n, 1) / REL_MAX_EXACT)
                             / np.log(REL_MAX_DIST / REL_MAX_EXACT)
                             * (half - REL_MAX_EXACT)).astype(np.int32)
    large = np.minimum(large, half - 1)
    return ret + np.where(n < REL_MAX_EXACT, n, large).astype(np.int32)


def _bucket_tiles():
    a = np.arange(Q_TILE_DIL)[:, None]
    j = np.arange(K_WIN_DIL)[None, :]
    rel = j - BAND_HALF - a
    in_range = [np.ones_like(j, bool), j >= BAND_HALF, j < K_WIN_DIL - BAND_HALF]
    tiles = []
    for _, dil in DIL_PATTERNS:
        for ok in in_range:
            tiles.append(np.where((np.abs(rel) <= BAND_HALF) & ok, _t5_bucket(rel * dil), -1))
    return np.stack(tiles).astype(np.int32)


N_EDGE = 3


def _bias_kernel(rb_ref, bucket_ref, o_ref):
    bucket = bucket_ref[0]
    accs = [jnp.where(bucket < 0, NEG_INF, 0.0).astype(F32) for _ in range(DIL_HEADS)]
    for bk in range(REL_BUCKETS):
        hit = bucket == bk
        for hd in range(DIL_HEADS):
            accs[hd] = jnp.where(hit, rb_ref[bk, hd], accs[hd])
    for hd in range(DIL_HEADS):
        o_ref[0, hd] = accs[hd]


def _bias_tiles(rel_bias):
    buckets = jnp.asarray(_bucket_tiles())
    n = buckets.shape[0]
    return pl.pallas_call(
        _bias_kernel,
        grid=(n,),
        in_specs=[pl.BlockSpec(memory_space=pltpu.SMEM),
                  pl.BlockSpec((1, Q_TILE_DIL, K_WIN_DIL), lambda p: (p, 0, 0))],
        out_specs=pl.BlockSpec((1, DIL_HEADS, Q_TILE_DIL, K_WIN_DIL), lambda p: (p, 0, 0, 0)),
        out_shape=jax.ShapeDtypeStruct((n, DIL_HEADS, Q_TILE_DIL, K_WIN_DIL), F32),
        compiler_params=_params(("arbitrary",)),
        name="dil_bias",
    )(rel_bias.astype(F32), buckets)


DIL_TILES_PER_ITER = 16
DIL_COPY_ROWS = 256
DIL_MERGE_ROWS = 256


def _dil_kernel(q_ref, k_ref, v_ref, bias_ref, o_ref,
                k0_ref, k1_ref, vp_ref, op_ref, lp_ref, *, seq):
    lane = lax.broadcasted_iota(jnp.int32, (1, LANES), 1)
    low = lane < DIL_HEAD_DIM
    zeros = jnp.zeros((BAND_HALF, LANES), BF16)

    def rows(start, size, dil):
        return pl.ds(start, size) if dil == 1 else pl.ds(start, size, stride=dil)

    for p, (_, dil) in enumerate(DIL_PATTERNS):
        length = seq // dil
        ntile = length // Q_TILE_DIL
        span = length + 2 * BAND_HALF

        def deinterleave(r, carry, dil=dil, length=length, span=span):
            base = pl.multiple_of(r * span, 2 * BAND_HALF)
            for ref in (k0_ref, k1_ref, vp_ref):
                ref[pl.ds(base, BAND_HALF), :] = zeros
                ref[pl.ds(base + BAND_HALF + length, BAND_HALF), :] = zeros
            for cidx in range(length // DIL_COPY_ROWS):
                src = rows(r + dil * DIL_COPY_ROWS * cidx, DIL_COPY_ROWS, dil)
                dst = pl.ds(pl.multiple_of(base + BAND_HALF + DIL_COPY_ROWS * cidx, BAND_HALF),
                            DIL_COPY_ROWS)
                kk = k_ref[0, src, :]
                k0_ref[dst, :] = jnp.where(low, kk, 0.0).astype(BF16)
                k1_ref[dst, :] = jnp.where(low, 0.0, kk).astype(BF16)
                vp_ref[dst, :] = v_ref[0, src, :].astype(BF16)
            return carry

        lax.fori_loop(0, dil, deinterleave, 0)

        def tiles(it, carry, p=p, dil=dil, ntile=ntile, span=span):
            for u in range(DIL_TILES_PER_ITER):
                t = it * DIL_TILES_PER_ITER + u
                r = t // ntile
                i = t % ntile
                edge = jnp.where(i == 0, 1, jnp.where(i == ntile - 1, 2, 0))
                r0 = i * Q_TILE_DIL
                tok = rows(r + dil * r0, Q_TILE_DIL, dil)
                win = pl.ds(pl.multiple_of(r * span + r0, Q_TILE_DIL), K_WIN_DIL)
                q = q_ref[0, tok, :].astype(BF16)
                vw = vp_ref[win, :]
                outs, lses = [], []
                for j, kref in enumerate((k0_ref, k1_ref)):
                    s = _dot_nt(q, kref[win, :]) + bias_ref[p * N_EDGE + edge, j]
                    m = jnp.max(s, axis=-1, keepdims=True)
                    e = jnp.exp(s - m)
                    den = jnp.sum(e, axis=-1, keepdims=True)
                    outs.append(_dot(e.astype(BF16), vw) / den)
                    lses.append(m + jnp.log(den))
                op_ref[p, tok, :] = jnp.where(low, outs[0], outs[1])
                lp_ref[p, tok, :] = jnp.where(low, lses[0], lses[1])
            return carry

        lax.fori_loop(0, seq // Q_TILE_DIL // DIL_TILES_PER_ITER, tiles, 0)

    def merge(g, carry):
        sl = pl.ds(pl.multiple_of(g * DIL_MERGE_ROWS, DIL_MERGE_ROWS), DIL_MERGE_ROWS)
        ls = [lp_ref[p, sl, :] for p in range(len(DIL_PATTERNS))]
        m = functools.reduce(jnp.maximum, ls)
        es = [jnp.exp(l - m) for l in ls]
        num = functools.reduce(lambda a, b: a + b,
                               [e * op_ref[p, sl, :] for p, e in enumerate(es)])
        den = functools.reduce(lambda a, b: a + b, es)
        o_ref[0, sl, :] = (num / den).astype(o_ref.dtype)
        return carry

    lax.fori_loop(0, seq // DIL_MERGE_ROWS, merge, 0)


def _dilated(dq, dk, dv, bias):
    bsz, s, w = dq.shape
    pairs = DIL_HEADS // 2
    npat = len(DIL_PATTERNS)
    blk = pl.BlockSpec((1, s, LANES), lambda b, p: (b, 0, p))
    pad_rows = s + 2 * BAND_HALF * max(dl for _, dl in DIL_PATTERNS)
    assert (s // Q_TILE_DIL) % DIL_TILES_PER_ITER == 0
    return pl.pallas_call(
        functools.partial(_dil_kernel, seq=s),
        grid=(bsz, pairs),
        in_specs=[blk, blk, blk,
                  pl.BlockSpec((npat * N_EDGE, 2, Q_TILE_DIL, K_WIN_DIL), lambda b, p: (0, p, 0, 0))],
        out_specs=blk,
        out_shape=jax.ShapeDtypeStruct((bsz, s, w), BF16),
        scratch_shapes=[pltpu.VMEM((pad_rows, LANES), BF16)] * 3
                       + [pltpu.VMEM((npat, s, LANES), F32)] * 2,
        compiler_params=_params(("parallel", "arbitrary")),
        name="dilated",
    )(dq, dk, dv, bias)


def _mix_kernel(x_ref, mla_ref, dil_ref,
                g1_ref, sc2_ref, sh2_ref, wout_ref, lg_ref, lb_ref, wr_ref,
                x1_ref, h2_ref, aff_ref, *, alpha):
    mix = _dot(mla_ref[0], wout_ref[0:MLA_WIDTH, :]) + _dot(dil_ref[0], wout_ref[MLA_WIDTH:, :])
    y = alpha * x_ref[0] + g1_ref[0] * mix
    x1 = _layer_norm(y, lg_ref[...], lb_ref[...])
    x1_ref[0] = x1
    h2 = x1 * (1.0 + sc2_ref[0]) + sh2_ref[0]
    h2_ref[0] = h2.reshape(h2.shape[0], 1, h2.shape[1])
    logits = lax.dot_general(wr_ref[...], h2, (((1,), (1,)), ((), ())),
                             preferred_element_type=F32,
                             precision=lax.Precision.HIGHEST)
    mx = jnp.max(logits, axis=0, keepdims=True)
    ex = jnp.exp(logits - mx)
    aff_ref[0] = ex / jnp.sum(ex, axis=0, keepdims=True)


def _mix(x, mla, dil, mod3, w_out, ln_g, ln_b, w_router_t, alpha, tm=512):
    bsz, s, d = x.shape
    row = lambda w: pl.BlockSpec((1, tm, w), lambda b, i: (b, i, 0))
    modspec = lambda k: pl.BlockSpec((1, 1, d), lambda b, i: (6 * b + k, 0, 0))
    full = lambda a: pl.BlockSpec(a.shape, lambda b, i: (0,) * a.ndim)
    return pl.pallas_call(
        functools.partial(_mix_kernel, alpha=alpha),
        grid=(bsz, s // tm),
        in_specs=[row(d), row(MLA_WIDTH), row(DIL_WIDTH)]
                 + [modspec(2), modspec(4), modspec(3),
                    full(w_out), full(ln_g), full(ln_b), full(w_router_t)],
        out_specs=[row(d), pl.BlockSpec((1, tm, 1, d), lambda b, i: (b, i, 0, 0)),
                   pl.BlockSpec((1, N_EXPERTS, tm), lambda b, i: (b, 0, i))],
        out_shape=[jax.ShapeDtypeStruct((bsz, s, d), F32),
                   jax.ShapeDtypeStruct((bsz, s, 1, d), F32),
                   jax.ShapeDtypeStruct((bsz, N_EXPERTS, s), F32)],
        compiler_params=_params(("parallel", "arbitrary")),
        name="mix",
    )(x, mla, dil, mod3, mod3, mod3, w_out, ln_g, ln_b, w_router_t)


def _cumsum_lanes(x, tri):
    rows, n = x.shape
    carry = jnp.zeros((rows, 1), F32)
    parts = []
    for j in range(n // LANES):
        inc = _dot(x[:, j * LANES:(j + 1) * LANES].astype(BF16), tri) + carry
        parts.append(inc)
        carry = inc[:, LANES - 1:LANES]
    return jnp.concatenate(parts, axis=1)


TOPK_EXPERTS_PER_ITER = 2


def _topk_kernel(aff_ref, idx_ref, val_ref, key_ref, *, cap):
    a = aff_ref[0]
    n_e, s = a.shape
    bits = pltpu.bitcast(a, jnp.int32)
    t = jnp.zeros((n_e, 1), jnp.int32)
    for bit in range(30, -1, -1):
        cand = t | (1 << bit)
        cnt = jnp.sum((bits >= cand).astype(jnp.int32), axis=1, keepdims=True)
        t = jnp.where(cnt >= cap, cand, t)
    gt = bits > t
    eq = bits == t
    n_gt = jnp.sum(gt.astype(jnp.int32), axis=1, keepdims=True)
    ri = lax.broadcasted_iota(jnp.int32, (LANES, LANES), 0)
    ci = lax.broadcasted_iota(jnp.int32, (LANES, LANES), 1)
    tri = jnp.where(ri <= ci, 1.0, 0.0).astype(BF16)
    eq_f = jnp.where(eq, 1.0, 0.0)
    rank_eq = _cumsum_lanes(eq_f, tri) - eq_f
    sel = gt | (eq & (rank_eq < (cap - n_gt).astype(F32)))
    sel_f = jnp.where(sel, 1.0, 0.0)
    pos = _cumsum_lanes(sel_f, tri) - sel_f
    key_ref[...] = jnp.where(sel, pos.astype(jnp.int32), -1)

    tok = lax.broadcasted_iota(jnp.int32, (1, s), 1)
    tok_hi = (tok >> 6).astype(F32)
    tok_lo = (tok & 63).astype(F32)
    slot = lax.broadcasted_iota(jnp.int32, (cap, 1), 0)
    rid = lax.broadcasted_iota(jnp.int32, (8, s), 0)

    def per_expert(e):
        key = key_ref[pl.ds(e, 1), :]
        onehot = jnp.where(key == slot, 1.0, 0.0).astype(BF16)
        ar = aff_ref[0, pl.ds(e, 1), :]
        a_hi = ar.astype(BF16).astype(F32)
        r1 = ar - a_hi
        a_mid = r1.astype(BF16).astype(F32)
        a_lo = r1 - a_mid
        lhs = jnp.where(rid == 0, tok_hi,
              jnp.where(rid == 1, tok_lo,
              jnp.where(rid == 2, a_hi,
              jnp.where(rid == 3, a_mid,
              jnp.where(rid == 4, a_lo, 0.0))))).astype(BF16)
        res = _dot_nt(lhs, onehot)
        idx_ref[0, pl.ds(e, 1), :] = (res[0:1] * 64.0 + res[1:2]).astype(jnp.int32)
        val_ref[0, pl.ds(e, 1), :] = res[2:3] + res[3:4] + res[4:5]

    def expert_group(g, carry):
        for u in range(TOPK_EXPERTS_PER_ITER):
            per_expert(g * TOPK_EXPERTS_PER_ITER + u)
        return carry

    lax.fori_loop(0, n_e // TOPK_EXPERTS_PER_ITER, expert_group, 0)


def _topk(aff, cap):
    bsz, n_e, s = aff.shape
    return pl.pallas_call(
        functools.partial(_topk_kernel, cap=cap),
        grid=(bsz,),
        in_specs=[pl.BlockSpec((1, n_e, s), lambda b: (b, 0, 0))],
        out_specs=[pl.BlockSpec((1, n_e, cap), lambda b: (b, 0, 0)),
                   pl.BlockSpec((1, n_e, cap), lambda b: (b, 0, 0))],
        out_shape=[jax.ShapeDtypeStruct((bsz, n_e, cap), jnp.int32),
                   jax.ShapeDtypeStruct((bsz, n_e, cap), F32)],
        scratch_shapes=[pltpu.VMEM((n_e, s), jnp.int32)],
        compiler_params=_params(("parallel",)),
        name="topk",
    )(aff)


GATHER_UNROLL = 8


def _gather_kernel(idx_ref, h_ref, o_ref, *, cap):
    def chunk(g, carry):
        base = g * GATHER_UNROLL
        for u in range(GATHER_UNROLL):
            c = base + u
            o_ref[0, 0, c] = h_ref[0, idx_ref[0, 0, c]]
        return carry

    lax.fori_loop(0, cap // GATHER_UNROLL, chunk, 0)


def _gather(idx3, h4, n_e, cap):
    bsz, s, _, d = h4.shape
    return pl.pallas_call(
        functools.partial(_gather_kernel, cap=cap),
        grid=(bsz, n_e),
        in_specs=[pl.BlockSpec((1, 1, cap), lambda b, e: (b * n_e + e, 0, 0),
                               memory_space=pltpu.SMEM),
                  pl.BlockSpec((1, s, 1, d), lambda b, e: (b, 0, 0, 0))],
        out_specs=pl.BlockSpec((1, 1, cap, 1, d), lambda b, e: (b, e, 0, 0, 0)),
        out_shape=jax.ShapeDtypeStruct((bsz, n_e, cap, 1, d), F32),
        compiler_params=_params(("parallel", "arbitrary")),
        name="gather",
    )(idx3, h4)


def _ffn_kernel(x_ref, wg_hbm, wu_hbm, wd_hbm, y_ref,
                wg_bf, wu_bf, wd_bf, stage_g, stage_u, stage_d, sems):
    e = pl.program_id(0)
    b = pl.program_id(1)
    n_e = pl.num_programs(0)
    n_slices = pl.num_programs(1)
    rows_g = stage_g.shape[0]
    rows_d = stage_d.shape[0]
    slot = e % 2

    def slice_copies(expert, i):
        return (pltpu.make_async_copy(wg_hbm.at[expert, pl.ds(i * rows_g, rows_g), :], stage_g, sems.at[0]),
                pltpu.make_async_copy(wu_hbm.at[expert, pl.ds(i * rows_g, rows_g), :], stage_u, sems.at[1]),
                pltpu.make_async_copy(wd_hbm.at[expert, pl.ds(i * rows_d, rows_d), :], stage_d, sems.at[2]))

    def cast_slice(dst_slot, i):
        wg_bf[dst_slot, pl.ds(pl.multiple_of(i * rows_g, 16), rows_g), :] = stage_g[...].astype(BF16)
        wu_bf[dst_slot, pl.ds(pl.multiple_of(i * rows_g, 16), rows_g), :] = stage_u[...].astype(BF16)
        wd_bf[dst_slot, pl.ds(pl.multiple_of(i * rows_d, 16), rows_d), :] = stage_d[...].astype(BF16)

    @pl.when(jnp.logical_and(e == 0, b == 0))
    def _():
        def load(i, carry):
            cps = slice_copies(0, i)
            for cp in cps:
                cp.start()
            for cp in cps:
                cp.wait()
            cast_slice(0, i)
            return carry
        lax.fori_loop(0, n_slices, load, 0)

    has_next = e + 1 < n_e

    @pl.when(has_next)
    def _():
        for cp in slice_copies(e + 1, b):
            cp.start()

    x = x_ref[0, 0]
    gt = _dot(x, wg_bf[slot])
    up = _dot(x, wu_bf[slot])
    act = (gt * (1.0 / (1.0 + jnp.exp(-gt))) * up).astype(BF16)
    y = _dot(act, wd_bf[slot])
    y_ref[0, 0] = y.reshape(y.shape[0], 1, y.shape[1])

    @pl.when(has_next)
    def _():
        for cp in slice_copies(e + 1, b):
            cp.wait()
        cast_slice(1 - slot, b)


def _ffn(xin, wg, wu, wd):
    bsz, n_e, cap, d = xin.shape
    f = wg.shape[-1]
    assert d % (16 * bsz) == 0 and f % (16 * bsz) == 0
    hbm = pl.BlockSpec(memory_space=pl.ANY)
    return pl.pallas_call(
        _ffn_kernel,
        grid=(n_e, bsz),
        in_specs=[pl.BlockSpec((1, 1, cap, d), lambda e, b: (b, e, 0, 0)), hbm, hbm, hbm],
        out_specs=pl.BlockSpec((1, 1, cap, 1, d), lambda e, b: (b, e, 0, 0, 0)),
        out_shape=jax.ShapeDtypeStruct((bsz, n_e, cap, 1, d), F32),
        scratch_shapes=[pltpu.VMEM((2, d, f), BF16), pltpu.VMEM((2, d, f), BF16), pltpu.VMEM((2, f, d), BF16),
                        pltpu.VMEM((d // bsz, f), F32), pltpu.VMEM((d // bsz, f), F32),
                        pltpu.VMEM((f // bsz, d), F32), pltpu.SemaphoreType.DMA((3,))],
        compiler_params=_params(("arbitrary", "arbitrary")),
        name="ffn",
    )(xin, wg, wu, wd)


SCATTER_UNROLL = 8


def _combine_kernel(idx_ref, val_ref, y_ref, o_ref, *, cap):
    @pl.when(pl.program_id(1) == 0)
    def _():
        o_ref[...] = jnp.zeros_like(o_ref)

    def chunk(g, carry):
        base = g * SCATTER_UNROLL
        rows = []
        for u in range(SCATTER_UNROLL):
            c = base + u
            t = idx_ref[0, 0, c]
            rows.append((t, o_ref[0, t] + val_ref[0, 0, c] * y_ref[0, 0, c]))
        for t, r in rows:
            o_ref[0, t] = r
        return carry

    lax.fori_loop(0, cap // SCATTER_UNROLL, chunk, 0)


def _combine(idx3, val3, y5, s):
    bsz, n_e, cap, _, d = y5.shape
    sm = lambda: pl.BlockSpec((1, 1, cap), lambda b, e: (b * n_e + e, 0, 0),
                              memory_space=pltpu.SMEM)
    return pl.pallas_call(
        functools.partial(_combine_kernel, cap=cap),
        grid=(bsz, n_e),
        in_specs=[sm(), sm(),
                  pl.BlockSpec((1, 1, cap, 1, d), lambda b, e: (b, e, 0, 0, 0))],
        out_specs=pl.BlockSpec((1, s, 1, d), lambda b, e: (b, 0, 0, 0)),
        out_shape=jax.ShapeDtypeStruct((bsz, s, 1, d), F32),
        compiler_params=_params(("parallel", "arbitrary")),
        name="combine",
    )(idx3, val3, y5)


def _final_kernel(x1_ref, moe_ref, g2_ref, lg_ref, lb_ref, o_ref, *, alpha):
    y = alpha * x1_ref[0] + g2_ref[0] * moe_ref[0]
    o_ref[0] = _layer_norm(y, lg_ref[...], lb_ref[...])


def _final(x1, moe, mod3, ln_g, ln_b, alpha, tm=512):
    bsz, s, d = x1.shape
    row = pl.BlockSpec((1, tm, d), lambda b, i: (b, i, 0))
    vec = pl.BlockSpec((1, d), lambda b, i: (0, 0))
    return pl.pallas_call(
        functools.partial(_final_kernel, alpha=alpha),
        grid=(bsz, s // tm),
        in_specs=[row, row, pl.BlockSpec((1, 1, d), lambda b, i: (6 * b + 5, 0, 0)), vec, vec],
        out_specs=row,
        out_shape=jax.ShapeDtypeStruct((bsz, s, d), F32),
        compiler_params=_params(("parallel", "arbitrary")),
        name="final",
    )(x1, moe, mod3, ln_g, ln_b)


def _rope_tables(s):
    inv = ROPE_THETA ** (-jnp.arange(0, MLA_ROPE, 2, dtype=F32) / MLA_ROPE)
    ang = jnp.arange(s, dtype=F32)[:, None] * inv[None, :]
    cos, sin = jnp.cos(ang), jnp.sin(ang)
    scale = (MLA_NOPE + MLA_ROPE) ** -0.5 * math.log2(math.e)
    z32 = jnp.zeros((s, HEAD_SLAB - MLA_NOPE - MLA_ROPE), F32)
    z64 = jnp.zeros((s, MLA_NOPE), F32)
    cosq = jnp.concatenate([jnp.full((s, MLA_NOPE), scale, F32), cos * scale, cos * scale, z32], 1)
    sinq_scaled = jnp.concatenate([z64, sin * scale, sin * scale, z32], 1)
    cosk = jnp.concatenate([z64, cos, cos, z32], 1)
    sink = jnp.concatenate([z64, sin, sin, z32], 1)
    return cosq, sinq_scaled, cosk, sink


def _layout_weights(w_in, w_uq, w_ukv):
    d = w_in.shape[0]
    c0 = MLA_Q_LORA + MLA_KV_LORA
    kr = w_in[:, c0:c0 + MLA_ROPE]
    t1, t2 = kr[:, :HALF_ROPE], kr[:, HALF_ROPE:]
    z = lambda n: jnp.zeros((d, n), w_in.dtype)
    pad = HEAD_SLAB - MLA_NOPE - MLA_ROPE
    w_in_r = jnp.concatenate(
        [w_in[:, :c0], w_in[:, c0 + MLA_ROPE:],
         z(MLA_NOPE), t1, t2, z(pad),
         z(MLA_NOPE), -t2, t1, z(pad)], axis=1).astype(BF16)

    ql = w_uq.shape[0]
    wq = w_uq.reshape(ql, MLA_HEADS, MLA_NOPE + MLA_ROPE)
    qn, q1, q2 = wq[..., :MLA_NOPE], wq[..., MLA_NOPE:MLA_NOPE + HALF_ROPE], wq[..., MLA_NOPE + HALF_ROPE:]
    zq = lambda n: jnp.zeros((ql, MLA_HEADS, n), w_uq.dtype)
    wq_pre = jnp.concatenate([qn, q1, q2, zq(pad)], -1).reshape(ql, -1).astype(BF16)
    wq_sw = jnp.concatenate([zq(MLA_NOPE), -q2, q1, zq(pad)], -1).reshape(ql, -1).astype(BF16)

    kl = w_ukv.shape[0]
    wkv = w_ukv.reshape(kl, MLA_HEADS, MLA_NOPE + MLA_V)
    wk = jnp.concatenate([wkv[..., :MLA_NOPE],
                          jnp.zeros((kl, MLA_HEADS, HEAD_SLAB - MLA_NOPE), w_ukv.dtype)],
                         -1).reshape(kl, -1).astype(BF16)
    wv = jnp.concatenate([wkv[..., MLA_NOPE:],
                          jnp.zeros((kl, MLA_HEADS, HEAD_SLAB - MLA_V), w_ukv.dtype)],
                         -1).reshape(kl, -1).T.astype(BF16)
    return w_in_r, wq_pre, wq_sw, wk, wv


def kernel(x, c, w_ada, b_ada, w_in, q_norm_g, w_uq, kv_norm_g, w_ukv, rel_bias, w_out, ln1_g,
           ln1_b, w_router, w_gate, w_up, w_down, ln2_g, ln2_b):
    bsz, s, d = x.shape
    depth = w_ada.shape[0]
    alpha = (2 * depth) ** 0.25
    cap = max(1, EC_CAPACITY_FACTOR * s // N_EXPERTS)
    assert s % (max(2 * Q_TILE_DIL, DIL_COPY_ROWS) * max(dl for _, dl in DIL_PATTERNS)) == 0
    assert all(win // 2 // dl == BAND_HALF for win, dl in DIL_PATTERNS)

    cosq, sinq, cosk, sink = _rope_tables(s)
    bias = _bias_tiles(rel_bias)

    for l in range(depth):
        mod3 = _ada(c, w_ada[l], b_ada[l]).reshape(bsz * 6, 1, d)
        w_in_r, wq_pre, wq_sw, wk, wv = _layout_weights(w_in[l], w_uq[l], w_ukv[l])
        q, k, v, dq, dk, dv = _proj(
            x, mod3, w_in_r, q_norm_g[l].reshape(1, -1), wq_pre, wq_sw,
            kv_norm_g[l].reshape(1, -1), wk, wv, cosq, sinq, cosk, sink)
        mla = _mla(q, k, v)
        dil_out = _dilated(dq, dk, dv, bias)
        x1, h2, aff = _mix(x, mla, dil_out, mod3, w_out[l].astype(BF16),
                           ln1_g[l].reshape(1, d), ln1_b[l].reshape(1, d),
                           w_router[l].T, alpha)
        idx, vals = _topk(aff, cap)
        idx3 = idx.reshape(bsz * N_EXPERTS, 1, cap)
        val3 = vals.reshape(bsz * N_EXPERTS, 1, cap)
        xin = _gather(idx3, h2, N_EXPERTS, cap)
        xin = xin.reshape(bsz, N_EXPERTS, cap, d).astype(BF16)
        y = _ffn(xin, w_gate[l], w_up[l], w_down[l])
        moe = _combine(idx3, val3, y, s)
        x = _final(x1, moe.reshape(bsz, s, d), mod3, ln2_g[l].reshape(1, d),
                   ln2_b[l].reshape(1, d), alpha)
    return x
```

```python
import functools
import math

import numpy as np
import jax
import jax.numpy as jnp
from jax import lax
from jax.experimental import pallas as pl
from jax.experimental.pallas import tpu as pltpu

MLA_HEADS = 8
MLA_NOPE = 64
MLA_ROPE = 32
MLA_V = 64
MLA_Q_LORA = 384
MLA_KV_LORA = 256
ROPE_THETA = 10000.0
DIL_HEADS = 8
DIL_HEAD_DIM = 64
DIL_PATTERNS = ((128, 1), (512, 4), (2048, 16))
REL_BUCKETS = 32
REL_MAX_EXACT = 8
REL_MAX_DIST = 1024
N_EXPERTS = 16
EC_CAPACITY_FACTOR = 2
NORM_EPS = 1e-6
NEG_INF = -1e30

LANES = 128
HEAD_SLAB = 128
VMEM_LIMIT = 48 * 1024 * 1024
F32 = jnp.float32
BF16 = jnp.bfloat16
HALF_ROPE = MLA_ROPE // 2
DIL_WIDTH = DIL_HEADS * DIL_HEAD_DIM
MLA_WIDTH = MLA_HEADS * MLA_V
BAND_HALF = 64
Q_TILE_DIL = 2 * BAND_HALF
K_WIN_DIL = 4 * BAND_HALF


def _params(sem, vmem=VMEM_LIMIT):
    return pltpu.CompilerParams(dimension_semantics=sem, vmem_limit_bytes=vmem)


def _dot(a, b):
    return jnp.dot(a, b, preferred_element_type=F32)


def _dot_nt(a, b):
    return lax.dot_general(a, b, (((1,), (1,)), ((), ())), preferred_element_type=F32)


def _layer_norm(y, g, b):
    mu = jnp.mean(y, axis=-1, keepdims=True)
    d = y - mu
    var = jnp.mean(d * d, axis=-1, keepdims=True)
    return d * lax.rsqrt(var + NORM_EPS) * g + b


def _ada_kernel(c_ref, w_ref, b_ref, o_ref):
    c = c_ref[...]
    s = c * (1.0 / (1.0 + jnp.exp(-c)))
    o_ref[...] = jnp.dot(s, w_ref[...], preferred_element_type=F32,
                         precision=lax.Precision.HIGHEST) + b_ref[...]


def _ada(c, w_ada, b_ada):
    bsz, d = c.shape
    n = w_ada.shape[1]
    tn = 1024
    return pl.pallas_call(
        _ada_kernel,
        grid=(n // tn,),
        in_specs=[pl.BlockSpec((bsz, d), lambda j: (0, 0)),
                  pl.BlockSpec((d, tn), lambda j: (0, j)),
                  pl.BlockSpec((1, tn), lambda j: (0, j))],
        out_specs=pl.BlockSpec((bsz, tn), lambda j: (0, j)),
        out_shape=jax.ShapeDtypeStruct((bsz, n), F32),
        compiler_params=_params(("arbitrary",)),
        name="ada",
    )(c, w_ada, b_ada.reshape(1, n))


_C_Q = 0
_C_KV = MLA_Q_LORA
_C_DQ = MLA_Q_LORA + MLA_KV_LORA
_C_DK = _C_DQ + DIL_WIDTH
_C_DV = _C_DK + DIL_WIDTH
_C_KR = _C_DV + DIL_WIDTH
_C_KRS = _C_KR + HEAD_SLAB
_C_END = _C_KRS + HEAD_SLAB


def _proj_kernel(x_ref, sc_ref, sh_ref, win_ref, gq_ref, wqp_ref, wqs_ref, gkv_ref,
                 wk_ref, wv_ref, cosq_ref, sinq_ref, cosk_ref, sink_ref,
                 q_ref, k_ref, v_ref, dq_ref, dk_ref, dv_ref):
    h = (x_ref[0] * (1.0 + sc_ref[0]) + sh_ref[0]).astype(BF16)
    proj = _dot(h, win_ref[...])

    c_q = proj[:, _C_Q:_C_KV]
    cqn = (c_q * lax.rsqrt(jnp.mean(c_q * c_q, axis=-1, keepdims=True) + NORM_EPS)
           * gq_ref[...]).astype(BF16)
    q_pre = _dot(cqn, wqp_ref[...])
    q_sw = _dot(cqn, wqs_ref[...])
    cosq = cosq_ref[...]
    sinq = sinq_ref[...]

    c_kv = proj[:, _C_KV:_C_DQ]
    ckvn = (c_kv * lax.rsqrt(jnp.mean(c_kv * c_kv, axis=-1, keepdims=True) + NORM_EPS)
            * gkv_ref[...]).astype(BF16)
    k_nope = _dot(ckvn, wk_ref[...])
    vt = _dot_nt(wv_ref[...], ckvn)
    vrow = lax.broadcasted_iota(jnp.int32, vt.shape, 0)
    vt = jnp.where((vrow & MLA_V) != 0, 1.0, vt).astype(BF16)
    v_ref[0] = vt.reshape(MLA_HEADS, HEAD_SLAB, vt.shape[1])
    k_rope = proj[:, _C_KR:_C_KRS] * cosk_ref[...] + proj[:, _C_KRS:_C_END] * sink_ref[...]

    for hd in range(MLA_HEADS):
        sl = slice(hd * HEAD_SLAB, (hd + 1) * HEAD_SLAB)
        q_ref[0, :, sl] = (q_pre[:, sl] * cosq + q_sw[:, sl] * sinq).astype(BF16)
        k_ref[0, :, sl] = (k_nope[:, sl] + k_rope).astype(BF16)

    dq_ref[0] = proj[:, _C_DQ:_C_DK] * (DIL_HEAD_DIM ** -0.5)
    dk_ref[0] = proj[:, _C_DK:_C_DV]
    dv_ref[0] = proj[:, _C_DV:_C_KR]


def _proj(x, mod3, w_in_r, gq, wq_pre, wq_sw, gkv, wk, wv, cosq, sinq, cosk, sink, tm=512):
    bsz, s, d = x.shape
    full = lambda a: pl.BlockSpec(a.shape, lambda b, i: (0,) * a.ndim)
    tab = pl.BlockSpec((tm, LANES), lambda b, i: (i, 0))
    hq = MLA_HEADS * HEAD_SLAB
    outs = [jax.ShapeDtypeStruct((bsz, s, hq), BF16),
            jax.ShapeDtypeStruct((bsz, s, hq), BF16),
            jax.ShapeDtypeStruct((bsz, MLA_HEADS, HEAD_SLAB, s), BF16),
            jax.ShapeDtypeStruct((bsz, s, DIL_WIDTH), F32),
            jax.ShapeDtypeStruct((bsz, s, DIL_WIDTH), F32),
            jax.ShapeDtypeStruct((bsz, s, DIL_WIDTH), F32)]
    ospec = lambda w: pl.BlockSpec((1, tm, w), lambda b, i: (b, i, 0))
    return pl.pallas_call(
        _proj_kernel,
        grid=(bsz, s // tm),
        in_specs=[pl.BlockSpec((1, tm, d), lambda b, i: (b, i, 0)),
                  pl.BlockSpec((1, 1, d), lambda b, i: (6 * b + 1, 0, 0)),
                  pl.BlockSpec((1, 1, d), lambda b, i: (6 * b + 0, 0, 0)),
                  full(w_in_r), full(gq), full(wq_pre), full(wq_sw), full(gkv),
                  full(wk), full(wv), tab, tab, tab, tab],
        out_specs=[ospec(hq), ospec(hq),
                   pl.BlockSpec((1, MLA_HEADS, HEAD_SLAB, tm), lambda b, i: (b, 0, 0, i)),
                   ospec(DIL_WIDTH),
                   ospec(DIL_WIDTH), ospec(DIL_WIDTH)],
        out_shape=outs,
        compiler_params=_params(("parallel", "arbitrary")),
        name="proj",
    )(x, mod3, mod3, w_in_r, gq, wq_pre, wq_sw, gkv, wk, wv, cosq, sinq, cosk, sink)


MLA_KEY_CHUNK = 512


MLA_Q_SUB = 256


def _mla_kernel(q_ref, k_ref, vt_ref, o_ref, s_ref, p_ref):
    tq = q_ref.shape[1]
    seq = k_ref.shape[1]
    nchunk = seq // MLA_KEY_CHUNK
    units = [(a, j) for a in range(tq // MLA_Q_SUB) for j in range(2)]
    n = len(units)
    maxima = [None] * n
    heads_out = {}

    def chunk(c):
        return slice(c * MLA_KEY_CHUNK, (c + 1) * MLA_KEY_CHUNK)

    def scores(u, c):
        a, j = units[u]
        sl = slice(j * HEAD_SLAB, (j + 1) * HEAD_SLAB)
        s = _dot_nt(k_ref[0, chunk(c), sl], q_ref[0, a * MLA_Q_SUB:(a + 1) * MLA_Q_SUB, sl])
        s_ref[u % 2, chunk(c), :] = s
        mc = jnp.max(s, axis=0, keepdims=True)
        maxima[u] = mc if maxima[u] is None else jnp.maximum(maxima[u], mc)

    def probs(u, c):
        p_ref[u % 2, chunk(c), :] = jnp.exp2(s_ref[u % 2, chunk(c), :] - maxima[u]).astype(BF16)

    def values(u):
        a, j = units[u]
        acc = _dot(vt_ref[0, j], p_ref[u % 2])
        heads_out[(a, j)] = acc[0:MLA_V] / acc[MLA_V:MLA_V + 1]
        if j == 1:
            o_t = jnp.concatenate([heads_out[(a, 0)], heads_out[(a, 1)]], axis=0)
            o_ref[0, a * MLA_Q_SUB:(a + 1) * MLA_Q_SUB, :] = o_t.T.astype(o_ref.dtype)

    for stage in range(n + 2):
        if 0 <= stage - 2 < n:
            values(stage - 2)
        for c in range(nchunk):
            if stage < n:
                scores(stage, c)
            if 0 <= stage - 1 < n:
                probs(stage - 1, c)


def _mla(q, k, vt, tq=1024):
    bsz, s, _ = q.shape
    pairs = MLA_HEADS // 2
    return pl.pallas_call(
        _mla_kernel,
        grid=(bsz, pairs, s // tq),
        in_specs=[pl.BlockSpec((1, tq, 2 * HEAD_SLAB), lambda b, p, i: (b, i, p)),
                  pl.BlockSpec((1, s, 2 * HEAD_SLAB), lambda b, p, i: (b, 0, p)),
                  pl.BlockSpec((1, 2, 2 * MLA_V, s), lambda b, p, i: (b, p, 0, 0))],
        out_specs=pl.BlockSpec((1, tq, 2 * MLA_V), lambda b, p, i: (b, i, p)),
        out_shape=jax.ShapeDtypeStruct((bsz, s, MLA_WIDTH), BF16),
        scratch_shapes=[pltpu.VMEM((2, s, MLA_Q_SUB), F32),
                        pltpu.VMEM((2, s, MLA_Q_SUB), BF16)],
        compiler_params=_params(("parallel", "arbitrary", "arbitrary")),
        name="mla",
    )(q, k, vt)


def _t5_bucket(rel):
    half = REL_BUCKETS // 2
    ret = (rel > 0).astype(np.int32) * half
    n = np.abs(rel)
    large = REL_MAX_EXACT + (np.log(np.maximum(n, 1) / REL_MAX_EXACT)
                             / np.log(REL_MAX_DIST / REL_MAX_EXACT)
                             * (half - REL_MAX_EXACT)).astype(np.int32)
    large = np.minimum(large, half - 1)
    return ret + np.where(n < REL_MAX_EXACT, n, large).astype(np.int32)


def _bucket_tiles():
    a = np.arange(Q_TILE_DIL)[:, None]
    j = np.arange(K_WIN_DIL)[None, :]
    rel = j - BAND_HALF - a
    in_range = [np.ones_like(j, bool), j >= BAND_HALF, j < K_WIN_DIL - BAND_HALF]
    tiles = []
    for _, dil in DIL_PATTERNS:
        for ok in in_range:
            tiles.append(np.where((np.abs(rel) <= BAND_HALF) & ok, _t5_bucket(rel * dil), -1))
    return np.stack(tiles).astype(np.int32)


N_EDGE = 3


def _bias_kernel(rb_ref, bucket_ref, o_ref):
    bucket = bucket_ref[0]
    accs = [jnp.where(bucket < 0, NEG_INF, 0.0).astype(F32) for _ in range(DIL_HEADS)]
    for bk in range(REL_BUCKETS):
        hit = bucket == bk
        for hd in range(DIL_HEADS):
            accs[hd] = jnp.where(hit, rb_ref[bk, hd], accs[hd])
    for hd in range(DIL_HEADS):
        o_ref[0, hd] = accs[hd]


def _bias_tiles(rel_bias):
    buckets = jnp.asarray(_bucket_tiles())
    n = buckets.shape[0]
    return pl.pallas_call(
        _bias_kernel,
        grid=(n,),
        in_specs=[pl.BlockSpec(memory_space=pltpu.SMEM),
                  pl.BlockSpec((1, Q_TILE_DIL, K_WIN_DIL), lambda p: (p, 0, 0))],
        out_specs=pl.BlockSpec((1, DIL_HEADS, Q_TILE_DIL, K_WIN_DIL), lambda p: (p, 0, 0, 0)),
        out_shape=jax.ShapeDtypeStruct((n, DIL_HEADS, Q_TILE_DIL, K_WIN_DIL), F32),
        compiler_params=_params(("arbitrary",)),
        name="dil_bias",
    )(rel_bias.astype(F32), buckets)


DIL_TILES_PER_ITER = 16
DIL_COPY_ROWS = 256
DIL_MERGE_ROWS = 256


def _dil_kernel(q_ref, k_ref, v_ref, bias_ref, o_ref,
                k0_ref, k1_ref, vp_ref, op_ref, lp_ref, *, seq):
    lane = lax.broadcasted_iota(jnp.int32, (1, LANES), 1)
    low = lane < DIL_HEAD_DIM
    zeros = jnp.zeros((BAND_HALF, LANES), BF16)

    def rows(start, size, dil):
        return pl.ds(start, size) if dil == 1 else pl.ds(start, size, stride=dil)

    for p, (_, dil) in enumerate(DIL_PATTERNS):
        length = seq // dil
        ntile = length // Q_TILE_DIL
        span = length + 2 * BAND_HALF

        def deinterleave(r, carry, dil=dil, length=length, span=span):
            base = pl.multiple_of(r * span, 2 * BAND_HALF)
            for ref in (k0_ref, k1_ref, vp_ref):
                ref[pl.ds(base, BAND_HALF), :] = zeros
                ref[pl.ds(base + BAND_HALF + length, BAND_HALF), :] = zeros
            for cidx in range(length // DIL_COPY_ROWS):
                src = rows(r + dil * DIL_COPY_ROWS * cidx, DIL_COPY_ROWS, dil)
                dst = pl.ds(pl.multiple_of(base + BAND_HALF + DIL_COPY_ROWS * cidx, BAND_HALF),
                            DIL_COPY_ROWS)
                kk = k_ref[0, src, :]
                k0_ref[dst, :] = jnp.where(low, kk, 0.0).astype(BF16)
                k1_ref[dst, :] = jnp.where(low, 0.0, kk).astype(BF16)
                vp_ref[dst, :] = v_ref[0, src, :].astype(BF16)
            return carry

        lax.fori_loop(0, dil, deinterleave, 0)

        def tiles(it, carry, p=p, dil=dil, ntile=ntile, span=span):
            for u in range(DIL_TILES_PER_ITER):
                t = it * DIL_TILES_PER_ITER + u
                r = t // ntile
                i = t % ntile
                edge = jnp.where(i == 0, 1, jnp.where(i == ntile - 1, 2, 0))
                r0 = i * Q_TILE_DIL
                tok = rows(r + dil * r0, Q_TILE_DIL, dil)
                win = pl.ds(pl.multiple_of(r * span + r0, Q_TILE_DIL), K_WIN_DIL)
                q = q_ref[0, tok, :].astype(BF16)
                vw = vp_ref[win, :]
                outs, lses = [], []
                for j, kref in enumerate((k0_ref, k1_ref)):
                    s = _dot_nt(q, kref[win, :]) + bias_ref[p * N_EDGE + edge, j]
                    m = jnp.max(s, axis=-1, keepdims=True)
                    e = jnp.exp(s - m)
                    den = jnp.sum(e, axis=-1, keepdims=True)
                    outs.append(_dot(e.astype(BF16), vw) / den)
                    lses.append(m + jnp.log(den))
                op_ref[p, tok, :] = jnp.where(low, outs[0], outs[1])
                lp_ref[p, tok, :] = jnp.where(low, lses[0], lses[1])
            return carry

        lax.fori_loop(0, seq // Q_TILE_DIL // DIL_TILES_PER_ITER, tiles, 0)

    def merge(g, carry):
        sl = pl.ds(pl.multiple_of(g * DIL_MERGE_ROWS, DIL_MERGE_ROWS), DIL_MERGE_ROWS)
        ls = [lp_ref[p, sl, :] for p in range(len(DIL_PATTERNS))]
        m = functools.reduce(jnp.maximum, ls)
        es = [jnp.exp(l - m) for l in ls]
        num = functools.reduce(lambda a, b: a + b,
                               [e * op_ref[p, sl, :] for p, e in enumerate(es)])
        den = functools.reduce(lambda a, b: a + b, es)
        o_ref[0, sl, :] = (num / den).astype(o_ref.dtype)
        return carry

    lax.fori_loop(0, seq // DIL_MERGE_ROWS, merge, 0)


def _dilated(dq, dk, dv, bias):
    bsz, s, w = dq.shape
    pairs = DIL_HEADS // 2
    npat = len(DIL_PATTERNS)
    blk = pl.BlockSpec((1, s, LANES), lambda b, p: (b, 0, p))
    pad_rows = s + 2 * BAND_HALF * max(dl for _, dl in DIL_PATTERNS)
    assert (s // Q_TILE_DIL) % DIL_TILES_PER_ITER == 0
    return pl.pallas_call(
        functools.partial(_dil_kernel, seq=s),
        grid=(bsz, pairs),
        in_specs=[blk, blk, blk,
                  pl.BlockSpec((npat * N_EDGE, 2, Q_TILE_DIL, K_WIN_DIL), lambda b, p: (0, p, 0, 0))],
        out_specs=blk,
        out_shape=jax.ShapeDtypeStruct((bsz, s, w), BF16),
        scratch_shapes=[pltpu.VMEM((pad_rows, LANES), BF16)] * 3
                       + [pltpu.VMEM((npat, s, LANES), F32)] * 2,
        compiler_params=_params(("parallel", "arbitrary")),
        name="dilated",
    )(dq, dk, dv, bias)


def _mix_kernel(x_ref, mla_ref, dil_ref,
                g1_ref, sc2_ref, sh2_ref, wout_ref, lg_ref, lb_ref, wr_ref,
                x1_ref, h2_ref, aff_ref, *, alpha):
    mix = _dot(mla_ref[0], wout_ref[0:MLA_WIDTH, :]) + _dot(dil_ref[0], wout_ref[MLA_WIDTH:, :])
    y = alpha * x_ref[0] + g1_ref[0] * mix
    x1 = _layer_norm(y, lg_ref[...], lb_ref[...])
    x1_ref[0] = x1
    h2 = x1 * (1.0 + sc2_ref[0]) + sh2_ref[0]
    h2_ref[0] = h2.reshape(h2.shape[0], 1, h2.shape[1])
    logits = lax.dot_general(wr_ref[...], h2, (((1,), (1,)), ((), ())),
                             preferred_element_type=F32,
                             precision=lax.Precision.HIGHEST)
    mx = jnp.max(logits, axis=0, keepdims=True)
    ex = jnp.exp(logits - mx)
    aff_ref[0] = ex / jnp.sum(ex, axis=0, keepdims=True)


def _mix(x, mla, dil, mod3, w_out, ln_g, ln_b, w_router_t, alpha, tm=512):
    bsz, s, d = x.shape
    row = lambda w: pl.BlockSpec((1, tm, w), lambda b, i: (b, i, 0))
    modspec = lambda k: pl.BlockSpec((1, 1, d), lambda b, i: (6 * b + k, 0, 0))
    full = lambda a: pl.BlockSpec(a.shape, lambda b, i: (0,) * a.ndim)
    return pl.pallas_call(
        functools.partial(_mix_kernel, alpha=alpha),
        grid=(bsz, s // tm),
        in_specs=[row(d), row(MLA_WIDTH), row(DIL_WIDTH)]
                 + [modspec(2), modspec(4), modspec(3),
                    full(w_out), full(ln_g), full(ln_b), full(w_router_t)],
        out_specs=[row(d), pl.BlockSpec((1, tm, 1, d), lambda b, i: (b, i, 0, 0)),
                   pl.BlockSpec((1, N_EXPERTS, tm), lambda b, i: (b, 0, i))],
        out_shape=[jax.ShapeDtypeStruct((bsz, s, d), F32),
                   jax.ShapeDtypeStruct((bsz, s, 1, d), F32),
                   jax.ShapeDtypeStruct((bsz, N_EXPERTS, s), F32)],
        compiler_params=_params(("parallel", "arbitrary")),
        name="mix",
    )(x, mla, dil, mod3, mod3, mod3, w_out, ln_g, ln_b, w_router_t)


def _cumsum_lanes(x, tri):
    rows, n = x.shape
    carry = jnp.zeros((rows, 1), F32)
    parts = []
    for j in range(n // LANES):
        inc = _dot(x[:, j * LANES:(j + 1) * LANES].astype(BF16), tri) + carry
        parts.append(inc)
        carry = inc[:, LANES - 1:LANES]
    return jnp.concatenate(parts, axis=1)


TOPK_EXPERTS_PER_ITER = 2


def _topk_kernel(aff_ref, idx_ref, val_ref, key_ref, *, cap):
    a = aff_ref[0]
    n_e, s = a.shape
    bits = pltpu.bitcast(a, jnp.int32)
    t = jnp.zeros((n_e, 1), jnp.int32)
    for bit in range(30, -1, -1):
        cand = t | (1 << bit)
        cnt = jnp.sum((bits >= cand).astype(jnp.int32), axis=1, keepdims=True)
        t = jnp.where(cnt >= cap, cand, t)
    gt = bits > t
    eq = bits == t
    n_gt = jnp.sum(gt.astype(jnp.int32), axis=1, keepdims=True)
    ri = lax.broadcasted_iota(jnp.int32, (LANES, LANES), 0)
    ci = lax.broadcasted_iota(jnp.int32, (LANES, LANES), 1)
    tri = jnp.where(ri <= ci, 1.0, 0.0).astype(BF16)
    eq_f = jnp.where(eq, 1.0, 0.0)
    rank_eq = _cumsum_lanes(eq_f, tri) - eq_f
    sel = gt | (eq & (rank_eq < (cap - n_gt).astype(F32)))
    sel_f = jnp.where(sel, 1.0, 0.0)
    pos = _cumsum_lanes(sel_f, tri) - sel_f
    key_ref[...] = jnp.where(sel, pos.astype(jnp.int32), -1)

    tok = lax.broadcasted_iota(jnp.int32, (1, s), 1)
    tok_hi = (tok >> 6).astype(F32)
    tok_lo = (tok & 63).astype(F32)
    slot = lax.broadcasted_iota(jnp.int32, (cap, 1), 0)
    rid = lax.broadcasted_iota(jnp.int32, (8, s), 0)

    def per_expert(e):
        key = key_ref[pl.ds(e, 1), :]
        onehot = jnp.where(key == slot, 1.0, 0.0).astype(BF16)
        ar = aff_ref[0, pl.ds(e, 1), :]
        a_hi = ar.astype(BF16).astype(F32)
        r1 = ar - a_hi
        a_mid = r1.astype(BF16).astype(F32)
        a_lo = r1 - a_mid
        lhs = jnp.where(rid == 0, tok_hi,
              jnp.where(rid == 1, tok_lo,
              jnp.where(rid == 2, a_hi,
              jnp.where(rid == 3, a_mid,
              jnp.where(rid == 4, a_lo, 0.0))))).astype(BF16)
        res = _dot_nt(lhs, onehot)
        idx_ref[0, pl.ds(e, 1), :] = (res[0:1] * 64.0 + res[1:2]).astype(jnp.int32)
        val_ref[0, pl.ds(e, 1), :] = res[2:3] + res[3:4] + res[4:5]

    def expert_group(g, carry):
        for u in range(TOPK_EXPERTS_PER_ITER):
            per_expert(g * TOPK_EXPERTS_PER_ITER + u)
        return carry

    lax.fori_loop(0, n_e // TOPK_EXPERTS_PER_ITER, expert_group, 0)


def _topk(aff, cap):
    bsz, n_e, s = aff.shape
    return pl.pallas_call(
        functools.partial(_topk_kernel, cap=cap),
        grid=(bsz,),
        in_specs=[pl.BlockSpec((1, n_e, s), lambda b: (b, 0, 0))],
        out_specs=[pl.BlockSpec((1, n_e, cap), lambda b: (b, 0, 0)),
                   pl.BlockSpec((1, n_e, cap), lambda b: (b, 0, 0))],
        out_shape=[jax.ShapeDtypeStruct((bsz, n_e, cap), jnp.int32),
                   jax.ShapeDtypeStruct((bsz, n_e, cap), F32)],
        scratch_shapes=[pltpu.VMEM((n_e, s), jnp.int32)],
        compiler_params=_params(("parallel",)),
        name="topk",
    )(aff)


GATHER_UNROLL = 8


def _gather_kernel(idx_ref, h_ref, o_ref, *, cap):
    def chunk(g, carry):
        base = g * GATHER_UNROLL
        for u in range(GATHER_UNROLL):
            c = base + u
            o_ref[0, 0, c] = h_ref[0, idx_ref[0, 0, c]]
        return carry

    lax.fori_loop(0, cap // GATHER_UNROLL, chunk, 0)


def _gather(idx3, h4, n_e, cap):
    bsz, s, _, d = h4.shape
    return pl.pallas_call(
        functools.partial(_gather_kernel, cap=cap),
        grid=(bsz, n_e),
        in_specs=[pl.BlockSpec((1, 1, cap), lambda b, e: (b * n_e + e, 0, 0),
                               memory_space=pltpu.SMEM),
                  pl.BlockSpec((1, s, 1, d), lambda b, e: (b, 0, 0, 0))],
        out_specs=pl.BlockSpec((1, 1, cap, 1, d), lambda b, e: (b, e, 0, 0, 0)),
        out_shape=jax.ShapeDtypeStruct((bsz, n_e, cap, 1, d), F32),
        compiler_params=_params(("parallel", "arbitrary")),
        name="gather",
    )(idx3, h4)


def _ffn_kernel(x_ref, wg_hbm, wu_hbm, wd_hbm, y_ref,
                wg_bf, wu_bf, wd_bf, stage_g, stage_u, stage_d, sems):
    e = pl.program_id(0)
    b = pl.program_id(1)
    n_e = pl.num_programs(0)
    n_slices = pl.num_programs(1)
    rows_g = stage_g.shape[0]
    rows_d = stage_d.shape[0]
    slot = e % 2

    def slice_copies(expert, i):
        return (pltpu.make_async_copy(wg_hbm.at[expert, pl.ds(i * rows_g, rows_g), :], stage_g, sems.at[0]),
                pltpu.make_async_copy(wu_hbm.at[expert, pl.ds(i * rows_g, rows_g), :], stage_u, sems.at[1]),
                pltpu.make_async_copy(wd_hbm.at[expert, pl.ds(i * rows_d, rows_d), :], stage_d, sems.at[2]))

    def cast_slice(dst_slot, i):
        wg_bf[dst_slot, pl.ds(pl.multiple_of(i * rows_g, 16), rows_g), :] = stage_g[...].astype(BF16)
        wu_bf[dst_slot, pl.ds(pl.multiple_of(i * rows_g, 16), rows_g), :] = stage_u[...].astype(BF16)
        wd_bf[dst_slot, pl.ds(pl.multiple_of(i * rows_d, 16), rows_d), :] = stage_d[...].astype(BF16)

    def fetched_expert(expert):
        return jnp.minimum(expert + 1, n_e - 1)

    @pl.when(jnp.logical_and(e == 0, b == 0))
    def _():
        def load(i, carry):
            cps = slice_copies(0, i)
            for cp in cps:
                cp.start()
            for cp in cps:
                cp.wait()
            cast_slice(0, i)
            return carry
        lax.fori_loop(0, n_slices, load, 0)
        for cp in slice_copies(fetched_expert(0), 0):
            cp.start()

    x = x_ref[0, 0]
    gt = _dot(x, wg_bf[slot])
    up = _dot(x, wu_bf[slot])
    act = (gt * (1.0 / (1.0 + jnp.exp(-gt))) * up).astype(BF16)
    y = _dot(act, wd_bf[slot])
    y_ref[0, 0] = y.reshape(y.shape[0], 1, y.shape[1])

    for cp in slice_copies(fetched_expert(e), b):
        cp.wait()
    cast_slice(1 - slot, b)

    @pl.when(jnp.logical_or(e + 1 < n_e, b + 1 < n_slices))
    def _():
        wrap = b + 1 == n_slices
        e2 = jnp.where(wrap, e + 1, e)
        b2 = jnp.where(wrap, 0, b + 1)
        for cp in slice_copies(fetched_expert(e2), b2):
            cp.start()


def _ffn(xin, wg, wu, wd):
    bsz, n_e, cap, d = xin.shape
    f = wg.shape[-1]
    assert d % (16 * bsz) == 0 and f % (16 * bsz) == 0
    hbm = pl.BlockSpec(memory_space=pl.ANY)
    return pl.pallas_call(
        _ffn_kernel,
        grid=(n_e, bsz),
        in_specs=[pl.BlockSpec((1, 1, cap, d), lambda e, b: (b, e, 0, 0)), hbm, hbm, hbm],
        out_specs=pl.BlockSpec((1, 1, cap, 1, d), lambda e, b: (b, e, 0, 0, 0)),
        out_shape=jax.ShapeDtypeStruct((bsz, n_e, cap, 1, d), F32),
        scratch_shapes=[pltpu.VMEM((2, d, f), BF16), pltpu.VMEM((2, d, f), BF16), pltpu.VMEM((2, f, d), BF16),
                        pltpu.VMEM((d // bsz, f), F32), pltpu.VMEM((d // bsz, f), F32),
                        pltpu.VMEM((f // bsz, d), F32), pltpu.SemaphoreType.DMA((3,))],
        compiler_params=_params(("arbitrary", "arbitrary")),
        name="ffn",
    )(xin, wg, wu, wd)


SCATTER_UNROLL = 8


def _combine_kernel(idx_ref, val_ref, y_ref, o_ref, *, cap):
    @pl.when(pl.program_id(1) == 0)
    def _():
        o_ref[...] = jnp.zeros_like(o_ref)

    def chunk(g, carry):
        base = g * SCATTER_UNROLL
        rows = []
        for u in range(SCATTER_UNROLL):
            c = base + u
            t = idx_ref[0, 0, c]
            rows.append((t, o_ref[0, t] + val_ref[0, 0, c] * y_ref[0, 0, c]))
        for t, r in rows:
            o_ref[0, t] = r
        return carry

    lax.fori_loop(0, cap // SCATTER_UNROLL, chunk, 0)


def _combine(idx3, val3, y5, s):
    bsz, n_e, cap, _, d = y5.shape
    sm = lambda: pl.BlockSpec((1, 1, cap), lambda b, e: (b * n_e + e, 0, 0),
                              memory_space=pltpu.SMEM)
    return pl.pallas_call(
        functools.partial(_combine_kernel, cap=cap),
        grid=(bsz, n_e),
        in_specs=[sm(), sm(),
                  pl.BlockSpec((1, 1, cap, 1, d), lambda b, e: (b, e, 0, 0, 0))],
        out_specs=pl.BlockSpec((1, s, 1, d), lambda b, e: (b, 0, 0, 0)),
        out_shape=jax.ShapeDtypeStruct((bsz, s, 1, d), F32),
        compiler_params=_params(("parallel", "arbitrary")),
        name="combine",
    )(idx3, val3, y5)


def _final_kernel(x1_ref, moe_ref, g2_ref, lg_ref, lb_ref, o_ref, *, alpha):
    y = alpha * x1_ref[0] + g2_ref[0] * moe_ref[0]
    o_ref[0] = _layer_norm(y, lg_ref[...], lb_ref[...])


def _final(x1, moe, mod3, ln_g, ln_b, alpha, tm=512):
    bsz, s, d = x1.shape
    row = pl.BlockSpec((1, tm, d), lambda b, i: (b, i, 0))
    vec = pl.BlockSpec((1, d), lambda b, i: (0, 0))
    return pl.pallas_call(
        functools.partial(_final_kernel, alpha=alpha),
        grid=(bsz, s // tm),
        in_specs=[row, row, pl.BlockSpec((1, 1, d), lambda b, i: (6 * b + 5, 0, 0)), vec, vec],
        out_specs=row,
        out_shape=jax.ShapeDtypeStruct((bsz, s, d), F32),
        compiler_params=_params(("parallel", "arbitrary")),
        name="final",
    )(x1, moe, mod3, ln_g, ln_b)


def _rope_tables(s):
    inv = ROPE_THETA ** (-jnp.arange(0, MLA_ROPE, 2, dtype=F32) / MLA_ROPE)
    ang = jnp.arange(s, dtype=F32)[:, None] * inv[None, :]
    cos, sin = jnp.cos(ang), jnp.sin(ang)
    scale = (MLA_NOPE + MLA_ROPE) ** -0.5 * math.log2(math.e)
    z32 = jnp.zeros((s, HEAD_SLAB - MLA_NOPE - MLA_ROPE), F32)
    z64 = jnp.zeros((s, MLA_NOPE), F32)
    cosq = jnp.concatenate([jnp.full((s, MLA_NOPE), scale, F32), cos * scale, cos * scale, z32], 1)
    sinq_scaled = jnp.concatenate([z64, sin * scale, sin * scale, z32], 1)
    cosk = jnp.concatenate([z64, cos, cos, z32], 1)
    sink = jnp.concatenate([z64, sin, sin, z32], 1)
    return cosq, sinq_scaled, cosk, sink


def _layout_weights(w_in, w_uq, w_ukv):
    d = w_in.shape[0]
    c0 = MLA_Q_LORA + MLA_KV_LORA
    kr = w_in[:, c0:c0 + MLA_ROPE]
    t1, t2 = kr[:, :HALF_ROPE], kr[:, HALF_ROPE:]
    z = lambda n: jnp.zeros((d, n), w_in.dtype)
    pad = HEAD_SLAB - MLA_NOPE - MLA_ROPE
    w_in_r = jnp.concatenate(
        [w_in[:, :c0], w_in[:, c0 + MLA_ROPE:],
         z(MLA_NOPE), t1, t2, z(pad),
         z(MLA_NOPE), -t2, t1, z(pad)], axis=1).astype(BF16)

    ql = w_uq.shape[0]
    wq = w_uq.reshape(ql, MLA_HEADS, MLA_NOPE + MLA_ROPE)
    qn, q1, q2 = wq[..., :MLA_NOPE], wq[..., MLA_NOPE:MLA_NOPE + HALF_ROPE], wq[..., MLA_NOPE + HALF_ROPE:]
    zq = lambda n: jnp.zeros((ql, MLA_HEADS, n), w_uq.dtype)
    wq_pre = jnp.concatenate([qn, q1, q2, zq(pad)], -1).reshape(ql, -1).astype(BF16)
    wq_sw = jnp.concatenate([zq(MLA_NOPE), -q2, q1, zq(pad)], -1).reshape(ql, -1).astype(BF16)

    kl = w_ukv.shape[0]
    wkv = w_ukv.reshape(kl, MLA_HEADS, MLA_NOPE + MLA_V)
    wk = jnp.concatenate([wkv[..., :MLA_NOPE],
                          jnp.zeros((kl, MLA_HEADS, HEAD_SLAB - MLA_NOPE), w_ukv.dtype)],
                         -1).reshape(kl, -1).astype(BF16)
    wv = jnp.concatenate([wkv[..., MLA_NOPE:],
                          jnp.zeros((kl, MLA_HEADS, HEAD_SLAB - MLA_V), w_ukv.dtype)],
                         -1).reshape(kl, -1).T.astype(BF16)
    return w_in_r, wq_pre, wq_sw, wk, wv


def kernel(x, c, w_ada, b_ada, w_in, q_norm_g, w_uq, kv_norm_g, w_ukv, rel_bias, w_out, ln1_g,
           ln1_b, w_router, w_gate, w_up, w_down, ln2_g, ln2_b):
    bsz, s, d = x.shape
    depth = w_ada.shape[0]
    alpha = (2 * depth) ** 0.25
    cap = max(1, EC_CAPACITY_FACTOR * s // N_EXPERTS)
    assert s % (max(2 * Q_TILE_DIL, DIL_COPY_ROWS) * max(dl for _, dl in DIL_PATTERNS)) == 0
    assert all(win // 2 // dl == BAND_HALF for win, dl in DIL_PATTERNS)

    cosq, sinq, cosk, sink = _rope_tables(s)
    bias = _bias_tiles(rel_bias)

    for l in range(depth):
        mod3 = _ada(c, w_ada[l], b_ada[l]).reshape(bsz * 6, 1, d)
        w_in_r, wq_pre, wq_sw, wk, wv = _layout_weights(w_in[l], w_uq[l], w_ukv[l])
        q, k, v, dq, dk, dv = _proj(
            x, mod3, w_in_r, q_norm_g[l].reshape(1, -1), wq_pre, wq_sw,
            kv_norm_g[l].reshape(1, -1), wk, wv, cosq, sinq, cosk, sink)
        mla = _mla(q, k, v)
        dil_out = _dilated(dq, dk, dv, bias)
        x1, h2, aff = _mix(x, mla, dil_out, mod3, w_out[l].astype(BF16),
                           ln1_g[l].reshape(1, d), ln1_b[l].reshape(1, d),
                           w_router[l].T, alpha)
        idx, vals = _topk(aff, cap)
        idx3 = idx.reshape(bsz * N_EXPERTS, 1, cap)
        val3 = vals.reshape(bsz * N_EXPERTS, 1, cap)
        xin = _gather(idx3, h2, N_EXPERTS, cap)
        xin = xin.reshape(bsz, N_EXPERTS, cap, d).astype(BF16)
        y = _ffn(xin, w_gate[l], w_up[l], w_down[l])
        moe = _combine(idx3, val3, y, s)
        x = _final(x1, moe.reshape(bsz, s, d), mod3, ln2_g[l].reshape(1, d),
                   ln2_b[l].reshape(1, d), alpha)
    return x
```

```python
import functools
import math

import numpy as np
import jax
import jax.numpy as jnp
from jax import lax
from jax.experimental import pallas as pl
from jax.experimental.pallas import tpu as pltpu

MLA_HEADS = 8
MLA_NOPE = 64
MLA_ROPE = 32
MLA_V = 64
MLA_Q_LORA = 384
MLA_KV_LORA = 256
ROPE_THETA = 10000.0
DIL_HEADS = 8
DIL_HEAD_DIM = 64
DIL_PATTERNS = ((128, 1), (512, 4), (2048, 16))
REL_BUCKETS = 32
REL_MAX_EXACT = 8
REL_MAX_DIST = 1024
N_EXPERTS = 16
EC_CAPACITY_FACTOR = 2
NORM_EPS = 1e-6
NEG_INF = -1e30

LANES = 128
HEAD_SLAB = 128
VMEM_LIMIT = 48 * 1024 * 1024
F32 = jnp.float32
BF16 = jnp.bfloat16
HALF_ROPE = MLA_ROPE // 2
DIL_WIDTH = DIL_HEADS * DIL_HEAD_DIM
MLA_WIDTH = MLA_HEADS * MLA_V
BAND_HALF = 64
Q_TILE_DIL = 2 * BAND_HALF
K_WIN_DIL = 4 * BAND_HALF


def _params(sem, vmem=VMEM_LIMIT):
    return pltpu.CompilerParams(dimension_semantics=sem, vmem_limit_bytes=vmem)


def _dot(a, b):
    return jnp.dot(a, b, preferred_element_type=F32)


def _dot_nt(a, b):
    return lax.dot_general(a, b, (((1,), (1,)), ((), ())), preferred_element_type=F32)


def _layer_norm(y, g, b):
    mu = jnp.mean(y, axis=-1, keepdims=True)
    d = y - mu
    var = jnp.mean(d * d, axis=-1, keepdims=True)
    return d * lax.rsqrt(var + NORM_EPS) * g + b


def _ada_kernel(c_ref, w_ref, b_ref, o_ref):
    c = c_ref[...]
    s = c * (1.0 / (1.0 + jnp.exp(-c)))
    o_ref[...] = jnp.dot(s, w_ref[...], preferred_element_type=F32,
                         precision=lax.Precision.HIGHEST) + b_ref[...]


def _ada(c, w_ada, b_ada):
    bsz, d = c.shape
    n = w_ada.shape[1]
    tn = 1024
    return pl.pallas_call(
        _ada_kernel,
        grid=(n // tn,),
        in_specs=[pl.BlockSpec((bsz, d), lambda j: (0, 0)),
                  pl.BlockSpec((d, tn), lambda j: (0, j)),
                  pl.BlockSpec((1, tn), lambda j: (0, j))],
        out_specs=pl.BlockSpec((bsz, tn), lambda j: (0, j)),
        out_shape=jax.ShapeDtypeStruct((bsz, n), F32),
        compiler_params=_params(("arbitrary",)),
        name="ada",
    )(c, w_ada, b_ada.reshape(1, n))


_C_Q = 0
_C_KV = MLA_Q_LORA
_C_DQ = MLA_Q_LORA + MLA_KV_LORA
_C_DK = _C_DQ + DIL_WIDTH
_C_DV = _C_DK + DIL_WIDTH
_C_KR = _C_DV + DIL_WIDTH
_C_KRS = _C_KR + HEAD_SLAB
_C_END = _C_KRS + HEAD_SLAB


def _proj_kernel(x_ref, sc_ref, sh_ref, win_ref, gq_ref, wqp_ref, wqs_ref, gkv_ref,
                 wk_ref, wv_ref, cosq_ref, sinq_ref, cosk_ref, sink_ref,
                 q_ref, k_ref, v_ref, dq_ref, dk_ref, dv_ref):
    h = (x_ref[0] * (1.0 + sc_ref[0]) + sh_ref[0]).astype(BF16)
    proj = _dot(h, win_ref[...])

    c_q = proj[:, _C_Q:_C_KV]
    cqn = (c_q * lax.rsqrt(jnp.mean(c_q * c_q, axis=-1, keepdims=True) + NORM_EPS)
           * gq_ref[...]).astype(BF16)
    q_pre = _dot(cqn, wqp_ref[...])
    q_sw = _dot(cqn, wqs_ref[...])
    cosq = cosq_ref[...]
    sinq = sinq_ref[...]

    c_kv = proj[:, _C_KV:_C_DQ]
    ckvn = (c_kv * lax.rsqrt(jnp.mean(c_kv * c_kv, axis=-1, keepdims=True) + NORM_EPS)
            * gkv_ref[...]).astype(BF16)
    k_nope = _dot(ckvn, wk_ref[...])
    vt = _dot_nt(wv_ref[...], ckvn)
    vrow = lax.broadcasted_iota(jnp.int32, vt.shape, 0)
    vt = jnp.where((vrow & MLA_V) != 0, 1.0, vt).astype(BF16)
    v_ref[0] = vt.reshape(MLA_HEADS, HEAD_SLAB, vt.shape[1])
    k_rope = proj[:, _C_KR:_C_KRS] * cosk_ref[...] + proj[:, _C_KRS:_C_END] * sink_ref[...]

    for hd in range(MLA_HEADS):
        sl = slice(hd * HEAD_SLAB, (hd + 1) * HEAD_SLAB)
        q_ref[0, :, sl] = (q_pre[:, sl] * cosq + q_sw[:, sl] * sinq).astype(BF16)
        k_ref[0, :, sl] = (k_nope[:, sl] + k_rope).astype(BF16)

    dq_ref[0] = proj[:, _C_DQ:_C_DK] * (DIL_HEAD_DIM ** -0.5)
    dk_ref[0] = proj[:, _C_DK:_C_DV]
    dv_ref[0] = proj[:, _C_DV:_C_KR]


def _proj(x, mod3, w_in_r, gq, wq_pre, wq_sw, gkv, wk, wv, cosq, sinq, cosk, sink, tm=512):
    bsz, s, d = x.shape
    full = lambda a: pl.BlockSpec(a.shape, lambda b, i: (0,) * a.ndim)
    tab = pl.BlockSpec((tm, LANES), lambda b, i: (i, 0))
    hq = MLA_HEADS * HEAD_SLAB
    outs = [jax.ShapeDtypeStruct((bsz, s, hq), BF16),
            jax.ShapeDtypeStruct((bsz, s, hq), BF16),
            jax.ShapeDtypeStruct((bsz, MLA_HEADS, HEAD_SLAB, s), BF16),
            jax.ShapeDtypeStruct((bsz, s, DIL_WIDTH), F32),
            jax.ShapeDtypeStruct((bsz, s, DIL_WIDTH), F32),
            jax.ShapeDtypeStruct((bsz, s, DIL_WIDTH), F32)]
    ospec = lambda w: pl.BlockSpec((1, tm, w), lambda b, i: (b, i, 0))
    return pl.pallas_call(
        _proj_kernel,
        grid=(bsz, s // tm),
        in_specs=[pl.BlockSpec((1, tm, d), lambda b, i: (b, i, 0)),
                  pl.BlockSpec((1, 1, d), lambda b, i: (6 * b + 1, 0, 0)),
                  pl.BlockSpec((1, 1, d), lambda b, i: (6 * b + 0, 0, 0)),
                  full(w_in_r), full(gq), full(wq_pre), full(wq_sw), full(gkv),
                  full(wk), full(wv), tab, tab, tab, tab],
        out_specs=[ospec(hq), ospec(hq),
                   pl.BlockSpec((1, MLA_HEADS, HEAD_SLAB, tm), lambda b, i: (b, 0, 0, i)),
                   ospec(DIL_WIDTH),
                   ospec(DIL_WIDTH), ospec(DIL_WIDTH)],
        out_shape=outs,
        compiler_params=_params(("parallel", "arbitrary")),
        name="proj",
    )(x, mod3, mod3, w_in_r, gq, wq_pre, wq_sw, gkv, wk, wv, cosq, sinq, cosk, sink)


MLA_KEY_CHUNK = 512


MLA_Q_SUB = 256
MLA_VALUE_PARTS = 4


def _mla_kernel(q_ref, k_ref, vt_ref, o_ref, s_ref, p_ref):
    tq = q_ref.shape[1]
    seq = k_ref.shape[1]
    nchunk = seq // MLA_KEY_CHUNK
    units = [(a, j) for a in range(tq // MLA_Q_SUB) for j in range(2)]
    n = len(units)
    maxima = [None] * n
    heads_out = {}

    def chunk(c):
        return slice(c * MLA_KEY_CHUNK, (c + 1) * MLA_KEY_CHUNK)

    def scores(u, c):
        a, j = units[u]
        sl = slice(j * HEAD_SLAB, (j + 1) * HEAD_SLAB)
        s = _dot_nt(k_ref[0, chunk(c), sl], q_ref[0, a * MLA_Q_SUB:(a + 1) * MLA_Q_SUB, sl])
        s_ref[u % 2, chunk(c), :] = s
        mc = jnp.max(s, axis=0, keepdims=True)
        maxima[u] = mc if maxima[u] is None else jnp.maximum(maxima[u], mc)

    def probs(u, c):
        p_ref[u % 2, chunk(c), :] = jnp.exp2(s_ref[u % 2, chunk(c), :] - maxima[u]).astype(BF16)

    partial = {}

    def values_part(u, part):
        a, j = units[u]
        width = seq // MLA_VALUE_PARTS
        ks = slice(part * width, (part + 1) * width)
        d = _dot(vt_ref[0, j, :, ks], p_ref[u % 2, ks, :])
        partial[u] = d if part == 0 else partial[u] + d

    def values(u):
        a, j = units[u]
        acc = partial.pop(u)
        heads_out[(a, j)] = acc[0:MLA_V] / acc[MLA_V:MLA_V + 1]
        if j == 1:
            o_t = jnp.concatenate([heads_out[(a, 0)], heads_out[(a, 1)]], axis=0)
            o_ref[0, a * MLA_Q_SUB:(a + 1) * MLA_Q_SUB, :] = o_t.T.astype(o_ref.dtype)

    chunks_per_part = nchunk // MLA_VALUE_PARTS
    for stage in range(n + 2):
        for c in range(nchunk):
            if 0 <= stage - 2 < n and c % chunks_per_part == 0:
                values_part(stage - 2, c // chunks_per_part)
            if stage < n:
                scores(stage, c)
            if 0 <= stage - 1 < n:
                probs(stage - 1, c)
        if 0 <= stage - 2 < n:
            values(stage - 2)


def _mla(q, k, vt, tq=2048):
    bsz, s, _ = q.shape
    pairs = MLA_HEADS // 2
    return pl.pallas_call(
        _mla_kernel,
        grid=(bsz, pairs, s // tq),
        in_specs=[pl.BlockSpec((1, tq, 2 * HEAD_SLAB), lambda b, p, i: (b, i, p)),
                  pl.BlockSpec((1, s, 2 * HEAD_SLAB), lambda b, p, i: (b, 0, p)),
                  pl.BlockSpec((1, 2, 2 * MLA_V, s), lambda b, p, i: (b, p, 0, 0))],
        out_specs=pl.BlockSpec((1, tq, 2 * MLA_V), lambda b, p, i: (b, i, p)),
        out_shape=jax.ShapeDtypeStruct((bsz, s, MLA_WIDTH), BF16),
        scratch_shapes=[pltpu.VMEM((2, s, MLA_Q_SUB), F32),
                        pltpu.VMEM((2, s, MLA_Q_SUB), BF16)],
        compiler_params=_params(("parallel", "arbitrary", "arbitrary")),
        name="mla",
    )(q, k, vt)


def _t5_bucket(rel):
    half = REL_BUCKETS // 2
    ret = (rel > 0).astype(np.int32) * half
    n = np.abs(rel)
    large = REL_MAX_EXACT + (np.log(np.maximum(n, 1) / REL_MAX_EXACT)
                             / np.log(REL_MAX_DIST / REL_MAX_EXACT)
                             * (half - REL_MAX_EXACT)).astype(np.int32)
    large = np.minimum(large, half - 1)
    return ret + np.where(n < REL_MAX_EXACT, n, large).astype(np.int32)


def _bucket_tiles():
    a = np.arange(Q_TILE_DIL)[:, None]
    j = np.arange(K_WIN_DIL)[None, :]
    rel = j - BAND_HALF - a
    in_range = [np.ones_like(j, bool), j >= BAND_HALF, j < K_WIN_DIL - BAND_HALF]
    tiles = []
    for _, dil in DIL_PATTERNS:
        for ok in in_range:
            tiles.append(np.where((np.abs(rel) <= BAND_HALF) & ok, _t5_bucket(rel * dil), -1))
    return np.stack(tiles).astype(np.int32)


N_EDGE = 3


def _bias_kernel(rb_ref, bucket_ref, o_ref):
    bucket = bucket_ref[0]
    accs = [jnp.where(bucket < 0, NEG_INF, 0.0).astype(F32) for _ in range(DIL_HEADS)]
    for bk in range(REL_BUCKETS):
        hit = bucket == bk
        for hd in range(DIL_HEADS):
            accs[hd] = jnp.where(hit, rb_ref[bk, hd], accs[hd])
    for hd in range(DIL_HEADS):
        o_ref[0, hd] = accs[hd]


def _bias_tiles(rel_bias):
    buckets = jnp.asarray(_bucket_tiles())
    n = buckets.shape[0]
    return pl.pallas_call(
        _bias_kernel,
        grid=(n,),
        in_specs=[pl.BlockSpec(memory_space=pltpu.SMEM),
                  pl.BlockSpec((1, Q_TILE_DIL, K_WIN_DIL), lambda p: (p, 0, 0))],
        out_specs=pl.BlockSpec((1, DIL_HEADS, Q_TILE_DIL, K_WIN_DIL), lambda p: (p, 0, 0, 0)),
        out_shape=jax.ShapeDtypeStruct((n, DIL_HEADS, Q_TILE_DIL, K_WIN_DIL), F32),
        compiler_params=_params(("arbitrary",)),
        name="dil_bias",
    )(rel_bias.astype(F32), buckets)


DIL_TILES_PER_ITER = 16
DIL_COPY_ROWS = 256
DIL_MERGE_ROWS = 256


def _dil_kernel(q_ref, k_ref, v_ref, bias_ref, o_ref,
                k0_ref, k1_ref, vp_ref, op_ref, lp_ref, *, seq):
    lane = lax.broadcasted_iota(jnp.int32, (1, LANES), 1)
    low = lane < DIL_HEAD_DIM
    zeros = jnp.zeros((BAND_HALF, LANES), BF16)

    def rows(start, size, dil):
        return pl.ds(start, size) if dil == 1 else pl.ds(start, size, stride=dil)

    for p, (_, dil) in enumerate(DIL_PATTERNS):
        length = seq // dil
        ntile = length // Q_TILE_DIL
        span = length + 2 * BAND_HALF

        def deinterleave(r, carry, dil=dil, length=length, span=span):
            base = pl.multiple_of(r * span, 2 * BAND_HALF)
            for ref in (k0_ref, k1_ref, vp_ref):
                ref[pl.ds(base, BAND_HALF), :] = zeros
                ref[pl.ds(base + BAND_HALF + length, BAND_HALF), :] = zeros
            for cidx in range(length // DIL_COPY_ROWS):
                src = rows(r + dil * DIL_COPY_ROWS * cidx, DIL_COPY_ROWS, dil)
                dst = pl.ds(pl.multiple_of(base + BAND_HALF + DIL_COPY_ROWS * cidx, BAND_HALF),
                            DIL_COPY_ROWS)
                kk = k_ref[0, src, :]
                k0_ref[dst, :] = jnp.where(low, kk, 0.0).astype(BF16)
                k1_ref[dst, :] = jnp.where(low, 0.0, kk).astype(BF16)
                vp_ref[dst, :] = v_ref[0, src, :].astype(BF16)
            return carry

        lax.fori_loop(0, dil, deinterleave, 0)

        def tiles(it, carry, p=p, dil=dil, ntile=ntile, span=span):
            for u in range(DIL_TILES_PER_ITER):
                t = it * DIL_TILES_PER_ITER + u
                r = t // ntile
                i = t % ntile
                edge = jnp.where(i == 0, 1, jnp.where(i == ntile - 1, 2, 0))
                r0 = i * Q_TILE_DIL
                tok = rows(r + dil * r0, Q_TILE_DIL, dil)
                win = pl.ds(pl.multiple_of(r * span + r0, Q_TILE_DIL), K_WIN_DIL)
                q = q_ref[0, tok, :].astype(BF16)
                vw = vp_ref[win, :]
                outs, lses = [], []
                for j, kref in enumerate((k0_ref, k1_ref)):
                    s = _dot_nt(q, kref[win, :]) + bias_ref[p * N_EDGE + edge, j]
                    m = jnp.max(s, axis=-1, keepdims=True)
                    e = jnp.exp(s - m)
                    den = jnp.sum(e, axis=-1, keepdims=True)
                    outs.append(_dot(e.astype(BF16), vw) / den)
                    lses.append(m + jnp.log(den))
                op_ref[p, tok, :] = jnp.where(low, outs[0], outs[1])
                lp_ref[p, tok, :] = jnp.where(low, lses[0], lses[1])
            return carry

        lax.fori_loop(0, seq // Q_TILE_DIL // DIL_TILES_PER_ITER, tiles, 0)

    def merge(g, carry):
        sl = pl.ds(pl.multiple_of(g * DIL_MERGE_ROWS, DIL_MERGE_ROWS), DIL_MERGE_ROWS)
        ls = [lp_ref[p, sl, :] for p in range(len(DIL_PATTERNS))]
        m = functools.reduce(jnp.maximum, ls)
        es = [jnp.exp(l - m) for l in ls]
        num = functools.reduce(lambda a, b: a + b,
                               [e * op_ref[p, sl, :] for p, e in enumerate(es)])
        den = functools.reduce(lambda a, b: a + b, es)
        o_ref[0, sl, :] = (num / den).astype(o_ref.dtype)
        return carry

    lax.fori_loop(0, seq // DIL_MERGE_ROWS, merge, 0)


def _dilated(dq, dk, dv, bias):
    bsz, s, w = dq.shape
    pairs = DIL_HEADS // 2
    npat = len(DIL_PATTERNS)
    blk = pl.BlockSpec((1, s, LANES), lambda b, p: (b, 0, p))
    pad_rows = s + 2 * BAND_HALF * max(dl for _, dl in DIL_PATTERNS)
    assert (s // Q_TILE_DIL) % DIL_TILES_PER_ITER == 0
    return pl.pallas_call(
        functools.partial(_dil_kernel, seq=s),
        grid=(bsz, pairs),
        in_specs=[blk, blk, blk,
                  pl.BlockSpec((npat * N_EDGE, 2, Q_TILE_DIL, K_WIN_DIL), lambda b, p: (0, p, 0, 0))],
        out_specs=blk,
        out_shape=jax.ShapeDtypeStruct((bsz, s, w), BF16),
        scratch_shapes=[pltpu.VMEM((pad_rows, LANES), BF16)] * 3
                       + [pltpu.VMEM((npat, s, LANES), F32)] * 2,
        compiler_params=_params(("parallel", "arbitrary")),
        name="dilated",
    )(dq, dk, dv, bias)


def _mix_kernel(x_ref, mla_ref, dil_ref,
                g1_ref, sc2_ref, sh2_ref, wout_ref, lg_ref, lb_ref, wr_ref,
                x1_ref, h2_ref, aff_ref, *, alpha):
    mix = _dot(mla_ref[0], wout_ref[0:MLA_WIDTH, :]) + _dot(dil_ref[0], wout_ref[MLA_WIDTH:, :])
    y = alpha * x_ref[0] + g1_ref[0] * mix
    x1 = _layer_norm(y, lg_ref[...], lb_ref[...])
    x1_ref[0] = x1
    h2 = x1 * (1.0 + sc2_ref[0]) + sh2_ref[0]
    h2_ref[0] = h2.reshape(h2.shape[0], 1, h2.shape[1])
    logits = lax.dot_general(wr_ref[...], h2, (((1,), (1,)), ((), ())),
                             preferred_element_type=F32,
                             precision=lax.Precision.HIGHEST)
    mx = jnp.max(logits, axis=0, keepdims=True)
    ex = jnp.exp(logits - mx)
    aff_ref[0] = ex / jnp.sum(ex, axis=0, keepdims=True)


def _mix(x, mla, dil, mod3, w_out, ln_g, ln_b, w_router_t, alpha, tm=512):
    bsz, s, d = x.shape
    row = lambda w: pl.BlockSpec((1, tm, w), lambda b, i: (b, i, 0))
    modspec = lambda k: pl.BlockSpec((1, 1, d), lambda b, i: (6 * b + k, 0, 0))
    full = lambda a: pl.BlockSpec(a.shape, lambda b, i: (0,) * a.ndim)
    return pl.pallas_call(
        functools.partial(_mix_kernel, alpha=alpha),
        grid=(bsz, s // tm),
        in_specs=[row(d), row(MLA_WIDTH), row(DIL_WIDTH)]
                 + [modspec(2), modspec(4), modspec(3),
                    full(w_out), full(ln_g), full(ln_b), full(w_router_t)],
        out_specs=[row(d), pl.BlockSpec((1, tm, 1, d), lambda b, i: (b, i, 0, 0)),
                   pl.BlockSpec((1, N_EXPERTS, tm), lambda b, i: (b, 0, i))],
        out_shape=[jax.ShapeDtypeStruct((bsz, s, d), F32),
                   jax.ShapeDtypeStruct((bsz, s, 1, d), F32),
                   jax.ShapeDtypeStruct((bsz, N_EXPERTS, s), F32)],
        compiler_params=_params(("parallel", "arbitrary")),
        name="mix",
    )(x, mla, dil, mod3, mod3, mod3, w_out, ln_g, ln_b, w_router_t)


def _cumsum_lanes(x, tri):
    rows, n = x.shape
    carry = jnp.zeros((rows, 1), F32)
    parts = []
    for j in range(n // LANES):
        inc = _dot(x[:, j * LANES:(j + 1) * LANES].astype(BF16), tri) + carry
        parts.append(inc)
        carry = inc[:, LANES - 1:LANES]
    return jnp.concatenate(parts, axis=1)


TOPK_EXPERTS_PER_ITER = 2


def _topk_kernel(aff_ref, idx_ref, val_ref, key_ref, *, cap):
    a = aff_ref[0]
    n_e, s = a.shape
    t = jnp.zeros((n_e, 1), jnp.int32)
    for bit in range(30, -1, -1):
        cand = t | (1 << bit)
        cnt = jnp.sum((a >= pltpu.bitcast(cand, F32)).astype(jnp.int32), axis=1, keepdims=True)
        t = jnp.where(cnt >= cap, cand, t)
    thr = pltpu.bitcast(t, F32)
    gt = a > thr
    eq = a == thr
    n_gt = jnp.sum(gt.astype(jnp.int32), axis=1, keepdims=True)
    ri = lax.broadcasted_iota(jnp.int32, (LANES, LANES), 0)
    ci = lax.broadcasted_iota(jnp.int32, (LANES, LANES), 1)
    tri = jnp.where(ri <= ci, 1.0, 0.0).astype(BF16)
    eq_f = jnp.where(eq, 1.0, 0.0)
    rank_eq = _cumsum_lanes(eq_f, tri) - eq_f
    sel = gt | (eq & (rank_eq < (cap - n_gt).astype(F32)))
    sel_f = jnp.where(sel, 1.0, 0.0)
    pos = _cumsum_lanes(sel_f, tri) - sel_f
    key_ref[...] = jnp.where(sel, pos.astype(jnp.int32), -1)

    tok = lax.broadcasted_iota(jnp.int32, (1, s), 1)
    tok_hi = (tok >> 6).astype(F32)
    tok_lo = (tok & 63).astype(F32)
    slot = lax.broadcasted_iota(jnp.int32, (cap, 1), 0)
    rid = lax.broadcasted_iota(jnp.int32, (8, s), 0)

    def per_expert(e):
        key = key_ref[pl.ds(e, 1), :]
        onehot = jnp.where(key == slot, 1.0, 0.0).astype(BF16)
        ar = aff_ref[0, pl.ds(e, 1), :]
        a_hi = ar.astype(BF16).astype(F32)
        r1 = ar - a_hi
        a_mid = r1.astype(BF16).astype(F32)
        a_lo = r1 - a_mid
        lhs = jnp.where(rid == 0, tok_hi,
              jnp.where(rid == 1, tok_lo,
              jnp.where(rid == 2, a_hi,
              jnp.where(rid == 3, a_mid,
              jnp.where(rid == 4, a_lo, 0.0))))).astype(BF16)
        res = _dot_nt(lhs, onehot)
        idx_ref[0, pl.ds(e, 1), :] = (res[0:1] * 64.0 + res[1:2]).astype(jnp.int32)
        val_ref[0, pl.ds(e, 1), :] = res[2:3] + res[3:4] + res[4:5]

    def expert_group(g, carry):
        for u in range(TOPK_EXPERTS_PER_ITER):
            per_expert(g * TOPK_EXPERTS_PER_ITER + u)
        return carry

    lax.fori_loop(0, n_e // TOPK_EXPERTS_PER_ITER, expert_group, 0)


def _topk(aff, cap):
    bsz, n_e, s = aff.shape
    return pl.pallas_call(
        functools.partial(_topk_kernel, cap=cap),
        grid=(bsz,),
        in_specs=[pl.BlockSpec((1, n_e, s), lambda b: (b, 0, 0))],
        out_specs=[pl.BlockSpec((1, n_e, cap), lambda b: (b, 0, 0)),
                   pl.BlockSpec((1, n_e, cap), lambda b: (b, 0, 0))],
        out_shape=[jax.ShapeDtypeStruct((bsz, n_e, cap), jnp.int32),
                   jax.ShapeDtypeStruct((bsz, n_e, cap), F32)],
        scratch_shapes=[pltpu.VMEM((n_e, s), jnp.int32)],
        compiler_params=_params(("parallel",)),
        name="topk",
    )(aff)


GATHER_UNROLL = 8


def _gather_kernel(idx_ref, h_ref, o_ref, *, cap):
    def chunk(g, carry):
        base = g * GATHER_UNROLL
        for u in range(GATHER_UNROLL):
            c = base + u
            o_ref[0, 0, c] = h_ref[0, idx_ref[0, 0, c]]
        return carry

    lax.fori_loop(0, cap // GATHER_UNROLL, chunk, 0)


def _gather(idx3, h4, n_e, cap):
    bsz, s, _, d = h4.shape
    return pl.pallas_call(
        functools.partial(_gather_kernel, cap=cap),
        grid=(bsz, n_e),
        in_specs=[pl.BlockSpec((1, 1, cap), lambda b, e: (b * n_e + e, 0, 0),
                               memory_space=pltpu.SMEM),
                  pl.BlockSpec((1, s, 1, d), lambda b, e: (b, 0, 0, 0))],
        out_specs=pl.BlockSpec((1, 1, cap, 1, d), lambda b, e: (b, e, 0, 0, 0)),
        out_shape=jax.ShapeDtypeStruct((bsz, n_e, cap, 1, d), F32),
        compiler_params=_params(("parallel", "arbitrary")),
        name="gather",
    )(idx3, h4)


def _ffn_kernel(x_ref, wg_hbm, wu_hbm, wd_hbm, y_ref,
                wg_bf, wu_bf, wd_bf, stage_g, stage_u, stage_d, sems):
    e = pl.program_id(0)
    b = pl.program_id(1)
    n_e = pl.num_programs(0)
    n_slices = pl.num_programs(1)
    rows_g = stage_g.shape[0]
    rows_d = stage_d.shape[0]
    slot = e % 2

    def slice_copies(expert, i):
        return (pltpu.make_async_copy(wg_hbm.at[expert, pl.ds(i * rows_g, rows_g), :], stage_g, sems.at[0]),
                pltpu.make_async_copy(wu_hbm.at[expert, pl.ds(i * rows_g, rows_g), :], stage_u, sems.at[1]),
                pltpu.make_async_copy(wd_hbm.at[expert, pl.ds(i * rows_d, rows_d), :], stage_d, sems.at[2]))

    def cast_slice(dst_slot, i):
        wg_bf[dst_slot, pl.ds(pl.multiple_of(i * rows_g, 16), rows_g), :] = stage_g[...].astype(BF16)
        wu_bf[dst_slot, pl.ds(pl.multiple_of(i * rows_g, 16), rows_g), :] = stage_u[...].astype(BF16)
        wd_bf[dst_slot, pl.ds(pl.multiple_of(i * rows_d, 16), rows_d), :] = stage_d[...].astype(BF16)

    def fetched_expert(expert):
        return jnp.minimum(expert + 1, n_e - 1)

    @pl.when(jnp.logical_and(e == 0, b == 0))
    def _():
        def load(i, carry):
            cps = slice_copies(0, i)
            for cp in cps:
                cp.start()
            for cp in cps:
                cp.wait()
            cast_slice(0, i)
            return carry
        lax.fori_loop(0, n_slices, load, 0)
        for cp in slice_copies(fetched_expert(0), 0):
            cp.start()

    x = x_ref[0, 0]
    gt = _dot(x, wg_bf[slot])
    up = _dot(x, wu_bf[slot])
    act = (gt * (1.0 / (1.0 + jnp.exp(-gt))) * up).astype(BF16)
    y = _dot(act, wd_bf[slot])
    y_ref[0, 0] = y.reshape(y.shape[0], 1, y.shape[1])

    for cp in slice_copies(fetched_expert(e), b):
        cp.wait()
    cast_slice(1 - slot, b)

    @pl.when(jnp.logical_or(e + 1 < n_e, b + 1 < n_slices))
    def _():
        wrap = b + 1 == n_slices
        e2 = jnp.where(wrap, e + 1, e)
        b2 = jnp.where(wrap, 0, b + 1)
        for cp in slice_copies(fetched_expert(e2), b2):
            cp.start()


def _ffn(xin, wg, wu, wd):
    bsz, n_e, cap, d = xin.shape
    f = wg.shape[-1]
    assert d % (16 * bsz) == 0 and f % (16 * bsz) == 0
    hbm = pl.BlockSpec(memory_space=pl.ANY)
    return pl.pallas_call(
        _ffn_kernel,
        grid=(n_e, bsz),
        in_specs=[pl.BlockSpec((1, 1, cap, d), lambda e, b: (b, e, 0, 0)), hbm, hbm, hbm],
        out_specs=pl.BlockSpec((1, 1, cap, 1, d), lambda e, b: (b, e, 0, 0, 0)),
        out_shape=jax.ShapeDtypeStruct((bsz, n_e, cap, 1, d), F32),
        scratch_shapes=[pltpu.VMEM((2, d, f), BF16), pltpu.VMEM((2, d, f), BF16), pltpu.VMEM((2, f, d), BF16),
                        pltpu.VMEM((d // bsz, f), F32), pltpu.VMEM((d // bsz, f), F32),
                        pltpu.VMEM((f // bsz, d), F32), pltpu.SemaphoreType.DMA((3,))],
        compiler_params=_params(("arbitrary", "arbitrary")),
        name="ffn",
    )(xin, wg, wu, wd)


SCATTER_UNROLL = 8


def _combine_kernel(idx_ref, val_ref, y_ref, o_ref, *, cap):
    @pl.when(pl.program_id(1) == 0)
    def _():
        o_ref[...] = jnp.zeros_like(o_ref)

    def chunk(g, carry):
        base = g * SCATTER_UNROLL
        rows = []
        for u in range(SCATTER_UNROLL):
            c = base + u
            t = idx_ref[0, 0, c]
            rows.append((t, o_ref[0, t] + val_ref[0, 0, c] * y_ref[0, 0, c]))
        for t, r in rows:
            o_ref[0, t] = r
        return carry

    lax.fori_loop(0, cap // SCATTER_UNROLL, chunk, 0)


def _combine(idx3, val3, y5, s):
    bsz, n_e, cap, _, d = y5.shape
    sm = lambda: pl.BlockSpec((1, 1, cap), lambda b, e: (b * n_e + e, 0, 0),
                              memory_space=pltpu.SMEM)
    return pl.pallas_call(
        functools.partial(_combine_kernel, cap=cap),
        grid=(bsz, n_e),
        in_specs=[sm(), sm(),
                  pl.BlockSpec((1, 1, cap, 1, d), lambda b, e: (b, e, 0, 0, 0))],
        out_specs=pl.BlockSpec((1, s, 1, d), lambda b, e: (b, 0, 0, 0)),
        out_shape=jax.ShapeDtypeStruct((bsz, s, 1, d), F32),
        compiler_params=_params(("parallel", "arbitrary")),
        name="combine",
    )(idx3, val3, y5)


def _final_kernel(x1_ref, moe_ref, g2_ref, lg_ref, lb_ref, o_ref, *, alpha):
    y = alpha * x1_ref[0] + g2_ref[0] * moe_ref[0]
    o_ref[0] = _layer_norm(y, lg_ref[...], lb_ref[...])


def _final(x1, moe, mod3, ln_g, ln_b, alpha, tm=512):
    bsz, s, d = x1.shape
    row = pl.BlockSpec((1, tm, d), lambda b, i: (b, i, 0))
    vec = pl.BlockSpec((1, d), lambda b, i: (0, 0))
    return pl.pallas_call(
        functools.partial(_final_kernel, alpha=alpha),
        grid=(bsz, s // tm),
        in_specs=[row, row, pl.BlockSpec((1, 1, d), lambda b, i: (6 * b + 5, 0, 0)), vec, vec],
        out_specs=row,
        out_shape=jax.ShapeDtypeStruct((bsz, s, d), F32),
        compiler_params=_params(("parallel", "arbitrary")),
        name="final",
    )(x1, moe, mod3, ln_g, ln_b)


def _rope_tables(s):
    inv = ROPE_THETA ** (-jnp.arange(0, MLA_ROPE, 2, dtype=F32) / MLA_ROPE)
    ang = jnp.arange(s, dtype=F32)[:, None] * inv[None, :]
    cos, sin = jnp.cos(ang), jnp.sin(ang)
    scale = (MLA_NOPE + MLA_ROPE) ** -0.5 * math.log2(math.e)
    z32 = jnp.zeros((s, HEAD_SLAB - MLA_NOPE - MLA_ROPE), F32)
    z64 = jnp.zeros((s, MLA_NOPE), F32)
    cosq = jnp.concatenate([jnp.full((s, MLA_NOPE), scale, F32), cos * scale, cos * scale, z32], 1)
    sinq_scaled = jnp.concatenate([z64, sin * scale, sin * scale, z32], 1)
    cosk = jnp.concatenate([z64, cos, cos, z32], 1)
    sink = jnp.concatenate([z64, sin, sin, z32], 1)
    return cosq, sinq_scaled, cosk, sink


def _layout_weights(w_in, w_uq, w_ukv):
    d = w_in.shape[0]
    c0 = MLA_Q_LORA + MLA_KV_LORA
    kr = w_in[:, c0:c0 + MLA_ROPE]
    t1, t2 = kr[:, :HALF_ROPE], kr[:, HALF_ROPE:]
    z = lambda n: jnp.zeros((d, n), w_in.dtype)
    pad = HEAD_SLAB - MLA_NOPE - MLA_ROPE
    w_in_r = jnp.concatenate(
        [w_in[:, :c0], w_in[:, c0 + MLA_ROPE:],
         z(MLA_NOPE), t1, t2, z(pad),
         z(MLA_NOPE), -t2, t1, z(pad)], axis=1).astype(BF16)

    ql = w_uq.shape[0]
    wq = w_uq.reshape(ql, MLA_HEADS, MLA_NOPE + MLA_ROPE)
    qn, q1, q2 = wq[..., :MLA_NOPE], wq[..., MLA_NOPE:MLA_NOPE + HALF_ROPE], wq[..., MLA_NOPE + HALF_ROPE:]
    zq = lambda n: jnp.zeros((ql, MLA_HEADS, n), w_uq.dtype)
    wq_pre = jnp.concatenate([qn, q1, q2, zq(pad)], -1).reshape(ql, -1).astype(BF16)
    wq_sw = jnp.concatenate([zq(MLA_NOPE), -q2, q1, zq(pad)], -1).reshape(ql, -1).astype(BF16)

    kl = w_ukv.shape[0]
    wkv = w_ukv.reshape(kl, MLA_HEADS, MLA_NOPE + MLA_V)
    wk = jnp.concatenate([wkv[..., :MLA_NOPE],
                          jnp.zeros((kl, MLA_HEADS, HEAD_SLAB - MLA_NOPE), w_ukv.dtype)],
                         -1).reshape(kl, -1).astype(BF16)
    wv = jnp.concatenate([wkv[..., MLA_NOPE:],
                          jnp.zeros((kl, MLA_HEADS, HEAD_SLAB - MLA_V), w_ukv.dtype)],
                         -1).reshape(kl, -1).T.astype(BF16)
    return w_in_r, wq_pre, wq_sw, wk, wv


def kernel(x, c, w_ada, b_ada, w_in, q_norm_g, w_uq, kv_norm_g, w_ukv, rel_bias, w_out, ln1_g,
           ln1_b, w_router, w_gate, w_up, w_down, ln2_g, ln2_b):
    bsz, s, d = x.shape
    depth = w_ada.shape[0]
    alpha = (2 * depth) ** 0.25
    cap = max(1, EC_CAPACITY_FACTOR * s // N_EXPERTS)
    assert s % (max(2 * Q_TILE_DIL, DIL_COPY_ROWS) * max(dl for _, dl in DIL_PATTERNS)) == 0
    assert all(win // 2 // dl == BAND_HALF for win, dl in DIL_PATTERNS)

    cosq, sinq, cosk, sink = _rope_tables(s)
    bias = _bias_tiles(rel_bias)

    for l in range(depth):
        mod3 = _ada(c, w_ada[l], b_ada[l]).reshape(bsz * 6, 1, d)
        w_in_r, wq_pre, wq_sw, wk, wv = _layout_weights(w_in[l], w_uq[l], w_ukv[l])
        q, k, v, dq, dk, dv = _proj(
            x, mod3, w_in_r, q_norm_g[l].reshape(1, -1), wq_pre, wq_sw,
            kv_norm_g[l].reshape(1, -1), wk, wv, cosq, sinq, cosk, sink)
        mla = _mla(q, k, v)
        dil_out = _dilated(dq, dk, dv, bias)
        x1, h2, aff = _mix(x, mla, dil_out, mod3, w_out[l].astype(BF16),
                           ln1_g[l].reshape(1, d), ln1_b[l].reshape(1, d),
                           w_router[l].T, alpha)
        idx, vals = _topk(aff, cap)
        idx3 = idx.reshape(bsz * N_EXPERTS, 1, cap)
        val3 = vals.reshape(bsz * N_EXPERTS, 1, cap)
        xin = _gather(idx3, h2, N_EXPERTS, cap)
        xin = xin.reshape(bsz, N_EXPERTS, cap, d).astype(BF16)
        y = _ffn(xin, w_gate[l], w_up[l], w_down[l])
        moe = _combine(idx3, val3, y, s)
        x = _final(x1, moe.reshape(bsz, s, d), mod3, ln2_g[l].reshape(1, d),
                   ln2_b[l].reshape(1, d), alpha)
    return x
```

```python
import functools
import math

import numpy as np
import jax
import jax.numpy as jnp
from jax import lax
from jax.experimental import pallas as pl
from jax.experimental.pallas import tpu as pltpu

MLA_HEADS = 8
MLA_NOPE = 64
MLA_ROPE = 32
MLA_V = 64
MLA_Q_LORA = 384
MLA_KV_LORA = 256
ROPE_THETA = 10000.0
DIL_HEADS = 8
DIL_HEAD_DIM = 64
DIL_PATTERNS = ((128, 1), (512, 4), (2048, 16))
REL_BUCKETS = 32
REL_MAX_EXACT = 8
REL_MAX_DIST = 1024
N_EXPERTS = 16
EC_CAPACITY_FACTOR = 2
NORM_EPS = 1e-6
NEG_INF = -1e30

LANES = 128
HEAD_SLAB = 128
VMEM_LIMIT = 48 * 1024 * 1024
F32 = jnp.float32
BF16 = jnp.bfloat16
HALF_ROPE = MLA_ROPE // 2
DIL_WIDTH = DIL_HEADS * DIL_HEAD_DIM
MLA_WIDTH = MLA_HEADS * MLA_V
BAND_HALF = 64
Q_TILE_DIL = 2 * BAND_HALF
K_WIN_DIL = 4 * BAND_HALF


def _params(sem, vmem=VMEM_LIMIT):
    return pltpu.CompilerParams(dimension_semantics=sem, vmem_limit_bytes=vmem)


def _dot(a, b):
    return jnp.dot(a, b, preferred_element_type=F32)


def _dot_nt(a, b):
    return lax.dot_general(a, b, (((1,), (1,)), ((), ())), preferred_element_type=F32)


def _layer_norm(y, g, b):
    mu = jnp.mean(y, axis=-1, keepdims=True)
    d = y - mu
    var = jnp.mean(d * d, axis=-1, keepdims=True)
    return d * lax.rsqrt(var + NORM_EPS) * g + b


def _ada_kernel(c_ref, w_ref, b_ref, o_ref):
    c = c_ref[...]
    s = c * (1.0 / (1.0 + jnp.exp(-c)))
    o_ref[...] = jnp.dot(s, w_ref[...], preferred_element_type=F32,
                         precision=lax.Precision.HIGHEST) + b_ref[...]


def _ada(c, w_ada, b_ada):
    bsz, d = c.shape
    n = w_ada.shape[1]
    tn = 1024
    return pl.pallas_call(
        _ada_kernel,
        grid=(n // tn,),
        in_specs=[pl.BlockSpec((bsz, d), lambda j: (0, 0)),
                  pl.BlockSpec((d, tn), lambda j: (0, j)),
                  pl.BlockSpec((1, tn), lambda j: (0, j))],
        out_specs=pl.BlockSpec((bsz, tn), lambda j: (0, j)),
        out_shape=jax.ShapeDtypeStruct((bsz, n), F32),
        compiler_params=_params(("arbitrary",)),
        name="ada",
    )(c, w_ada, b_ada.reshape(1, n))


_C_Q = 0
_C_KV = MLA_Q_LORA
_C_DQ = MLA_Q_LORA + MLA_KV_LORA
_C_DK = _C_DQ + DIL_WIDTH
_C_DV = _C_DK + DIL_WIDTH
_C_KR = _C_DV + DIL_WIDTH
_C_KRS = _C_KR + HEAD_SLAB
_C_END = _C_KRS + HEAD_SLAB


def _proj_kernel(x_ref, sc_ref, sh_ref, win_ref, gq_ref, wqp_ref, wqs_ref, gkv_ref,
                 wk_ref, wv_ref, cosq_ref, sinq_ref, cosk_ref, sink_ref,
                 q_ref, k_ref, v_ref, dq_ref, dk_ref, dv_ref):
    h = (x_ref[0] * (1.0 + sc_ref[0]) + sh_ref[0]).astype(BF16)
    proj = _dot(h, win_ref[...])

    c_q = proj[:, _C_Q:_C_KV]
    cqn = (c_q * lax.rsqrt(jnp.mean(c_q * c_q, axis=-1, keepdims=True) + NORM_EPS)
           * gq_ref[...]).astype(BF16)
    q_pre = _dot(cqn, wqp_ref[...])
    q_sw = _dot(cqn, wqs_ref[...])
    cosq = cosq_ref[...]
    sinq = sinq_ref[...]

    c_kv = proj[:, _C_KV:_C_DQ]
    ckvn = (c_kv * lax.rsqrt(jnp.mean(c_kv * c_kv, axis=-1, keepdims=True) + NORM_EPS)
            * gkv_ref[...]).astype(BF16)
    k_nope = _dot(ckvn, wk_ref[...])
    vt = _dot_nt(wv_ref[...], ckvn)
    vrow = lax.broadcasted_iota(jnp.int32, vt.shape, 0)
    vt = jnp.where((vrow & MLA_V) != 0, 1.0, vt).astype(BF16)
    v_ref[0] = vt.reshape(MLA_HEADS, HEAD_SLAB, vt.shape[1])
    k_rope = proj[:, _C_KR:_C_KRS] * cosk_ref[...] + proj[:, _C_KRS:_C_END] * sink_ref[...]

    for hd in range(MLA_HEADS):
        sl = slice(hd * HEAD_SLAB, (hd + 1) * HEAD_SLAB)
        q_ref[0, :, sl] = (q_pre[:, sl] * cosq + q_sw[:, sl] * sinq).astype(BF16)
        k_ref[0, :, sl] = (k_nope[:, sl] + k_rope).astype(BF16)

    dq_ref[0] = proj[:, _C_DQ:_C_DK] * (DIL_HEAD_DIM ** -0.5)
    dk_ref[0] = proj[:, _C_DK:_C_DV]
    dv_ref[0] = proj[:, _C_DV:_C_KR]


def _proj(x, mod3, w_in_r, gq, wq_pre, wq_sw, gkv, wk, wv, cosq, sinq, cosk, sink, tm=512):
    bsz, s, d = x.shape
    full = lambda a: pl.BlockSpec(a.shape, lambda b, i: (0,) * a.ndim)
    tab = pl.BlockSpec((tm, LANES), lambda b, i: (i, 0))
    hq = MLA_HEADS * HEAD_SLAB
    outs = [jax.ShapeDtypeStruct((bsz, s, hq), BF16),
            jax.ShapeDtypeStruct((bsz, s, hq), BF16),
            jax.ShapeDtypeStruct((bsz, MLA_HEADS, HEAD_SLAB, s), BF16),
            jax.ShapeDtypeStruct((bsz, s, DIL_WIDTH), F32),
            jax.ShapeDtypeStruct((bsz, s, DIL_WIDTH), F32),
            jax.ShapeDtypeStruct((bsz, s, DIL_WIDTH), F32)]
    ospec = lambda w: pl.BlockSpec((1, tm, w), lambda b, i: (b, i, 0))
    return pl.pallas_call(
        _proj_kernel,
        grid=(bsz, s // tm),
        in_specs=[pl.BlockSpec((1, tm, d), lambda b, i: (b, i, 0)),
                  pl.BlockSpec((1, 1, d), lambda b, i: (6 * b + 1, 0, 0)),
                  pl.BlockSpec((1, 1, d), lambda b, i: (6 * b + 0, 0, 0)),
                  full(w_in_r), full(gq), full(wq_pre), full(wq_sw), full(gkv),
                  full(wk), full(wv), tab, tab, tab, tab],
        out_specs=[ospec(hq), ospec(hq),
                   pl.BlockSpec((1, MLA_HEADS, HEAD_SLAB, tm), lambda b, i: (b, 0, 0, i)),
                   ospec(DIL_WIDTH),
                   ospec(DIL_WIDTH), ospec(DIL_WIDTH)],
        out_shape=outs,
        compiler_params=_params(("parallel", "arbitrary")),
        name="proj",
    )(x, mod3, mod3, w_in_r, gq, wq_pre, wq_sw, gkv, wk, wv, cosq, sinq, cosk, sink)


MLA_KEY_CHUNK = 512


MLA_Q_SUB = 256
MLA_VALUE_PARTS = 4


def _mla_kernel(q_ref, k_ref, vt_ref, o_ref, s_ref, p_ref):
    tq = q_ref.shape[1]
    seq = k_ref.shape[1]
    nchunk = seq // MLA_KEY_CHUNK
    units = [(a, j) for a in range(tq // MLA_Q_SUB) for j in range(2)]
    n = len(units)
    maxima = [None] * n
    heads_out = {}

    def chunk(c):
        return slice(c * MLA_KEY_CHUNK, (c + 1) * MLA_KEY_CHUNK)

    def scores(u, c):
        a, j = units[u]
        sl = slice(j * HEAD_SLAB, (j + 1) * HEAD_SLAB)
        s = _dot_nt(k_ref[0, chunk(c), sl], q_ref[0, a * MLA_Q_SUB:(a + 1) * MLA_Q_SUB, sl])
        s_ref[u % 2, chunk(c), :] = s
        mc = jnp.max(s, axis=0, keepdims=True)
        maxima[u] = mc if maxima[u] is None else jnp.maximum(maxima[u], mc)

    def probs(u, c):
        p_ref[u % 2, chunk(c), :] = jnp.exp2(s_ref[u % 2, chunk(c), :] - maxima[u]).astype(BF16)

    partial = {}

    def values_part(u, part):
        a, j = units[u]
        width = seq // MLA_VALUE_PARTS
        ks = slice(part * width, (part + 1) * width)
        d = _dot(vt_ref[0, j, :, ks], p_ref[u % 2, ks, :])
        partial[u] = d if part == 0 else partial[u] + d

    def values(u):
        a, j = units[u]
        acc = partial.pop(u)
        heads_out[(a, j)] = acc[0:MLA_V] / acc[MLA_V:MLA_V + 1]
        if j == 1:
            o_t = jnp.concatenate([heads_out[(a, 0)], heads_out[(a, 1)]], axis=0)
            o_ref[0, a * MLA_Q_SUB:(a + 1) * MLA_Q_SUB, :] = o_t.T.astype(o_ref.dtype)

    chunks_per_part = nchunk // MLA_VALUE_PARTS
    for stage in range(n + 2):
        for c in range(nchunk):
            if 0 <= stage - 2 < n and c % chunks_per_part == 0:
                values_part(stage - 2, c // chunks_per_part)
            if stage < n:
                scores(stage, c)
            if 0 <= stage - 1 < n:
                probs(stage - 1, c)
        if 0 <= stage - 2 < n:
            values(stage - 2)


def _mla(q, k, vt, tq=2048):
    bsz, s, _ = q.shape
    pairs = MLA_HEADS // 2
    return pl.pallas_call(
        _mla_kernel,
        grid=(bsz, pairs, s // tq),
        in_specs=[pl.BlockSpec((1, tq, 2 * HEAD_SLAB), lambda b, p, i: (b, i, p)),
                  pl.BlockSpec((1, s, 2 * HEAD_SLAB), lambda b, p, i: (b, 0, p)),
                  pl.BlockSpec((1, 2, 2 * MLA_V, s), lambda b, p, i: (b, p, 0, 0))],
        out_specs=pl.BlockSpec((1, tq, 2 * MLA_V), lambda b, p, i: (b, i, p)),
        out_shape=jax.ShapeDtypeStruct((bsz, s, MLA_WIDTH), BF16),
        scratch_shapes=[pltpu.VMEM((2, s, MLA_Q_SUB), F32),
                        pltpu.VMEM((2, s, MLA_Q_SUB), BF16)],
        compiler_params=_params(("parallel", "arbitrary", "arbitrary")),
        name="mla",
    )(q, k, vt)


def _t5_bucket(rel):
    half = REL_BUCKETS // 2
    ret = (rel > 0).astype(np.int32) * half
    n = np.abs(rel)
    large = REL_MAX_EXACT + (np.log(np.maximum(n, 1) / REL_MAX_EXACT)
                             / np.log(REL_MAX_DIST / REL_MAX_EXACT)
                             * (half - REL_MAX_EXACT)).astype(np.int32)
    large = np.minimum(large, half - 1)
    return ret + np.where(n < REL_MAX_EXACT, n, large).astype(np.int32)


def _bucket_tiles():
    a = np.arange(Q_TILE_DIL)[:, None]
    j = np.arange(K_WIN_DIL)[None, :]
    rel = j - BAND_HALF - a
    in_range = [np.ones_like(j, bool), j >= BAND_HALF, j < K_WIN_DIL - BAND_HALF]
    tiles = []
    for _, dil in DIL_PATTERNS:
        for ok in in_range:
            tiles.append(np.where((np.abs(rel) <= BAND_HALF) & ok, _t5_bucket(rel * dil), -1))
    return np.stack(tiles).astype(np.int32)


N_EDGE = 3


def _bias_kernel(rb_ref, bucket_ref, o_ref):
    bucket = bucket_ref[0]
    accs = [jnp.where(bucket < 0, NEG_INF, 0.0).astype(F32) for _ in range(DIL_HEADS)]
    for bk in range(REL_BUCKETS):
        hit = bucket == bk
        for hd in range(DIL_HEADS):
            accs[hd] = jnp.where(hit, rb_ref[bk, hd], accs[hd])
    for hd in range(DIL_HEADS):
        o_ref[0, hd] = accs[hd]


def _bias_tiles(rel_bias):
    buckets = jnp.asarray(_bucket_tiles())
    n = buckets.shape[0]
    return pl.pallas_call(
        _bias_kernel,
        grid=(n,),
        in_specs=[pl.BlockSpec(memory_space=pltpu.SMEM),
                  pl.BlockSpec((1, Q_TILE_DIL, K_WIN_DIL), lambda p: (p, 0, 0))],
        out_specs=pl.BlockSpec((1, DIL_HEADS, Q_TILE_DIL, K_WIN_DIL), lambda p: (p, 0, 0, 0)),
        out_shape=jax.ShapeDtypeStruct((n, DIL_HEADS, Q_TILE_DIL, K_WIN_DIL), F32),
        compiler_params=_params(("arbitrary",)),
        name="dil_bias",
    )(rel_bias.astype(F32), buckets)


DIL_TILES_PER_ITER = 16
DIL_COPY_ROWS = 256
DIL_MERGE_ROWS = 256


def _dil_kernel(q_ref, k_ref, v_ref, bias_ref, o_ref,
                k0_ref, k1_ref, vp_ref, op_ref, lp_ref, *, seq):
    lane = lax.broadcasted_iota(jnp.int32, (1, LANES), 1)
    low = lane < DIL_HEAD_DIM
    zeros = jnp.zeros((BAND_HALF, LANES), BF16)

    def rows(start, size, dil):
        return pl.ds(start, size) if dil == 1 else pl.ds(start, size, stride=dil)

    for p, (_, dil) in enumerate(DIL_PATTERNS):
        length = seq // dil
        ntile = length // Q_TILE_DIL
        span = length + 2 * BAND_HALF

        def deinterleave(r, carry, dil=dil, length=length, span=span):
            base = pl.multiple_of(r * span, 2 * BAND_HALF)
            for ref in (k0_ref, k1_ref, vp_ref):
                ref[pl.ds(base, BAND_HALF), :] = zeros
                ref[pl.ds(base + BAND_HALF + length, BAND_HALF), :] = zeros
            for cidx in range(length // DIL_COPY_ROWS):
                src = rows(r + dil * DIL_COPY_ROWS * cidx, DIL_COPY_ROWS, dil)
                dst = pl.ds(pl.multiple_of(base + BAND_HALF + DIL_COPY_ROWS * cidx, BAND_HALF),
                            DIL_COPY_ROWS)
                kk = k_ref[0, src, :]
                k0_ref[dst, :] = jnp.where(low, kk, 0.0).astype(BF16)
                k1_ref[dst, :] = jnp.where(low, 0.0, kk).astype(BF16)
                vp_ref[dst, :] = v_ref[0, src, :].astype(BF16)
            return carry

        lax.fori_loop(0, dil, deinterleave, 0)

        def tiles(it, carry, p=p, dil=dil, ntile=ntile, span=span):
            for u in range(DIL_TILES_PER_ITER):
                t = it * DIL_TILES_PER_ITER + u
                r = t // ntile
                i = t % ntile
                edge = jnp.where(i == 0, 1, jnp.where(i == ntile - 1, 2, 0))
                r0 = i * Q_TILE_DIL
                tok = rows(r + dil * r0, Q_TILE_DIL, dil)
                win = pl.ds(pl.multiple_of(r * span + r0, Q_TILE_DIL), K_WIN_DIL)
                q = q_ref[0, tok, :].astype(BF16)
                vw = vp_ref[win, :]
                pvs, maxs, dens = [], [], []
                for j, kref in enumerate((k0_ref, k1_ref)):
                    s = _dot_nt(q, kref[win, :]) + bias_ref[p * N_EDGE + edge, j]
                    m = jnp.max(s, axis=-1, keepdims=True)
                    e = jnp.exp(s - m)
                    dens.append(jnp.sum(e, axis=-1, keepdims=True))
                    pvs.append(_dot(e.astype(BF16), vw))
                    maxs.append(m)
                den = jnp.where(low, dens[0], dens[1])
                op_ref[p, tok, :] = jnp.where(low, pvs[0], pvs[1]) / den
                lp_ref[p, tok, :] = jnp.where(low, maxs[0], maxs[1]) + jnp.log(den)
            return carry

        lax.fori_loop(0, seq // Q_TILE_DIL // DIL_TILES_PER_ITER, tiles, 0)

    def merge(g, carry):
        sl = pl.ds(pl.multiple_of(g * DIL_MERGE_ROWS, DIL_MERGE_ROWS), DIL_MERGE_ROWS)
        ls = [lp_ref[p, sl, :] for p in range(len(DIL_PATTERNS))]
        m = functools.reduce(jnp.maximum, ls)
        es = [jnp.exp(l - m) for l in ls]
        num = functools.reduce(lambda a, b: a + b,
                               [e * op_ref[p, sl, :] for p, e in enumerate(es)])
        den = functools.reduce(lambda a, b: a + b, es)
        o_ref[0, sl, :] = (num / den).astype(o_ref.dtype)
        return carry

    lax.fori_loop(0, seq // DIL_MERGE_ROWS, merge, 0)


def _dilated(dq, dk, dv, bias):
    bsz, s, w = dq.shape
    pairs = DIL_HEADS // 2
    npat = len(DIL_PATTERNS)
    blk = pl.BlockSpec((1, s, LANES), lambda b, p: (b, 0, p))
    pad_rows = s + 2 * BAND_HALF * max(dl for _, dl in DIL_PATTERNS)
    assert (s // Q_TILE_DIL) % DIL_TILES_PER_ITER == 0
    return pl.pallas_call(
        functools.partial(_dil_kernel, seq=s),
        grid=(bsz, pairs),
        in_specs=[blk, blk, blk,
                  pl.BlockSpec((npat * N_EDGE, 2, Q_TILE_DIL, K_WIN_DIL), lambda b, p: (0, p, 0, 0))],
        out_specs=blk,
        out_shape=jax.ShapeDtypeStruct((bsz, s, w), BF16),
        scratch_shapes=[pltpu.VMEM((pad_rows, LANES), BF16)] * 3
                       + [pltpu.VMEM((npat, s, LANES), F32)] * 2,
        compiler_params=_params(("parallel", "arbitrary")),
        name="dilated",
    )(dq, dk, dv, bias)


def _mix_kernel(x_ref, mla_ref, dil_ref,
                g1_ref, sc2_ref, sh2_ref, wout_ref, lg_ref, lb_ref, wr_ref,
                x1_ref, h2_ref, aff_ref, *, alpha):
    mix = _dot(mla_ref[0], wout_ref[0:MLA_WIDTH, :]) + _dot(dil_ref[0], wout_ref[MLA_WIDTH:, :])
    y = alpha * x_ref[0] + g1_ref[0] * mix
    x1 = _layer_norm(y, lg_ref[...], lb_ref[...])
    x1_ref[0] = x1
    h2 = x1 * (1.0 + sc2_ref[0]) + sh2_ref[0]
    h2_ref[0] = h2.reshape(h2.shape[0], 1, h2.shape[1])
    logits = lax.dot_general(wr_ref[...], h2, (((1,), (1,)), ((), ())),
                             preferred_element_type=F32,
                             precision=lax.Precision.HIGHEST)
    mx = jnp.max(logits, axis=0, keepdims=True)
    ex = jnp.exp(logits - mx)
    aff_ref[0] = ex / jnp.sum(ex, axis=0, keepdims=True)


def _mix(x, mla, dil, mod3, w_out, ln_g, ln_b, w_router_t, alpha, tm=512):
    bsz, s, d = x.shape
    row = lambda w: pl.BlockSpec((1, tm, w), lambda b, i: (b, i, 0))
    modspec = lambda k: pl.BlockSpec((1, 1, d), lambda b, i: (6 * b + k, 0, 0))
    full = lambda a: pl.BlockSpec(a.shape, lambda b, i: (0,) * a.ndim)
    return pl.pallas_call(
        functools.partial(_mix_kernel, alpha=alpha),
        grid=(bsz, s // tm),
        in_specs=[row(d), row(MLA_WIDTH), row(DIL_WIDTH)]
                 + [modspec(2), modspec(4), modspec(3),
                    full(w_out), full(ln_g), full(ln_b), full(w_router_t)],
        out_specs=[row(d), pl.BlockSpec((1, tm, 1, d), lambda b, i: (b, i, 0, 0)),
                   pl.BlockSpec((1, N_EXPERTS, tm), lambda b, i: (b, 0, i))],
        out_shape=[jax.ShapeDtypeStruct((bsz, s, d), F32),
                   jax.ShapeDtypeStruct((bsz, s, 1, d), F32),
                   jax.ShapeDtypeStruct((bsz, N_EXPERTS, s), F32)],
        compiler_params=_params(("parallel", "arbitrary")),
        name="mix",
    )(x, mla, dil, mod3, mod3, mod3, w_out, ln_g, ln_b, w_router_t)


def _cumsum_lanes(x, tri):
    rows, n = x.shape
    carry = jnp.zeros((rows, 1), F32)
    parts = []
    for j in range(n // LANES):
        inc = _dot(x[:, j * LANES:(j + 1) * LANES].astype(BF16), tri) + carry
        parts.append(inc)
        carry = inc[:, LANES - 1:LANES]
    return jnp.concatenate(parts, axis=1)


TOPK_EXPERTS_PER_ITER = 2


def _topk_kernel(aff_ref, idx_ref, val_ref, key_ref, *, cap):
    a = aff_ref[0]
    n_e, s = a.shape
    t = jnp.zeros((n_e, 1), jnp.int32)
    for bit in range(30, -1, -1):
        cand = t | (1 << bit)
        cnt = jnp.sum((a >= pltpu.bitcast(cand, F32)).astype(jnp.int32), axis=1, keepdims=True)
        t = jnp.where(cnt >= cap, cand, t)
    thr = pltpu.bitcast(t, F32)
    gt = a > thr
    eq = a == thr
    n_gt = jnp.sum(gt.astype(jnp.int32), axis=1, keepdims=True)
    ri = lax.broadcasted_iota(jnp.int32, (LANES, LANES), 0)
    ci = lax.broadcasted_iota(jnp.int32, (LANES, LANES), 1)
    tri = jnp.where(ri <= ci, 1.0, 0.0).astype(BF16)
    eq_f = jnp.where(eq, 1.0, 0.0)
    rank_eq = _cumsum_lanes(eq_f, tri) - eq_f
    sel = gt | (eq & (rank_eq < (cap - n_gt).astype(F32)))
    sel_f = jnp.where(sel, 1.0, 0.0)
    pos = _cumsum_lanes(sel_f, tri) - sel_f
    key_ref[...] = jnp.where(sel, pos.astype(jnp.int32), -1)

    tok = lax.broadcasted_iota(jnp.int32, (1, s), 1)
    tok_hi = (tok >> 6).astype(F32)
    tok_lo = (tok & 63).astype(F32)
    slot = lax.broadcasted_iota(jnp.int32, (cap, 1), 0)
    rid = lax.broadcasted_iota(jnp.int32, (8, s), 0)

    def per_expert(e):
        key = key_ref[pl.ds(e, 1), :]
        onehot = jnp.where(key == slot, 1.0, 0.0).astype(BF16)
        ar = aff_ref[0, pl.ds(e, 1), :]
        a_hi = ar.astype(BF16).astype(F32)
        r1 = ar - a_hi
        a_mid = r1.astype(BF16).astype(F32)
        a_lo = r1 - a_mid
        lhs = jnp.where(rid == 0, tok_hi,
              jnp.where(rid == 1, tok_lo,
              jnp.where(rid == 2, a_hi,
              jnp.where(rid == 3, a_mid,
              jnp.where(rid == 4, a_lo, 0.0))))).astype(BF16)
        res = _dot_nt(lhs, onehot)
        idx_ref[0, pl.ds(e, 1), :] = (res[0:1] * 64.0 + res[1:2]).astype(jnp.int32)
        val_ref[0, pl.ds(e, 1), :] = res[2:3] + res[3:4] + res[4:5]

    def expert_group(g, carry):
        for u in range(TOPK_EXPERTS_PER_ITER):
            per_expert(g * TOPK_EXPERTS_PER_ITER + u)
        return carry

    lax.fori_loop(0, n_e // TOPK_EXPERTS_PER_ITER, expert_group, 0)


def _topk(aff, cap):
    bsz, n_e, s = aff.shape
    return pl.pallas_call(
        functools.partial(_topk_kernel, cap=cap),
        grid=(bsz,),
        in_specs=[pl.BlockSpec((1, n_e, s), lambda b: (b, 0, 0))],
        out_specs=[pl.BlockSpec((1, n_e, cap), lambda b: (b, 0, 0)),
                   pl.BlockSpec((1, n_e, cap), lambda b: (b, 0, 0))],
        out_shape=[jax.ShapeDtypeStruct((bsz, n_e, cap), jnp.int32),
                   jax.ShapeDtypeStruct((bsz, n_e, cap), F32)],
        scratch_shapes=[pltpu.VMEM((n_e, s), jnp.int32)],
        compiler_params=_params(("parallel",)),
        name="topk",
    )(aff)


GATHER_ROWS = 16


def _gather_kernel(idx_ref, h_ref, o_ref, rows_ref, *, cap):
    d = o_ref.shape[-1]

    def chunk(g, carry):
        base = pl.multiple_of(g * GATHER_ROWS, GATHER_ROWS)
        for u in range(GATHER_ROWS):
            rows_ref[base + u] = h_ref[0, idx_ref[0, 0, base + u]]
        cols = [rows_ref[pl.ds(base, GATHER_ROWS), 0, pl.ds(LANES * j, LANES)] for j in range(d // LANES)]
        o_ref[0, 0, pl.ds(base, GATHER_ROWS), :] = jnp.concatenate(cols, axis=1).astype(o_ref.dtype)
        return carry

    lax.fori_loop(0, cap // GATHER_ROWS, chunk, 0)


def _gather(idx3, h4, n_e, cap):
    bsz, s, _, d = h4.shape
    return pl.pallas_call(
        functools.partial(_gather_kernel, cap=cap),
        grid=(bsz, n_e),
        in_specs=[pl.BlockSpec((1, 1, cap), lambda b, e: (b * n_e + e, 0, 0),
                               memory_space=pltpu.SMEM),
                  pl.BlockSpec((1, s, 1, d), lambda b, e: (b, 0, 0, 0))],
        out_specs=pl.BlockSpec((1, 1, cap, d), lambda b, e: (b, e, 0, 0)),
        out_shape=jax.ShapeDtypeStruct((bsz, n_e, cap, d), BF16),
        scratch_shapes=[pltpu.VMEM((cap, 1, d), F32)],
        compiler_params=_params(("parallel", "arbitrary")),
        name="gather",
    )(idx3, h4)


def _ffn_kernel(x_ref, wg_hbm, wu_hbm, wd_hbm, y_ref,
                wg_bf, wu_bf, wd_bf, stage_g, stage_u, stage_d, sems):
    e = pl.program_id(0)
    b = pl.program_id(1)
    n_e = pl.num_programs(0)
    n_slices = pl.num_programs(1)
    rows_g = stage_g.shape[0]
    rows_d = stage_d.shape[0]
    slot = e % 2

    def slice_copies(expert, i):
        return (pltpu.make_async_copy(wg_hbm.at[expert, pl.ds(i * rows_g, rows_g), :], stage_g, sems.at[0]),
                pltpu.make_async_copy(wu_hbm.at[expert, pl.ds(i * rows_g, rows_g), :], stage_u, sems.at[1]),
                pltpu.make_async_copy(wd_hbm.at[expert, pl.ds(i * rows_d, rows_d), :], stage_d, sems.at[2]))

    def cast_slice(dst_slot, i):
        wg_bf[dst_slot, pl.ds(pl.multiple_of(i * rows_g, 16), rows_g), :] = stage_g[...].astype(BF16)
        wu_bf[dst_slot, pl.ds(pl.multiple_of(i * rows_g, 16), rows_g), :] = stage_u[...].astype(BF16)
        wd_bf[dst_slot, pl.ds(pl.multiple_of(i * rows_d, 16), rows_d), :] = stage_d[...].astype(BF16)

    def fetched_expert(expert):
        return jnp.minimum(expert + 1, n_e - 1)

    @pl.when(jnp.logical_and(e == 0, b == 0))
    def _():
        def load(i, carry):
            cps = slice_copies(0, i)
            for cp in cps:
                cp.start()
            for cp in cps:
                cp.wait()
            cast_slice(0, i)
            return carry
        lax.fori_loop(0, n_slices, load, 0)
        for cp in slice_copies(fetched_expert(0), 0):
            cp.start()

    x = x_ref[0, 0]
    gt = _dot(x, wg_bf[slot])
    up = _dot(x, wu_bf[slot])
    act = (gt * (1.0 / (1.0 + jnp.exp(-gt))) * up).astype(BF16)
    y = _dot(act, wd_bf[slot])
    y_ref[0, 0] = y.reshape(y.shape[0], 1, y.shape[1])

    for cp in slice_copies(fetched_expert(e), b):
        cp.wait()
    cast_slice(1 - slot, b)

    @pl.when(jnp.logical_or(e + 1 < n_e, b + 1 < n_slices))
    def _():
        wrap = b + 1 == n_slices
        e2 = jnp.where(wrap, e + 1, e)
        b2 = jnp.where(wrap, 0, b + 1)
        for cp in slice_copies(fetched_expert(e2), b2):
            cp.start()


def _ffn(xin, wg, wu, wd):
    bsz, n_e, cap, d = xin.shape
    f = wg.shape[-1]
    assert d % (16 * bsz) == 0 and f % (16 * bsz) == 0
    hbm = pl.BlockSpec(memory_space=pl.ANY)
    return pl.pallas_call(
        _ffn_kernel,
        grid=(n_e, bsz),
        in_specs=[pl.BlockSpec((1, 1, cap, d), lambda e, b: (b, e, 0, 0)), hbm, hbm, hbm],
        out_specs=pl.BlockSpec((1, 1, cap, 1, d), lambda e, b: (b, e, 0, 0, 0)),
        out_shape=jax.ShapeDtypeStruct((bsz, n_e, cap, 1, d), F32),
        scratch_shapes=[pltpu.VMEM((2, d, f), BF16), pltpu.VMEM((2, d, f), BF16), pltpu.VMEM((2, f, d), BF16),
                        pltpu.VMEM((d // bsz, f), F32), pltpu.VMEM((d // bsz, f), F32),
                        pltpu.VMEM((f // bsz, d), F32), pltpu.SemaphoreType.DMA((3,))],
        compiler_params=_params(("arbitrary", "arbitrary")),
        name="ffn",
    )(xin, wg, wu, wd)


SCATTER_UNROLL = 8


def _combine_kernel(idx_ref, val_ref, y_ref, o_ref, *, cap):
    @pl.when(pl.program_id(1) == 0)
    def _():
        o_ref[...] = jnp.zeros_like(o_ref)

    def chunk(g, carry):
        base = g * SCATTER_UNROLL
        rows = []
        for u in range(SCATTER_UNROLL):
            c = base + u
            t = idx_ref[0, 0, c]
            rows.append((t, o_ref[0, t] + val_ref[0, 0, c] * y_ref[0, 0, c]))
        for t, r in rows:
            o_ref[0, t] = r
        return carry

    lax.fori_loop(0, cap // SCATTER_UNROLL, chunk, 0)


def _combine(idx3, val3, y5, s):
    bsz, n_e, cap, _, d = y5.shape
    sm = lambda: pl.BlockSpec((1, 1, cap), lambda b, e: (b * n_e + e, 0, 0),
                              memory_space=pltpu.SMEM)
    return pl.pallas_call(
        functools.partial(_combine_kernel, cap=cap),
        grid=(bsz, n_e),
        in_specs=[sm(), sm(),
                  pl.BlockSpec((1, 1, cap, 1, d), lambda b, e: (b, e, 0, 0, 0))],
        out_specs=pl.BlockSpec((1, s, 1, d), lambda b, e: (b, 0, 0, 0)),
        out_shape=jax.ShapeDtypeStruct((bsz, s, 1, d), F32),
        compiler_params=_params(("parallel", "arbitrary")),
        name="combine",
    )(idx3, val3, y5)


FINAL_ROWS = 8


def _final_kernel(x1_ref, moe_ref, g2_ref, lg_ref, lb_ref, o_ref, moe2d_ref, *, alpha):
    tm, d = moe2d_ref.shape

    def retile(g, carry):
        r0 = pl.multiple_of(g * FINAL_ROWS, FINAL_ROWS)
        cols = [moe_ref[0, pl.ds(r0, FINAL_ROWS), 0, pl.ds(LANES * j, LANES)] for j in range(d // LANES)]
        moe2d_ref[pl.ds(r0, FINAL_ROWS), :] = jnp.concatenate(cols, axis=1)
        return carry

    lax.fori_loop(0, tm // FINAL_ROWS, retile, 0)
    y = alpha * x1_ref[0] + g2_ref[0] * moe2d_ref[...]
    o_ref[0] = _layer_norm(y, lg_ref[...], lb_ref[...])


def _final(x1, moe4, mod3, ln_g, ln_b, alpha, tm=512):
    bsz, s, d = x1.shape
    row = pl.BlockSpec((1, tm, d), lambda b, i: (b, i, 0))
    vec = pl.BlockSpec((1, d), lambda b, i: (0, 0))
    return pl.pallas_call(
        functools.partial(_final_kernel, alpha=alpha),
        grid=(bsz, s // tm),
        in_specs=[row, pl.BlockSpec((1, tm, 1, d), lambda b, i: (b, i, 0, 0)),
                  pl.BlockSpec((1, 1, d), lambda b, i: (6 * b + 5, 0, 0)), vec, vec],
        out_specs=row,
        out_shape=jax.ShapeDtypeStruct((bsz, s, d), F32),
        scratch_shapes=[pltpu.VMEM((tm, d), F32)],
        compiler_params=_params(("parallel", "arbitrary")),
        name="final",
    )(x1, moe4, mod3, ln_g, ln_b)


def _rope_tables(s):
    inv = ROPE_THETA ** (-jnp.arange(0, MLA_ROPE, 2, dtype=F32) / MLA_ROPE)
    ang = jnp.arange(s, dtype=F32)[:, None] * inv[None, :]
    cos, sin = jnp.cos(ang), jnp.sin(ang)
    scale = (MLA_NOPE + MLA_ROPE) ** -0.5 * math.log2(math.e)
    z32 = jnp.zeros((s, HEAD_SLAB - MLA_NOPE - MLA_ROPE), F32)
    z64 = jnp.zeros((s, MLA_NOPE), F32)
    cosq = jnp.concatenate([jnp.full((s, MLA_NOPE), scale, F32), cos * scale, cos * scale, z32], 1)
    sinq_scaled = jnp.concatenate([z64, sin * scale, sin * scale, z32], 1)
    cosk = jnp.concatenate([z64, cos, cos, z32], 1)
    sink = jnp.concatenate([z64, sin, sin, z32], 1)
    return cosq, sinq_scaled, cosk, sink


def _layout_weights(w_in, w_uq, w_ukv):
    d = w_in.shape[0]
    c0 = MLA_Q_LORA + MLA_KV_LORA
    kr = w_in[:, c0:c0 + MLA_ROPE]
    t1, t2 = kr[:, :HALF_ROPE], kr[:, HALF_ROPE:]
    z = lambda n: jnp.zeros((d, n), w_in.dtype)
    pad = HEAD_SLAB - MLA_NOPE - MLA_ROPE
    w_in_r = jnp.concatenate(
        [w_in[:, :c0], w_in[:, c0 + MLA_ROPE:],
         z(MLA_NOPE), t1, t2, z(pad),
         z(MLA_NOPE), -t2, t1, z(pad)], axis=1).astype(BF16)

    ql = w_uq.shape[0]
    wq = w_uq.reshape(ql, MLA_HEADS, MLA_NOPE + MLA_ROPE)
    qn, q1, q2 = wq[..., :MLA_NOPE], wq[..., MLA_NOPE:MLA_NOPE + HALF_ROPE], wq[..., MLA_NOPE + HALF_ROPE:]
    zq = lambda n: jnp.zeros((ql, MLA_HEADS, n), w_uq.dtype)
    wq_pre = jnp.concatenate([qn, q1, q2, zq(pad)], -1).reshape(ql, -1).astype(BF16)
    wq_sw = jnp.concatenate([zq(MLA_NOPE), -q2, q1, zq(pad)], -1).reshape(ql, -1).astype(BF16)

    kl = w_ukv.shape[0]
    wkv = w_ukv.reshape(kl, MLA_HEADS, MLA_NOPE + MLA_V)
    wk = jnp.concatenate([wkv[..., :MLA_NOPE],
                          jnp.zeros((kl, MLA_HEADS, HEAD_SLAB - MLA_NOPE), w_ukv.dtype)],
                         -1).reshape(kl, -1).astype(BF16)
    wv = jnp.concatenate([wkv[..., MLA_NOPE:],
                          jnp.zeros((kl, MLA_HEADS, HEAD_SLAB - MLA_V), w_ukv.dtype)],
                         -1).reshape(kl, -1).T.astype(BF16)
    return w_in_r, wq_pre, wq_sw, wk, wv


def kernel(x, c, w_ada, b_ada, w_in, q_norm_g, w_uq, kv_norm_g, w_ukv, rel_bias, w_out, ln1_g,
           ln1_b, w_router, w_gate, w_up, w_down, ln2_g, ln2_b):
    bsz, s, d = x.shape
    depth = w_ada.shape[0]
    alpha = (2 * depth) ** 0.25
    cap = max(1, EC_CAPACITY_FACTOR * s // N_EXPERTS)
    assert s % (max(2 * Q_TILE_DIL, DIL_COPY_ROWS) * max(dl for _, dl in DIL_PATTERNS)) == 0
    assert all(win // 2 // dl == BAND_HALF for win, dl in DIL_PATTERNS)

    cosq, sinq, cosk, sink = _rope_tables(s)
    bias = _bias_tiles(rel_bias)

    for l in range(depth):
        mod3 = _ada(c, w_ada[l], b_ada[l]).reshape(bsz * 6, 1, d)
        w_in_r, wq_pre, wq_sw, wk, wv = _layout_weights(w_in[l], w_uq[l], w_ukv[l])
        q, k, v, dq, dk, dv = _proj(
            x, mod3, w_in_r, q_norm_g[l].reshape(1, -1), wq_pre, wq_sw,
            kv_norm_g[l].reshape(1, -1), wk, wv, cosq, sinq, cosk, sink)
        mla = _mla(q, k, v)
        dil_out = _dilated(dq, dk, dv, bias)
        x1, h2, aff = _mix(x, mla, dil_out, mod3, w_out[l].astype(BF16),
                           ln1_g[l].reshape(1, d), ln1_b[l].reshape(1, d),
                           w_router[l].T, alpha)
        idx, vals = _topk(aff, cap)
        idx3 = idx.reshape(bsz * N_EXPERTS, 1, cap)
        val3 = vals.reshape(bsz * N_EXPERTS, 1, cap)
        xin = _gather(idx3, h2, N_EXPERTS, cap)
        y = _ffn(xin, w_gate[l], w_up[l], w_down[l])
        moe = _combine(idx3, val3, y, s)
        x = _final(x1, moe, mod3, ln2_g[l].reshape(1, d),
                   ln2_b[l].reshape(1, d), alpha)
    return x
```

```python
import functools
import math

import numpy as np
import jax
import jax.numpy as jnp
from jax import lax
from jax.experimental import pallas as pl
from jax.experimental.pallas import tpu as pltpu

MLA_HEADS = 8
MLA_NOPE = 64
MLA_ROPE = 32
MLA_V = 64
MLA_Q_LORA = 384
MLA_KV_LORA = 256
ROPE_THETA = 10000.0
DIL_HEADS = 8
DIL_HEAD_DIM = 64
DIL_PATTERNS = ((128, 1), (512, 4), (2048, 16))
REL_BUCKETS = 32
REL_MAX_EXACT = 8
REL_MAX_DIST = 1024
N_EXPERTS = 16
EC_CAPACITY_FACTOR = 2
NORM_EPS = 1e-6
NEG_INF = -1e30

LANES = 128
HEAD_SLAB = 128
VMEM_LIMIT = 48 * 1024 * 1024
F32 = jnp.float32
BF16 = jnp.bfloat16
HALF_ROPE = MLA_ROPE // 2
DIL_WIDTH = DIL_HEADS * DIL_HEAD_DIM
MLA_WIDTH = MLA_HEADS * MLA_V
BAND_HALF = 64
Q_TILE_DIL = 2 * BAND_HALF
K_WIN_DIL = 4 * BAND_HALF


def _params(sem, vmem=VMEM_LIMIT):
    return pltpu.CompilerParams(dimension_semantics=sem, vmem_limit_bytes=vmem)


def _dot(a, b):
    return jnp.dot(a, b, preferred_element_type=F32)


def _dot_nt(a, b):
    return lax.dot_general(a, b, (((1,), (1,)), ((), ())), preferred_element_type=F32)


def _layer_norm(y, g, b):
    mu = jnp.mean(y, axis=-1, keepdims=True)
    d = y - mu
    var = jnp.mean(d * d, axis=-1, keepdims=True)
    return d * lax.rsqrt(var + NORM_EPS) * g + b


def _ada_kernel(c_ref, w_ref, b_ref, o_ref):
    c = c_ref[...]
    s = c * (1.0 / (1.0 + jnp.exp(-c)))
    o_ref[...] = jnp.dot(s, w_ref[...], preferred_element_type=F32,
                         precision=lax.Precision.HIGHEST) + b_ref[...]


def _ada(c, w_ada, b_ada):
    bsz, d = c.shape
    n = w_ada.shape[1]
    tn = 1024
    return pl.pallas_call(
        _ada_kernel,
        grid=(n // tn,),
        in_specs=[pl.BlockSpec((bsz, d), lambda j: (0, 0)),
                  pl.BlockSpec((d, tn), lambda j: (0, j)),
                  pl.BlockSpec((1, tn), lambda j: (0, j))],
        out_specs=pl.BlockSpec((bsz, tn), lambda j: (0, j)),
        out_shape=jax.ShapeDtypeStruct((bsz, n), F32),
        compiler_params=_params(("arbitrary",)),
        name="ada",
    )(c, w_ada, b_ada.reshape(1, n))


_C_Q = 0
_C_KV = MLA_Q_LORA
_C_DQ = MLA_Q_LORA + MLA_KV_LORA
_C_DK = _C_DQ + DIL_WIDTH
_C_DV = _C_DK + DIL_WIDTH
_C_KR = _C_DV + DIL_WIDTH
_C_KRS = _C_KR + HEAD_SLAB
_C_END = _C_KRS + HEAD_SLAB


def _proj_kernel(x_ref, sc_ref, sh_ref, win_ref, gq_ref, wqp_ref, gkv_ref,
                 wk_ref, wv_ref, cosq_ref, sinq_ref, cosk_ref, sink_ref,
                 q_ref, k_ref, v_ref, dq_ref, dk_ref, dv_ref):
    h = (x_ref[0] * (1.0 + sc_ref[0]) + sh_ref[0]).astype(BF16)
    proj = _dot(h, win_ref[...])

    c_q = proj[:, _C_Q:_C_KV]
    cqn = (c_q * lax.rsqrt(jnp.mean(c_q * c_q, axis=-1, keepdims=True) + NORM_EPS)
           * gq_ref[...]).astype(BF16)
    q_pre = _dot(cqn, wqp_ref[...])
    cosq = cosq_ref[...]
    sinq = sinq_ref[...]
    lane = lax.broadcasted_iota(jnp.int32, (1, HEAD_SLAB), 1)
    first_half = lane < MLA_NOPE + HALF_ROPE

    c_kv = proj[:, _C_KV:_C_DQ]
    ckvn = (c_kv * lax.rsqrt(jnp.mean(c_kv * c_kv, axis=-1, keepdims=True) + NORM_EPS)
            * gkv_ref[...]).astype(BF16)
    k_nope = _dot(ckvn, wk_ref[...])
    vt = _dot_nt(wv_ref[...], ckvn)
    vrow = lax.broadcasted_iota(jnp.int32, vt.shape, 0)
    vt = jnp.where((vrow & MLA_V) != 0, 1.0, vt).astype(BF16)
    v_ref[0] = vt.reshape(MLA_HEADS, HEAD_SLAB, vt.shape[1])
    k_rope = proj[:, _C_KR:_C_KRS] * cosk_ref[...] + proj[:, _C_KRS:_C_END] * sink_ref[...]

    for hd in range(MLA_HEADS):
        sl = slice(hd * HEAD_SLAB, (hd + 1) * HEAD_SLAB)
        qp = q_pre[:, sl]
        partner = jnp.where(first_half, pltpu.roll(qp, HEAD_SLAB - HALF_ROPE, axis=1),
                            pltpu.roll(qp, HALF_ROPE, axis=1))
        q_ref[0, :, sl] = (qp * cosq + partner * sinq).astype(BF16)
        k_ref[0, :, sl] = (k_nope[:, sl] + k_rope).astype(BF16)

    dq_ref[0] = proj[:, _C_DQ:_C_DK] * (DIL_HEAD_DIM ** -0.5)
    dk_ref[0] = proj[:, _C_DK:_C_DV]
    dv_ref[0] = proj[:, _C_DV:_C_KR]


def _proj(x, mod3, w_in_r, gq, wq_pre, gkv, wk, wv, cosq, sinq, cosk, sink, tm=512):
    bsz, s, d = x.shape
    full = lambda a: pl.BlockSpec(a.shape, lambda b, i: (0,) * a.ndim)
    tab = pl.BlockSpec((tm, LANES), lambda b, i: (i, 0))
    hq = MLA_HEADS * HEAD_SLAB
    outs = [jax.ShapeDtypeStruct((bsz, s, hq), BF16),
            jax.ShapeDtypeStruct((bsz, s, hq), BF16),
            jax.ShapeDtypeStruct((bsz, MLA_HEADS, HEAD_SLAB, s), BF16),
            jax.ShapeDtypeStruct((bsz, s, DIL_WIDTH), F32),
            jax.ShapeDtypeStruct((bsz, s, DIL_WIDTH), F32),
            jax.ShapeDtypeStruct((bsz, s, DIL_WIDTH), F32)]
    ospec = lambda w: pl.BlockSpec((1, tm, w), lambda b, i: (b, i, 0))
    return pl.pallas_call(
        _proj_kernel,
        grid=(bsz, s // tm),
        in_specs=[pl.BlockSpec((1, tm, d), lambda b, i: (b, i, 0)),
                  pl.BlockSpec((1, 1, d), lambda b, i: (6 * b + 1, 0, 0)),
                  pl.BlockSpec((1, 1, d), lambda b, i: (6 * b + 0, 0, 0)),
                  full(w_in_r), full(gq), full(wq_pre), full(gkv),
                  full(wk), full(wv), tab, tab, tab, tab],
        out_specs=[ospec(hq), ospec(hq),
                   pl.BlockSpec((1, MLA_HEADS, HEAD_SLAB, tm), lambda b, i: (b, 0, 0, i)),
                   ospec(DIL_WIDTH),
                   ospec(DIL_WIDTH), ospec(DIL_WIDTH)],
        out_shape=outs,
        compiler_params=_params(("parallel", "arbitrary")),
        name="proj",
    )(x, mod3, mod3, w_in_r, gq, wq_pre, gkv, wk, wv, cosq, sinq, cosk, sink)


MLA_KEY_CHUNK = 512


MLA_Q_SUB = 256
MLA_VALUE_PARTS = 4


def _mla_kernel(q_ref, k_ref, vt_ref, o_ref, s_ref, p_ref):
    tq = q_ref.shape[1]
    seq = k_ref.shape[1]
    nchunk = seq // MLA_KEY_CHUNK
    units = [(a, j) for a in range(tq // MLA_Q_SUB) for j in range(2)]
    n = len(units)
    maxima = [None] * n
    heads_out = {}

    def chunk(c):
        return slice(c * MLA_KEY_CHUNK, (c + 1) * MLA_KEY_CHUNK)

    def scores(u, c):
        a, j = units[u]
        sl = slice(j * HEAD_SLAB, (j + 1) * HEAD_SLAB)
        s = _dot_nt(k_ref[0, chunk(c), sl], q_ref[0, a * MLA_Q_SUB:(a + 1) * MLA_Q_SUB, sl])
        s_ref[u % 2, chunk(c), :] = s
        mc = jnp.max(s, axis=0, keepdims=True)
        maxima[u] = mc if maxima[u] is None else jnp.maximum(maxima[u], mc)

    def probs(u, c):
        p_ref[u % 2, chunk(c), :] = jnp.exp2(s_ref[u % 2, chunk(c), :] - maxima[u]).astype(BF16)

    partial = {}

    def values_part(u, part):
        a, j = units[u]
        width = seq // MLA_VALUE_PARTS
        ks = slice(part * width, (part + 1) * width)
        d = _dot(vt_ref[0, j, :, ks], p_ref[u % 2, ks, :])
        partial[u] = d if part == 0 else partial[u] + d

    def values(u):
        a, j = units[u]
        acc = partial.pop(u)
        heads_out[(a, j)] = acc[0:MLA_V] / acc[MLA_V:MLA_V + 1]
        if j == 1:
            o_t = jnp.concatenate([heads_out[(a, 0)], heads_out[(a, 1)]], axis=0)
            o_ref[0, a * MLA_Q_SUB:(a + 1) * MLA_Q_SUB, :] = o_t.T.astype(o_ref.dtype)

    chunks_per_part = nchunk // MLA_VALUE_PARTS
    for stage in range(n + 2):
        for c in range(nchunk):
            if 0 <= stage - 2 < n and c % chunks_per_part == 0:
                values_part(stage - 2, c // chunks_per_part)
            if stage < n:
                scores(stage, c)
            if 0 <= stage - 1 < n:
                probs(stage - 1, c)
        if 0 <= stage - 2 < n:
            values(stage - 2)


def _mla(q, k, vt, tq=2048):
    bsz, s, _ = q.shape
    pairs = MLA_HEADS // 2
    return pl.pallas_call(
        _mla_kernel,
        grid=(bsz, pairs, s // tq),
        in_specs=[pl.BlockSpec((1, tq, 2 * HEAD_SLAB), lambda b, p, i: (b, i, p)),
                  pl.BlockSpec((1, s, 2 * HEAD_SLAB), lambda b, p, i: (b, 0, p)),
                  pl.BlockSpec((1, 2, 2 * MLA_V, s), lambda b, p, i: (b, p, 0, 0))],
        out_specs=pl.BlockSpec((1, tq, 2 * MLA_V), lambda b, p, i: (b, i, p)),
        out_shape=jax.ShapeDtypeStruct((bsz, s, MLA_WIDTH), BF16),
        scratch_shapes=[pltpu.VMEM((2, s, MLA_Q_SUB), F32),
                        pltpu.VMEM((2, s, MLA_Q_SUB), BF16)],
        compiler_params=_params(("parallel", "arbitrary", "arbitrary")),
        name="mla",
    )(q, k, vt)


def _t5_bucket(rel):
    half = REL_BUCKETS // 2
    ret = (rel > 0).astype(np.int32) * half
    n = np.abs(rel)
    large = REL_MAX_EXACT + (np.log(np.maximum(n, 1) / REL_MAX_EXACT)
                             / np.log(REL_MAX_DIST / REL_MAX_EXACT)
                             * (half - REL_MAX_EXACT)).astype(np.int32)
    large = np.minimum(large, half - 1)
    return ret + np.where(n < REL_MAX_EXACT, n, large).astype(np.int32)


def _bucket_tiles():
    a = np.arange(Q_TILE_DIL)[:, None]
    j = np.arange(K_WIN_DIL)[None, :]
    rel = j - BAND_HALF - a
    in_range = [np.ones_like(j, bool), j >= BAND_HALF, j < K_WIN_DIL - BAND_HALF]
    tiles = []
    for _, dil in DIL_PATTERNS:
        for ok in in_range:
            tiles.append(np.where((np.abs(rel) <= BAND_HALF) & ok, _t5_bucket(rel * dil), -1))
    return np.stack(tiles).astype(np.int32)


N_EDGE = 3


def _bias_kernel(rb_ref, bucket_ref, o_ref):
    bucket = bucket_ref[0]
    accs = [jnp.where(bucket < 0, NEG_INF, 0.0).astype(F32) for _ in range(DIL_HEADS)]
    for bk in range(REL_BUCKETS):
        hit = bucket == bk
        for hd in range(DIL_HEADS):
            accs[hd] = jnp.where(hit, rb_ref[bk, hd], accs[hd])
    for hd in range(DIL_HEADS):
        o_ref[0, hd] = accs[hd]


def _bias_tiles(rel_bias):
    buckets = jnp.asarray(_bucket_tiles())
    n = buckets.shape[0]
    return pl.pallas_call(
        _bias_kernel,
        grid=(n,),
        in_specs=[pl.BlockSpec(memory_space=pltpu.SMEM),
                  pl.BlockSpec((1, Q_TILE_DIL, K_WIN_DIL), lambda p: (p, 0, 0))],
        out_specs=pl.BlockSpec((1, DIL_HEADS, Q_TILE_DIL, K_WIN_DIL), lambda p: (p, 0, 0, 0)),
        out_shape=jax.ShapeDtypeStruct((n, DIL_HEADS, Q_TILE_DIL, K_WIN_DIL), F32),
        compiler_params=_params(("arbitrary",)),
        name="dil_bias",
    )(rel_bias.astype(F32), buckets)


DIL_TILES_PER_ITER = 16
DIL_COPY_ROWS = 256
DIL_MERGE_ROWS = 256


def _dil_kernel(q_ref, k_ref, v_ref, bias_ref, o_ref,
                k0_ref, k1_ref, vp_ref, op_ref, lp_ref, *, seq):
    lane = lax.broadcasted_iota(jnp.int32, (1, LANES), 1)
    low = lane < DIL_HEAD_DIM
    zeros = jnp.zeros((BAND_HALF, LANES), BF16)

    def rows(start, size, dil):
        return pl.ds(start, size) if dil == 1 else pl.ds(start, size, stride=dil)

    for p, (_, dil) in enumerate(DIL_PATTERNS):
        length = seq // dil
        ntile = length // Q_TILE_DIL
        span = length + 2 * BAND_HALF

        def deinterleave(r, carry, dil=dil, length=length, span=span):
            base = pl.multiple_of(r * span, 2 * BAND_HALF)
            for ref in (k0_ref, k1_ref, vp_ref):
                ref[pl.ds(base, BAND_HALF), :] = zeros
                ref[pl.ds(base + BAND_HALF + length, BAND_HALF), :] = zeros
            for cidx in range(length // DIL_COPY_ROWS):
                src = rows(r + dil * DIL_COPY_ROWS * cidx, DIL_COPY_ROWS, dil)
                dst = pl.ds(pl.multiple_of(base + BAND_HALF + DIL_COPY_ROWS * cidx, BAND_HALF),
                            DIL_COPY_ROWS)
                kk = k_ref[0, src, :]
                k0_ref[dst, :] = jnp.where(low, kk, 0.0).astype(BF16)
                k1_ref[dst, :] = jnp.where(low, 0.0, kk).astype(BF16)
                vp_ref[dst, :] = v_ref[0, src, :].astype(BF16)
            return carry

        lax.fori_loop(0, dil, deinterleave, 0)

        def tiles(it, carry, p=p, dil=dil, ntile=ntile, span=span):
            for u in range(DIL_TILES_PER_ITER):
                t = it * DIL_TILES_PER_ITER + u
                r = t // ntile
                i = t % ntile
                edge = jnp.where(i == 0, 1, jnp.where(i == ntile - 1, 2, 0))
                r0 = i * Q_TILE_DIL
                tok = rows(r + dil * r0, Q_TILE_DIL, dil)
                win = pl.ds(pl.multiple_of(r * span + r0, Q_TILE_DIL), K_WIN_DIL)
                q = q_ref[0, tok, :].astype(BF16)
                vw = vp_ref[win, :]
                pvs, maxs, dens = [], [], []
                for j, kref in enumerate((k0_ref, k1_ref)):
                    s = _dot_nt(q, kref[win, :]) + bias_ref[p * N_EDGE + edge, j]
                    m = jnp.max(s, axis=-1, keepdims=True)
                    e = jnp.exp(s - m)
                    dens.append(jnp.sum(e, axis=-1, keepdims=True))
                    pvs.append(_dot(e.astype(BF16), vw))
                    maxs.append(m)
                den = jnp.where(low, dens[0], dens[1])
                op_ref[p, tok, :] = jnp.where(low, pvs[0], pvs[1]) / den
                lp_ref[p, tok, :] = jnp.where(low, maxs[0], maxs[1]) + jnp.log(den)
            return carry

        lax.fori_loop(0, seq // Q_TILE_DIL // DIL_TILES_PER_ITER, tiles, 0)

    def merge(g, carry):
        sl = pl.ds(pl.multiple_of(g * DIL_MERGE_ROWS, DIL_MERGE_ROWS), DIL_MERGE_ROWS)
        ls = [lp_ref[p, sl, :] for p in range(len(DIL_PATTERNS))]
        m = functools.reduce(jnp.maximum, ls)
        es = [jnp.exp(l - m) for l in ls]
        num = functools.reduce(lambda a, b: a + b,
                               [e * op_ref[p, sl, :] for p, e in enumerate(es)])
        den = functools.reduce(lambda a, b: a + b, es)
        o_ref[0, sl, :] = (num / den).astype(o_ref.dtype)
        return carry

    lax.fori_loop(0, seq // DIL_MERGE_ROWS, merge, 0)


def _dilated(dq, dk, dv, bias):
    bsz, s, w = dq.shape
    pairs = DIL_HEADS // 2
    npat = len(DIL_PATTERNS)
    blk = pl.BlockSpec((1, s, LANES), lambda b, p: (b, 0, p))
    pad_rows = s + 2 * BAND_HALF * max(dl for _, dl in DIL_PATTERNS)
    assert (s // Q_TILE_DIL) % DIL_TILES_PER_ITER == 0
    return pl.pallas_call(
        functools.partial(_dil_kernel, seq=s),
        grid=(bsz, pairs),
        in_specs=[blk, blk, blk,
                  pl.BlockSpec((npat * N_EDGE, 2, Q_TILE_DIL, K_WIN_DIL), lambda b, p: (0, p, 0, 0))],
        out_specs=blk,
        out_shape=jax.ShapeDtypeStruct((bsz, s, w), BF16),
        scratch_shapes=[pltpu.VMEM((pad_rows, LANES), BF16)] * 3
                       + [pltpu.VMEM((npat, s, LANES), F32)] * 2,
        compiler_params=_params(("parallel", "arbitrary")),
        name="dilated",
    )(dq, dk, dv, bias)


def _mix_kernel(x_ref, mla_ref, dil_ref,
                g1_ref, sc2_ref, sh2_ref, wout_ref, lg_ref, lb_ref, wr_ref,
                x1_ref, h2_ref, aff_ref, *, alpha):
    mix = _dot(mla_ref[0], wout_ref[0:MLA_WIDTH, :]) + _dot(dil_ref[0], wout_ref[MLA_WIDTH:, :])
    y = alpha * x_ref[0] + g1_ref[0] * mix
    x1 = _layer_norm(y, lg_ref[...], lb_ref[...])
    x1_ref[0] = x1
    h2 = x1 * (1.0 + sc2_ref[0]) + sh2_ref[0]
    h2_ref[0] = h2.reshape(h2.shape[0], 1, h2.shape[1])
    logits = lax.dot_general(wr_ref[...], h2, (((1,), (1,)), ((), ())),
                             preferred_element_type=F32,
                             precision=lax.Precision.HIGHEST)
    mx = jnp.max(logits, axis=0, keepdims=True)
    ex = jnp.exp(logits - mx)
    aff_ref[0] = ex / jnp.sum(ex, axis=0, keepdims=True)


def _mix(x, mla, dil, mod3, w_out, ln_g, ln_b, w_router_t, alpha, tm=512):
    bsz, s, d = x.shape
    row = lambda w: pl.BlockSpec((1, tm, w), lambda b, i: (b, i, 0))
    modspec = lambda k: pl.BlockSpec((1, 1, d), lambda b, i: (6 * b + k, 0, 0))
    full = lambda a: pl.BlockSpec(a.shape, lambda b, i: (0,) * a.ndim)
    return pl.pallas_call(
        functools.partial(_mix_kernel, alpha=alpha),
        grid=(bsz, s // tm),
        in_specs=[row(d), row(MLA_WIDTH), row(DIL_WIDTH)]
                 + [modspec(2), modspec(4), modspec(3),
                    full(w_out), full(ln_g), full(ln_b), full(w_router_t)],
        out_specs=[row(d), pl.BlockSpec((1, tm, 1, d), lambda b, i: (b, i, 0, 0)),
                   pl.BlockSpec((1, N_EXPERTS, tm), lambda b, i: (b, 0, i))],
        out_shape=[jax.ShapeDtypeStruct((bsz, s, d), F32),
                   jax.ShapeDtypeStruct((bsz, s, 1, d), F32),
                   jax.ShapeDtypeStruct((bsz, N_EXPERTS, s), F32)],
        compiler_params=_params(("parallel", "arbitrary")),
        name="mix",
    )(x, mla, dil, mod3, mod3, mod3, w_out, ln_g, ln_b, w_router_t)


def _cumsum_lanes(x, tri):
    rows, n = x.shape
    carry = jnp.zeros((rows, 1), F32)
    parts = []
    for j in range(n // LANES):
        inc = _dot(x[:, j * LANES:(j + 1) * LANES].astype(BF16), tri) + carry
        parts.append(inc)
        carry = inc[:, LANES - 1:LANES]
    return jnp.concatenate(parts, axis=1)


TOPK_EXPERTS_PER_ITER = 2


def _topk_kernel(aff_ref, idx_ref, val_ref, key_ref, *, cap):
    a = aff_ref[0]
    n_e, s = a.shape
    t = jnp.zeros((n_e, 1), jnp.int32)
    for bit in range(30, -1, -1):
        cand = t | (1 << bit)
        cnt = jnp.sum((a >= pltpu.bitcast(cand, F32)).astype(jnp.int32), axis=1, keepdims=True)
        t = jnp.where(cnt >= cap, cand, t)
    thr = pltpu.bitcast(t, F32)
    gt = a > thr
    eq = a == thr
    n_gt = jnp.sum(gt.astype(jnp.int32), axis=1, keepdims=True)
    ri = lax.broadcasted_iota(jnp.int32, (LANES, LANES), 0)
    ci = lax.broadcasted_iota(jnp.int32, (LANES, LANES), 1)
    tri = jnp.where(ri <= ci, 1.0, 0.0).astype(BF16)
    eq_f = jnp.where(eq, 1.0, 0.0)
    rank_eq = _cumsum_lanes(eq_f, tri) - eq_f
    sel = gt | (eq & (rank_eq < (cap - n_gt).astype(F32)))
    sel_f = jnp.where(sel, 1.0, 0.0)
    pos = _cumsum_lanes(sel_f, tri) - sel_f
    key_ref[...] = jnp.where(sel, pos.astype(jnp.int32), -1)

    tok = lax.broadcasted_iota(jnp.int32, (1, s), 1)
    tok_hi = (tok >> 6).astype(F32)
    tok_lo = (tok & 63).astype(F32)
    slot = lax.broadcasted_iota(jnp.int32, (cap, 1), 0)
    rid = lax.broadcasted_iota(jnp.int32, (8, s), 0)

    def per_expert(e):
        key = key_ref[pl.ds(e, 1), :]
        onehot = jnp.where(key == slot, 1.0, 0.0).astype(BF16)
        ar = aff_ref[0, pl.ds(e, 1), :]
        a_hi = ar.astype(BF16).astype(F32)
        r1 = ar - a_hi
        a_mid = r1.astype(BF16).astype(F32)
        a_lo = r1 - a_mid
        lhs = jnp.where(rid == 0, tok_hi,
              jnp.where(rid == 1, tok_lo,
              jnp.where(rid == 2, a_hi,
              jnp.where(rid == 3, a_mid,
              jnp.where(rid == 4, a_lo, 0.0))))).astype(BF16)
        res = _dot_nt(lhs, onehot)
        idx_ref[0, pl.ds(e, 1), :] = (res[0:1] * 64.0 + res[1:2]).astype(jnp.int32)
        val_ref[0, pl.ds(e, 1), :] = res[2:3] + res[3:4] + res[4:5]

    def expert_group(g, carry):
        for u in range(TOPK_EXPERTS_PER_ITER):
            per_expert(g * TOPK_EXPERTS_PER_ITER + u)
        return carry

    lax.fori_loop(0, n_e // TOPK_EXPERTS_PER_ITER, expert_group, 0)


def _topk(aff, cap):
    bsz, n_e, s = aff.shape
    return pl.pallas_call(
        functools.partial(_topk_kernel, cap=cap),
        grid=(bsz,),
        in_specs=[pl.BlockSpec((1, n_e, s), lambda b: (b, 0, 0))],
        out_specs=[pl.BlockSpec((1, n_e, cap), lambda b: (b, 0, 0)),
                   pl.BlockSpec((1, n_e, cap), lambda b: (b, 0, 0))],
        out_shape=[jax.ShapeDtypeStruct((bsz, n_e, cap), jnp.int32),
                   jax.ShapeDtypeStruct((bsz, n_e, cap), F32)],
        scratch_shapes=[pltpu.VMEM((n_e, s), jnp.int32)],
        compiler_params=_params(("parallel",)),
        name="topk",
    )(aff)


GATHER_ROWS = 16


def _gather_kernel(idx_ref, h_ref, o_ref, rows_ref, *, cap):
    d = o_ref.shape[-1]

    def chunk(g, carry):
        base = pl.multiple_of(g * GATHER_ROWS, GATHER_ROWS)
        for u in range(GATHER_ROWS):
            rows_ref[base + u] = h_ref[0, idx_ref[0, 0, base + u]]
        cols = [rows_ref[pl.ds(base, GATHER_ROWS), 0, pl.ds(LANES * j, LANES)] for j in range(d // LANES)]
        o_ref[0, 0, pl.ds(base, GATHER_ROWS), :] = jnp.concatenate(cols, axis=1).astype(o_ref.dtype)
        return carry

    lax.fori_loop(0, cap // GATHER_ROWS, chunk, 0)


def _gather(idx3, h4, n_e, cap):
    bsz, s, _, d = h4.shape
    return pl.pallas_call(
        functools.partial(_gather_kernel, cap=cap),
        grid=(bsz, n_e),
        in_specs=[pl.BlockSpec((1, 1, cap), lambda b, e: (b * n_e + e, 0, 0),
                               memory_space=pltpu.SMEM),
                  pl.BlockSpec((1, s, 1, d), lambda b, e: (b, 0, 0, 0))],
        out_specs=pl.BlockSpec((1, 1, cap, d), lambda b, e: (b, e, 0, 0)),
        out_shape=jax.ShapeDtypeStruct((bsz, n_e, cap, d), BF16),
        scratch_shapes=[pltpu.VMEM((cap, 1, d), F32)],
        compiler_params=_params(("parallel", "arbitrary")),
        name="gather",
    )(idx3, h4)


def _ffn_kernel(x_ref, val_ref, wg_hbm, wu_hbm, wd_hbm, y_ref,
                wg_bf, wu_bf, wd_bf, stage_g, stage_u, stage_d, sems):
    e = pl.program_id(0)
    b = pl.program_id(1)
    n_e = pl.num_programs(0)
    n_slices = pl.num_programs(1)
    rows_g = stage_g.shape[0]
    rows_d = stage_d.shape[0]
    slot = e % 2

    def slice_copies(expert, i):
        return (pltpu.make_async_copy(wg_hbm.at[expert, pl.ds(i * rows_g, rows_g), :], stage_g, sems.at[0]),
                pltpu.make_async_copy(wu_hbm.at[expert, pl.ds(i * rows_g, rows_g), :], stage_u, sems.at[1]),
                pltpu.make_async_copy(wd_hbm.at[expert, pl.ds(i * rows_d, rows_d), :], stage_d, sems.at[2]))

    def cast_slice(dst_slot, i):
        wg_bf[dst_slot, pl.ds(pl.multiple_of(i * rows_g, 16), rows_g), :] = stage_g[...].astype(BF16)
        wu_bf[dst_slot, pl.ds(pl.multiple_of(i * rows_g, 16), rows_g), :] = stage_u[...].astype(BF16)
        wd_bf[dst_slot, pl.ds(pl.multiple_of(i * rows_d, 16), rows_d), :] = stage_d[...].astype(BF16)

    def fetched_expert(expert):
        return jnp.minimum(expert + 1, n_e - 1)

    @pl.when(jnp.logical_and(e == 0, b == 0))
    def _():
        def load(i, carry):
            cps = slice_copies(0, i)
            for cp in cps:
                cp.start()
            for cp in cps:
                cp.wait()
            cast_slice(0, i)
            return carry
        lax.fori_loop(0, n_slices, load, 0)
        for cp in slice_copies(fetched_expert(0), 0):
            cp.start()

    x = x_ref[0, 0]
    gt = _dot(x, wg_bf[slot])
    up = _dot(x, wu_bf[slot])
    act = (gt * (1.0 / (1.0 + jnp.exp(-gt))) * up).astype(BF16)
    y = _dot(act, wd_bf[slot]) * val_ref[0, 0]
    y_ref[0, 0] = y.reshape(y.shape[0], 1, y.shape[1])

    for cp in slice_copies(fetched_expert(e), b):
        cp.wait()
    cast_slice(1 - slot, b)

    @pl.when(jnp.logical_or(e + 1 < n_e, b + 1 < n_slices))
    def _():
        wrap = b + 1 == n_slices
        e2 = jnp.where(wrap, e + 1, e)
        b2 = jnp.where(wrap, 0, b + 1)
        for cp in slice_copies(fetched_expert(e2), b2):
            cp.start()


def _ffn(xin, vals4, wg, wu, wd):
    bsz, n_e, cap, d = xin.shape
    f = wg.shape[-1]
    assert d % (16 * bsz) == 0 and f % (16 * bsz) == 0
    hbm = pl.BlockSpec(memory_space=pl.ANY)
    return pl.pallas_call(
        _ffn_kernel,
        grid=(n_e, bsz),
        in_specs=[pl.BlockSpec((1, 1, cap, d), lambda e, b: (b, e, 0, 0)),
                  pl.BlockSpec((1, 1, cap, 1), lambda e, b: (b, e, 0, 0)), hbm, hbm, hbm],
        out_specs=pl.BlockSpec((1, 1, cap, 1, d), lambda e, b: (b, e, 0, 0, 0)),
        out_shape=jax.ShapeDtypeStruct((bsz, n_e, cap, 1, d), F32),
        scratch_shapes=[pltpu.VMEM((2, d, f), BF16), pltpu.VMEM((2, d, f), BF16), pltpu.VMEM((2, f, d), BF16),
                        pltpu.VMEM((d // bsz, f), F32), pltpu.VMEM((d // bsz, f), F32),
                        pltpu.VMEM((f // bsz, d), F32), pltpu.SemaphoreType.DMA((3,))],
        compiler_params=_params(("arbitrary", "arbitrary")),
        name="ffn",
    )(xin, vals4, wg, wu, wd)


SCATTER_UNROLL = 8


def _combine_kernel(idx_ref, y_ref, o_ref, *, cap):
    @pl.when(pl.program_id(1) == 0)
    def _():
        o_ref[...] = jnp.zeros_like(o_ref)

    def chunk(g, carry):
        base = g * SCATTER_UNROLL
        rows = []
        for u in range(SCATTER_UNROLL):
            c = base + u
            t = idx_ref[0, 0, c]
            rows.append((t, o_ref[0, t] + y_ref[0, 0, c]))
        for t, r in rows:
            o_ref[0, t] = r
        return carry

    lax.fori_loop(0, cap // SCATTER_UNROLL, chunk, 0)


def _combine(idx3, y5, s):
    bsz, n_e, cap, _, d = y5.shape
    return pl.pallas_call(
        functools.partial(_combine_kernel, cap=cap),
        grid=(bsz, n_e),
        in_specs=[pl.BlockSpec((1, 1, cap), lambda b, e: (b * n_e + e, 0, 0),
                               memory_space=pltpu.SMEM),
                  pl.BlockSpec((1, 1, cap, 1, d), lambda b, e: (b, e, 0, 0, 0))],
        out_specs=pl.BlockSpec((1, s, 1, d), lambda b, e: (b, 0, 0, 0)),
        out_shape=jax.ShapeDtypeStruct((bsz, s, 1, d), F32),
        compiler_params=_params(("parallel", "arbitrary")),
        name="combine",
    )(idx3, y5)


FINAL_ROWS = 8


def _final_kernel(x1_ref, moe_ref, g2_ref, lg_ref, lb_ref, o_ref, moe2d_ref, *, alpha):
    tm, d = moe2d_ref.shape

    def retile(g, carry):
        r0 = pl.multiple_of(g * FINAL_ROWS, FINAL_ROWS)
        cols = [moe_ref[0, pl.ds(r0, FINAL_ROWS), 0, pl.ds(LANES * j, LANES)] for j in range(d // LANES)]
        moe2d_ref[pl.ds(r0, FINAL_ROWS), :] = jnp.concatenate(cols, axis=1)
        return carry

    lax.fori_loop(0, tm // FINAL_ROWS, retile, 0)
    y = alpha * x1_ref[0] + g2_ref[0] * moe2d_ref[...]
    o_ref[0] = _layer_norm(y, lg_ref[...], lb_ref[...])


def _final(x1, moe4, mod3, ln_g, ln_b, alpha, tm=512):
    bsz, s, d = x1.shape
    row = pl.BlockSpec((1, tm, d), lambda b, i: (b, i, 0))
    vec = pl.BlockSpec((1, d), lambda b, i: (0, 0))
    return pl.pallas_call(
        functools.partial(_final_kernel, alpha=alpha),
        grid=(bsz, s // tm),
        in_specs=[row, pl.BlockSpec((1, tm, 1, d), lambda b, i: (b, i, 0, 0)),
                  pl.BlockSpec((1, 1, d), lambda b, i: (6 * b + 5, 0, 0)), vec, vec],
        out_specs=row,
        out_shape=jax.ShapeDtypeStruct((bsz, s, d), F32),
        scratch_shapes=[pltpu.VMEM((tm, d), F32)],
        compiler_params=_params(("parallel", "arbitrary")),
        name="final",
    )(x1, moe4, mod3, ln_g, ln_b)


def _rope_tables(s):
    inv = ROPE_THETA ** (-jnp.arange(0, MLA_ROPE, 2, dtype=F32) / MLA_ROPE)
    ang = jnp.arange(s, dtype=F32)[:, None] * inv[None, :]
    cos, sin = jnp.cos(ang), jnp.sin(ang)
    scale = (MLA_NOPE + MLA_ROPE) ** -0.5 * math.log2(math.e)
    z32 = jnp.zeros((s, HEAD_SLAB - MLA_NOPE - MLA_ROPE), F32)
    z64 = jnp.zeros((s, MLA_NOPE), F32)
    cosq = jnp.concatenate([jnp.full((s, MLA_NOPE), scale, F32), cos * scale, cos * scale, z32], 1)
    sinq_scaled = jnp.concatenate([z64, -sin * scale, sin * scale, z32], 1)
    cosk = jnp.concatenate([z64, cos, cos, z32], 1)
    sink = jnp.concatenate([z64, sin, sin, z32], 1)
    return cosq, sinq_scaled, cosk, sink


def _layout_weights(w_in, w_uq, w_ukv):
    d = w_in.shape[0]
    c0 = MLA_Q_LORA + MLA_KV_LORA
    kr = w_in[:, c0:c0 + MLA_ROPE]
    t1, t2 = kr[:, :HALF_ROPE], kr[:, HALF_ROPE:]
    z = lambda n: jnp.zeros((d, n), w_in.dtype)
    pad = HEAD_SLAB - MLA_NOPE - MLA_ROPE
    w_in_r = jnp.concatenate(
        [w_in[:, :c0], w_in[:, c0 + MLA_ROPE:],
         z(MLA_NOPE), t1, t2, z(pad),
         z(MLA_NOPE), -t2, t1, z(pad)], axis=1).astype(BF16)

    ql = w_uq.shape[0]
    wq = w_uq.reshape(ql, MLA_HEADS, MLA_NOPE + MLA_ROPE)
    qn, q1, q2 = wq[..., :MLA_NOPE], wq[..., MLA_NOPE:MLA_NOPE + HALF_ROPE], wq[..., MLA_NOPE + HALF_ROPE:]
    zq = lambda n: jnp.zeros((ql, MLA_HEADS, n), w_uq.dtype)
    wq_pre = jnp.concatenate([qn, q1, q2, zq(pad)], -1).reshape(ql, -1).astype(BF16)

    kl = w_ukv.shape[0]
    wkv = w_ukv.reshape(kl, MLA_HEADS, MLA_NOPE + MLA_V)
    wk = jnp.concatenate([wkv[..., :MLA_NOPE],
                          jnp.zeros((kl, MLA_HEADS, HEAD_SLAB - MLA_NOPE), w_ukv.dtype)],
                         -1).reshape(kl, -1).astype(BF16)
    wv = jnp.concatenate([wkv[..., MLA_NOPE:],
                          jnp.zeros((kl, MLA_HEADS, HEAD_SLAB - MLA_V), w_ukv.dtype)],
                         -1).reshape(kl, -1).T.astype(BF16)
    return w_in_r, wq_pre, wk, wv


def kernel(x, c, w_ada, b_ada, w_in, q_norm_g, w_uq, kv_norm_g, w_ukv, rel_bias, w_out, ln1_g,
           ln1_b, w_router, w_gate, w_up, w_down, ln2_g, ln2_b):
    bsz, s, d = x.shape
    depth = w_ada.shape[0]
    alpha = (2 * depth) ** 0.25
    cap = max(1, EC_CAPACITY_FACTOR * s // N_EXPERTS)
    assert s % (max(2 * Q_TILE_DIL, DIL_COPY_ROWS) * max(dl for _, dl in DIL_PATTERNS)) == 0
    assert all(win // 2 // dl == BAND_HALF for win, dl in DIL_PATTERNS)

    cosq, sinq, cosk, sink = _rope_tables(s)
    bias = _bias_tiles(rel_bias)

    for l in range(depth):
        mod3 = _ada(c, w_ada[l], b_ada[l]).reshape(bsz * 6, 1, d)
        w_in_r, wq_pre, wk, wv = _layout_weights(w_in[l], w_uq[l], w_ukv[l])
        q, k, v, dq, dk, dv = _proj(
            x, mod3, w_in_r, q_norm_g[l].reshape(1, -1), wq_pre,
            kv_norm_g[l].reshape(1, -1), wk, wv, cosq, sinq, cosk, sink)
        mla = _mla(q, k, v)
        dil_out = _dilated(dq, dk, dv, bias)
        x1, h2, aff = _mix(x, mla, dil_out, mod3, w_out[l].astype(BF16),
                           ln1_g[l].reshape(1, d), ln1_b[l].reshape(1, d),
                           w_router[l].T, alpha)
        idx, vals = _topk(aff, cap)
        idx3 = idx.reshape(bsz * N_EXPERTS, 1, cap)
        xin = _gather(idx3, h2, N_EXPERTS, cap)
        y = _ffn(xin, vals.reshape(bsz, N_EXPERTS, cap, 1), w_gate[l], w_up[l], w_down[l])
        moe = _combine(idx3, y, s)
        x = _final(x1, moe, mod3, ln2_g[l].reshape(1, d),
                   ln2_b[l].reshape(1, d), alpha)
    return x
```

```python
import functools
import math

import numpy as np
import jax
import jax.numpy as jnp
from jax import lax
from jax.experimental import pallas as pl
from jax.experimental.pallas import tpu as pltpu

MLA_HEADS = 8
MLA_NOPE = 64
MLA_ROPE = 32
MLA_V = 64
MLA_Q_LORA = 384
MLA_KV_LORA = 256
ROPE_THETA = 10000.0
DIL_HEADS = 8
DIL_HEAD_DIM = 64
DIL_PATTERNS = ((128, 1), (512, 4), (2048, 16))
REL_BUCKETS = 32
REL_MAX_EXACT = 8
REL_MAX_DIST = 1024
N_EXPERTS = 16
EC_CAPACITY_FACTOR = 2
NORM_EPS = 1e-6
NEG_INF = -1e30

LANES = 128
HEAD_SLAB = 128
VMEM_LIMIT = 48 * 1024 * 1024
F32 = jnp.float32
BF16 = jnp.bfloat16
HALF_ROPE = MLA_ROPE // 2
DIL_WIDTH = DIL_HEADS * DIL_HEAD_DIM
MLA_WIDTH = MLA_HEADS * MLA_V
BAND_HALF = 64
Q_TILE_DIL = 2 * BAND_HALF
K_WIN_DIL = 4 * BAND_HALF


def _params(sem, vmem=VMEM_LIMIT):
    return pltpu.CompilerParams(dimension_semantics=sem, vmem_limit_bytes=vmem)


def _dot(a, b):
    return jnp.dot(a, b, preferred_element_type=F32)


def _dot_nt(a, b):
    return lax.dot_general(a, b, (((1,), (1,)), ((), ())), preferred_element_type=F32)


def _layer_norm(y, g, b):
    mu = jnp.mean(y, axis=-1, keepdims=True)
    d = y - mu
    var = jnp.mean(d * d, axis=-1, keepdims=True)
    return d * lax.rsqrt(var + NORM_EPS) * g + b


def _ada_kernel(c_ref, w_ref, b_ref, o_ref):
    c = c_ref[...]
    s = c * (1.0 / (1.0 + jnp.exp(-c)))
    o_ref[...] = jnp.dot(s, w_ref[...], preferred_element_type=F32,
                         precision=lax.Precision.HIGHEST) + b_ref[...]


def _ada(c, w_ada, b_ada):
    bsz, d = c.shape
    n = w_ada.shape[1]
    tn = 1024
    return pl.pallas_call(
        _ada_kernel,
        grid=(n // tn,),
        in_specs=[pl.BlockSpec((bsz, d), lambda j: (0, 0)),
                  pl.BlockSpec((d, tn), lambda j: (0, j)),
                  pl.BlockSpec((1, tn), lambda j: (0, j))],
        out_specs=pl.BlockSpec((bsz, tn), lambda j: (0, j)),
        out_shape=jax.ShapeDtypeStruct((bsz, n), F32),
        compiler_params=_params(("arbitrary",)),
        name="ada",
    )(c, w_ada, b_ada.reshape(1, n))


_C_Q = 0
_C_KV = MLA_Q_LORA
_C_DQ = MLA_Q_LORA + MLA_KV_LORA
_C_DK = _C_DQ + DIL_WIDTH
_C_DV = _C_DK + DIL_WIDTH
_C_KR = _C_DV + DIL_WIDTH
_C_KRS = _C_KR + HEAD_SLAB
_C_END = _C_KRS + HEAD_SLAB


def _proj_kernel(x_ref, sc_ref, sh_ref, win_ref, gq_ref, wqp_ref, gkv_ref,
                 wk_ref, wv_ref, cosq_ref, sinq_ref, cosk_ref, sink_ref,
                 q_ref, k_ref, v_ref, dq_ref, dk_ref, dv_ref):
    h = (x_ref[0] * (1.0 + sc_ref[0]) + sh_ref[0]).astype(BF16)
    proj = _dot(h, win_ref[...])

    c_q = proj[:, _C_Q:_C_KV]
    cqn = (c_q * lax.rsqrt(jnp.mean(c_q * c_q, axis=-1, keepdims=True) + NORM_EPS)
           * gq_ref[...]).astype(BF16)
    q_pre = _dot(cqn, wqp_ref[...])
    cosq = cosq_ref[...]
    sinq = sinq_ref[...]
    lane = lax.broadcasted_iota(jnp.int32, (1, HEAD_SLAB), 1)
    first_half = lane < MLA_NOPE + HALF_ROPE

    c_kv = proj[:, _C_KV:_C_DQ]
    ckvn = (c_kv * lax.rsqrt(jnp.mean(c_kv * c_kv, axis=-1, keepdims=True) + NORM_EPS)
            * gkv_ref[...]).astype(BF16)
    k_nope = _dot(ckvn, wk_ref[...])
    vt = _dot_nt(wv_ref[...], ckvn)
    vrow = lax.broadcasted_iota(jnp.int32, vt.shape, 0)
    vt = jnp.where((vrow & MLA_V) != 0, 1.0, vt).astype(BF16)
    v_ref[0] = vt.reshape(MLA_HEADS, HEAD_SLAB, vt.shape[1])
    k_rope = proj[:, _C_KR:_C_KRS] * cosk_ref[...] + proj[:, _C_KRS:_C_END] * sink_ref[...]

    for hd in range(MLA_HEADS):
        sl = slice(hd * HEAD_SLAB, (hd + 1) * HEAD_SLAB)
        qp = q_pre[:, sl]
        partner = jnp.where(first_half, pltpu.roll(qp, HEAD_SLAB - HALF_ROPE, axis=1),
                            pltpu.roll(qp, HALF_ROPE, axis=1))
        q_ref[0, :, sl] = (qp * cosq + partner * sinq).astype(BF16)
        k_ref[0, :, sl] = (k_nope[:, sl] + k_rope).astype(BF16)

    dq_ref[0] = proj[:, _C_DQ:_C_DK] * (DIL_HEAD_DIM ** -0.5)
    dk_ref[0] = proj[:, _C_DK:_C_DV]
    dv_ref[0] = proj[:, _C_DV:_C_KR]


def _proj(x, mod3, w_in_r, gq, wq_pre, gkv, wk, wv, cosq, sinq, cosk, sink, tm=512):
    bsz, s, d = x.shape
    full = lambda a: pl.BlockSpec(a.shape, lambda b, i: (0,) * a.ndim)
    tab = pl.BlockSpec((tm, LANES), lambda b, i: (i, 0))
    hq = MLA_HEADS * HEAD_SLAB
    outs = [jax.ShapeDtypeStruct((bsz, s, hq), BF16),
            jax.ShapeDtypeStruct((bsz, s, hq), BF16),
            jax.ShapeDtypeStruct((bsz, MLA_HEADS, HEAD_SLAB, s), BF16),
            jax.ShapeDtypeStruct((bsz, s, DIL_WIDTH), F32),
            jax.ShapeDtypeStruct((bsz, s, DIL_WIDTH), F32),
            jax.ShapeDtypeStruct((bsz, s, DIL_WIDTH), F32)]
    ospec = lambda w: pl.BlockSpec((1, tm, w), lambda b, i: (b, i, 0))
    return pl.pallas_call(
        _proj_kernel,
        grid=(bsz, s // tm),
        in_specs=[pl.BlockSpec((1, tm, d), lambda b, i: (b, i, 0)),
                  pl.BlockSpec((1, 1, d), lambda b, i: (6 * b + 1, 0, 0)),
                  pl.BlockSpec((1, 1, d), lambda b, i: (6 * b + 0, 0, 0)),
                  full(w_in_r), full(gq), full(wq_pre), full(gkv),
                  full(wk), full(wv), tab, tab, tab, tab],
        out_specs=[ospec(hq), ospec(hq),
                   pl.BlockSpec((1, MLA_HEADS, HEAD_SLAB, tm), lambda b, i: (b, 0, 0, i)),
                   ospec(DIL_WIDTH),
                   ospec(DIL_WIDTH), ospec(DIL_WIDTH)],
        out_shape=outs,
        compiler_params=_params(("parallel", "arbitrary")),
        name="proj",
    )(x, mod3, mod3, w_in_r, gq, wq_pre, gkv, wk, wv, cosq, sinq, cosk, sink)


MLA_KEY_CHUNK = 512


MLA_Q_SUB = 256
MLA_VALUE_PARTS = 4


def _mla_kernel(q_ref, k_ref, vt_ref, o_ref, s_ref, p_ref):
    tq = q_ref.shape[1]
    seq = k_ref.shape[1]
    nchunk = seq // MLA_KEY_CHUNK
    units = [(a, j) for a in range(tq // MLA_Q_SUB) for j in range(2)]
    n = len(units)
    maxima = [None] * n
    heads_out = {}

    def chunk(c):
        return slice(c * MLA_KEY_CHUNK, (c + 1) * MLA_KEY_CHUNK)

    def scores(u, c):
        a, j = units[u]
        sl = slice(j * HEAD_SLAB, (j + 1) * HEAD_SLAB)
        s = _dot_nt(k_ref[0, chunk(c), sl], q_ref[0, a * MLA_Q_SUB:(a + 1) * MLA_Q_SUB, sl])
        s_ref[u % 2, chunk(c), :] = s
        mc = jnp.max(s, axis=0, keepdims=True)
        maxima[u] = mc if maxima[u] is None else jnp.maximum(maxima[u], mc)

    def probs(u, c):
        p_ref[u % 2, chunk(c), :] = jnp.exp2(s_ref[u % 2, chunk(c), :] - maxima[u]).astype(BF16)

    partial = {}

    def values_part(u, part):
        a, j = units[u]
        width = seq // MLA_VALUE_PARTS
        ks = slice(part * width, (part + 1) * width)
        d = _dot(vt_ref[0, j, :, ks], p_ref[u % 2, ks, :])
        partial[u] = d if part == 0 else partial[u] + d

    def values(u):
        a, j = units[u]
        acc = partial.pop(u)
        heads_out[(a, j)] = acc[0:MLA_V] / acc[MLA_V:MLA_V + 1]
        if j == 1:
            o_t = jnp.concatenate([heads_out[(a, 0)], heads_out[(a, 1)]], axis=0)
            o_ref[0, a * MLA_Q_SUB:(a + 1) * MLA_Q_SUB, :] = o_t.T.astype(o_ref.dtype)

    chunks_per_part = nchunk // MLA_VALUE_PARTS
    for stage in range(n + 2):
        for c in range(nchunk):
            if 0 <= stage - 2 < n and c % chunks_per_part == 0:
                values_part(stage - 2, c // chunks_per_part)
            if stage < n:
                scores(stage, c)
            if 0 <= stage - 1 < n:
                probs(stage - 1, c)
        if 0 <= stage - 2 < n:
            values(stage - 2)


def _mla(q, k, vt, tq=2048):
    bsz, s, _ = q.shape
    pairs = MLA_HEADS // 2
    return pl.pallas_call(
        _mla_kernel,
        grid=(bsz, pairs, s // tq),
        in_specs=[pl.BlockSpec((1, tq, 2 * HEAD_SLAB), lambda b, p, i: (b, i, p)),
                  pl.BlockSpec((1, s, 2 * HEAD_SLAB), lambda b, p, i: (b, 0, p)),
                  pl.BlockSpec((1, 2, 2 * MLA_V, s), lambda b, p, i: (b, p, 0, 0))],
        out_specs=pl.BlockSpec((1, tq, 2 * MLA_V), lambda b, p, i: (b, i, p)),
        out_shape=jax.ShapeDtypeStruct((bsz, s, MLA_WIDTH), BF16),
        scratch_shapes=[pltpu.VMEM((2, s, MLA_Q_SUB), F32),
                        pltpu.VMEM((2, s, MLA_Q_SUB), BF16)],
        compiler_params=_params(("parallel", "arbitrary", "arbitrary")),
        name="mla",
    )(q, k, vt)


def _t5_bucket(rel):
    half = REL_BUCKETS // 2
    ret = (rel > 0).astype(np.int32) * half
    n = np.abs(rel)
    large = REL_MAX_EXACT + (np.log(np.maximum(n, 1) / REL_MAX_EXACT)
                             / np.log(REL_MAX_DIST / REL_MAX_EXACT)
                             * (half - REL_MAX_EXACT)).astype(np.int32)
    large = np.minimum(large, half - 1)
    return ret + np.where(n < REL_MAX_EXACT, n, large).astype(np.int32)


def _bucket_tiles():
    a = np.arange(Q_TILE_DIL)[:, None]
    j = np.arange(K_WIN_DIL)[None, :]
    rel = j - BAND_HALF - a
    in_range = [np.ones_like(j, bool), j >= BAND_HALF, j < K_WIN_DIL - BAND_HALF]
    tiles = []
    for _, dil in DIL_PATTERNS:
        for ok in in_range:
            tiles.append(np.where((np.abs(rel) <= BAND_HALF) & ok, _t5_bucket(rel * dil), -1))
    return np.stack(tiles).astype(np.int32)


N_EDGE = 3


def _bias_kernel(rb_ref, bucket_ref, o_ref):
    bucket = bucket_ref[0]
    accs = [jnp.where(bucket < 0, NEG_INF, 0.0).astype(F32) for _ in range(DIL_HEADS)]
    for bk in range(REL_BUCKETS):
        hit = bucket == bk
        for hd in range(DIL_HEADS):
            accs[hd] = jnp.where(hit, rb_ref[bk, hd], accs[hd])
    for hd in range(DIL_HEADS):
        o_ref[0, hd] = accs[hd]


def _bias_tiles(rel_bias):
    buckets = jnp.asarray(_bucket_tiles())
    n = buckets.shape[0]
    return pl.pallas_call(
        _bias_kernel,
        grid=(n,),
        in_specs=[pl.BlockSpec(memory_space=pltpu.SMEM),
                  pl.BlockSpec((1, Q_TILE_DIL, K_WIN_DIL), lambda p: (p, 0, 0))],
        out_specs=pl.BlockSpec((1, DIL_HEADS, Q_TILE_DIL, K_WIN_DIL), lambda p: (p, 0, 0, 0)),
        out_shape=jax.ShapeDtypeStruct((n, DIL_HEADS, Q_TILE_DIL, K_WIN_DIL), F32),
        compiler_params=_params(("arbitrary",)),
        name="dil_bias",
    )(rel_bias.astype(F32), buckets)


DIL_TILES_PER_ITER = 16
DIL_COPY_ROWS = 256
DIL_MERGE_ROWS = 256


def _dil_kernel(q_ref, k_ref, v_ref, bias_ref, o_ref,
                k0_ref, k1_ref, vp_ref, op_ref, lp_ref, *, seq):
    lane = lax.broadcasted_iota(jnp.int32, (1, LANES), 1)
    low = lane < DIL_HEAD_DIM
    zeros = jnp.zeros((BAND_HALF, LANES), BF16)

    def rows(start, size, dil):
        return pl.ds(start, size) if dil == 1 else pl.ds(start, size, stride=dil)

    for p, (_, dil) in enumerate(DIL_PATTERNS):
        length = seq // dil
        ntile = length // Q_TILE_DIL
        span = length + 2 * BAND_HALF

        def deinterleave(r, carry, dil=dil, length=length, span=span):
            base = pl.multiple_of(r * span, 2 * BAND_HALF)
            for ref in (k0_ref, k1_ref, vp_ref):
                ref[pl.ds(base, BAND_HALF), :] = zeros
                ref[pl.ds(base + BAND_HALF + length, BAND_HALF), :] = zeros
            for cidx in range(length // DIL_COPY_ROWS):
                src = rows(r + dil * DIL_COPY_ROWS * cidx, DIL_COPY_ROWS, dil)
                dst = pl.ds(pl.multiple_of(base + BAND_HALF + DIL_COPY_ROWS * cidx, BAND_HALF),
                            DIL_COPY_ROWS)
                kk = k_ref[0, src, :]
                k0_ref[dst, :] = jnp.where(low, kk, 0.0).astype(BF16)
                k1_ref[dst, :] = jnp.where(low, 0.0, kk).astype(BF16)
                vp_ref[dst, :] = v_ref[0, src, :].astype(BF16)
            return carry

        lax.fori_loop(0, dil, deinterleave, 0)

        def tiles(it, carry, p=p, dil=dil, ntile=ntile, span=span):
            for u in range(DIL_TILES_PER_ITER):
                t = it * DIL_TILES_PER_ITER + u
                r = t // ntile
                i = t % ntile
                edge = jnp.where(i == 0, 1, jnp.where(i == ntile - 1, 2, 0))
                r0 = i * Q_TILE_DIL
                tok = rows(r + dil * r0, Q_TILE_DIL, dil)
                win = pl.ds(pl.multiple_of(r * span + r0, Q_TILE_DIL), K_WIN_DIL)
                q = q_ref[0, tok, :].astype(BF16)
                vw = vp_ref[win, :]
                pvs, maxs, dens = [], [], []
                for j, kref in enumerate((k0_ref, k1_ref)):
                    s = _dot_nt(q, kref[win, :]) + bias_ref[p * N_EDGE + edge, j]
                    m = jnp.max(s, axis=-1, keepdims=True)
                    e = jnp.exp(s - m)
                    dens.append(jnp.sum(e, axis=-1, keepdims=True))
                    pvs.append(_dot(e.astype(BF16), vw))
                    maxs.append(m)
                den = jnp.where(low, dens[0], dens[1])
                op_ref[p, tok, :] = jnp.where(low, pvs[0], pvs[1]) / den
                lp_ref[p, tok, :] = jnp.where(low, maxs[0], maxs[1]) + jnp.log(den)
            return carry

        lax.fori_loop(0, seq // Q_TILE_DIL // DIL_TILES_PER_ITER, tiles, 0)

    def merge(g, carry):
        sl = pl.ds(pl.multiple_of(g * DIL_MERGE_ROWS, DIL_MERGE_ROWS), DIL_MERGE_ROWS)
        ls = [lp_ref[p, sl, :] for p in range(len(DIL_PATTERNS))]
        m = functools.reduce(jnp.maximum, ls)
        es = [jnp.exp(l - m) for l in ls]
        num = functools.reduce(lambda a, b: a + b,
                               [e * op_ref[p, sl, :] for p, e in enumerate(es)])
        den = functools.reduce(lambda a, b: a + b, es)
        o_ref[0, sl, :] = (num / den).astype(o_ref.dtype)
        return carry

    lax.fori_loop(0, seq // DIL_MERGE_ROWS, merge, 0)


def _dilated(dq, dk, dv, bias):
    bsz, s, w = dq.shape
    pairs = DIL_HEADS // 2
    npat = len(DIL_PATTERNS)
    blk = pl.BlockSpec((1, s, LANES), lambda b, p: (b, 0, p))
    pad_rows = s + 2 * BAND_HALF * max(dl for _, dl in DIL_PATTERNS)
    assert (s // Q_TILE_DIL) % DIL_TILES_PER_ITER == 0
    return pl.pallas_call(
        functools.partial(_dil_kernel, seq=s),
        grid=(bsz, pairs),
        in_specs=[blk, blk, blk,
                  pl.BlockSpec((npat * N_EDGE, 2, Q_TILE_DIL, K_WIN_DIL), lambda b, p: (0, p, 0, 0))],
        out_specs=blk,
        out_shape=jax.ShapeDtypeStruct((bsz, s, w), BF16),
        scratch_shapes=[pltpu.VMEM((pad_rows, LANES), BF16)] * 3
                       + [pltpu.VMEM((npat, s, LANES), F32)] * 2,
        compiler_params=_params(("parallel", "arbitrary")),
        name="dilated",
    )(dq, dk, dv, bias)


def _mix_kernel(x_ref, mla_ref, dil_ref,
                g1_ref, sc2_ref, sh2_ref, wout_ref, lg_ref, lb_ref, wr_ref,
                x1_ref, h2_ref, aff_ref, *, alpha):
    mix = _dot(mla_ref[0], wout_ref[0:MLA_WIDTH, :]) + _dot(dil_ref[0], wout_ref[MLA_WIDTH:, :])
    y = alpha * x_ref[0] + g1_ref[0] * mix
    x1 = _layer_norm(y, lg_ref[...], lb_ref[...])
    x1_ref[0] = x1
    h2 = x1 * (1.0 + sc2_ref[0]) + sh2_ref[0]
    h2_ref[0] = h2.reshape(h2.shape[0], 1, h2.shape[1])
    logits = lax.dot_general(wr_ref[...], h2, (((1,), (1,)), ((), ())),
                             preferred_element_type=F32,
                             precision=lax.Precision.HIGHEST)
    mx = jnp.max(logits, axis=0, keepdims=True)
    ex = jnp.exp(logits - mx)
    aff_ref[0] = ex / jnp.sum(ex, axis=0, keepdims=True)


def _mix(x, mla, dil, mod3, w_out, ln_g, ln_b, w_router_t, alpha, tm=512):
    bsz, s, d = x.shape
    row = lambda w: pl.BlockSpec((1, tm, w), lambda b, i: (b, i, 0))
    modspec = lambda k: pl.BlockSpec((1, 1, d), lambda b, i: (6 * b + k, 0, 0))
    full = lambda a: pl.BlockSpec(a.shape, lambda b, i: (0,) * a.ndim)
    return pl.pallas_call(
        functools.partial(_mix_kernel, alpha=alpha),
        grid=(bsz, s // tm),
        in_specs=[row(d), row(MLA_WIDTH), row(DIL_WIDTH)]
                 + [modspec(2), modspec(4), modspec(3),
                    full(w_out), full(ln_g), full(ln_b), full(w_router_t)],
        out_specs=[row(d), pl.BlockSpec((1, tm, 1, d), lambda b, i: (b, i, 0, 0)),
                   pl.BlockSpec((1, N_EXPERTS, tm), lambda b, i: (b, 0, i))],
        out_shape=[jax.ShapeDtypeStruct((bsz, s, d), F32),
                   jax.ShapeDtypeStruct((bsz, s, 1, d), F32),
                   jax.ShapeDtypeStruct((bsz, N_EXPERTS, s), F32)],
        compiler_params=_params(("parallel", "arbitrary")),
        name="mix",
    )(x, mla, dil, mod3, mod3, mod3, w_out, ln_g, ln_b, w_router_t)


def _cumsum_lanes(x, tri):
    rows, n = x.shape
    carry = jnp.zeros((rows, 1), F32)
    parts = []
    for j in range(n // LANES):
        inc = _dot(x[:, j * LANES:(j + 1) * LANES].astype(BF16), tri) + carry
        parts.append(inc)
        carry = inc[:, LANES - 1:LANES]
    return jnp.concatenate(parts, axis=1)


TOPK_EXPERTS_PER_ITER = 2


def _topk_kernel(aff_ref, idx_ref, val_ref, key_ref, *, cap):
    a = aff_ref[0]
    n_e, s = a.shape
    t = jnp.zeros((n_e, 1), jnp.int32)
    for bit in range(30, -1, -1):
        cand = t | (1 << bit)
        cnt = jnp.sum((a >= pltpu.bitcast(cand, F32)).astype(jnp.int32), axis=1, keepdims=True)
        t = jnp.where(cnt >= cap, cand, t)
    thr = pltpu.bitcast(t, F32)
    gt = a > thr
    eq = a == thr
    n_gt = jnp.sum(gt.astype(jnp.int32), axis=1, keepdims=True)
    ri = lax.broadcasted_iota(jnp.int32, (LANES, LANES), 0)
    ci = lax.broadcasted_iota(jnp.int32, (LANES, LANES), 1)
    tri = jnp.where(ri <= ci, 1.0, 0.0).astype(BF16)
    eq_f = jnp.where(eq, 1.0, 0.0)
    rank_eq = _cumsum_lanes(eq_f, tri) - eq_f
    sel = gt | (eq & (rank_eq < (cap - n_gt).astype(F32)))
    sel_f = jnp.where(sel, 1.0, 0.0)
    pos = _cumsum_lanes(sel_f, tri) - sel_f
    key_ref[...] = jnp.where(sel, pos.astype(jnp.int32), -1)

    tok = lax.broadcasted_iota(jnp.int32, (1, s), 1)
    tok_hi = (tok >> 6).astype(F32)
    tok_lo = (tok & 63).astype(F32)
    slot = lax.broadcasted_iota(jnp.int32, (cap, 1), 0)
    rid = lax.broadcasted_iota(jnp.int32, (8, s), 0)

    def per_expert(e):
        key = key_ref[pl.ds(e, 1), :]
        onehot = jnp.where(key == slot, 1.0, 0.0).astype(BF16)
        ar = aff_ref[0, pl.ds(e, 1), :]
        a_hi = ar.astype(BF16).astype(F32)
        r1 = ar - a_hi
        a_mid = r1.astype(BF16).astype(F32)
        a_lo = r1 - a_mid
        lhs = jnp.where(rid == 0, tok_hi,
              jnp.where(rid == 1, tok_lo,
              jnp.where(rid == 2, a_hi,
              jnp.where(rid == 3, a_mid,
              jnp.where(rid == 4, a_lo, 0.0))))).astype(BF16)
        res = _dot_nt(lhs, onehot)
        idx_ref[0, pl.ds(e, 1), :] = (res[0:1] * 64.0 + res[1:2]).astype(jnp.int32)
        val_ref[0, pl.ds(e, 1), :] = res[2:3] + res[3:4] + res[4:5]

    def expert_group(g, carry):
        for u in range(TOPK_EXPERTS_PER_ITER):
            per_expert(g * TOPK_EXPERTS_PER_ITER + u)
        return carry

    lax.fori_loop(0, n_e // TOPK_EXPERTS_PER_ITER, expert_group, 0)


def _topk(aff, cap):
    bsz, n_e, s = aff.shape
    return pl.pallas_call(
        functools.partial(_topk_kernel, cap=cap),
        grid=(bsz,),
        in_specs=[pl.BlockSpec((1, n_e, s), lambda b: (b, 0, 0))],
        out_specs=[pl.BlockSpec((1, n_e, cap), lambda b: (b, 0, 0)),
                   pl.BlockSpec((1, n_e, cap), lambda b: (b, 0, 0))],
        out_shape=[jax.ShapeDtypeStruct((bsz, n_e, cap), jnp.int32),
                   jax.ShapeDtypeStruct((bsz, n_e, cap), F32)],
        scratch_shapes=[pltpu.VMEM((n_e, s), jnp.int32)],
        compiler_params=_params(("parallel",)),
        name="topk",
    )(aff)


GATHER_ROWS = 16


def _gather_kernel(idx_ref, h_ref, o_ref, rows_ref, *, cap):
    d = o_ref.shape[-1]
    pieces = d // LANES

    def chunk(g, carry):
        base = pl.multiple_of(g * GATHER_ROWS, GATHER_ROWS)
        for u in range(GATHER_ROWS):
            row = h_ref[0, idx_ref[0, 0, base + u]]
            rows_ref[pl.ds(pl.multiple_of((base + u) * pieces, pieces), pieces), :] = row.reshape(pieces, LANES)
        cols = [rows_ref[pl.ds(base * pieces + j, GATHER_ROWS, stride=pieces), :] for j in range(pieces)]
        o_ref[0, 0, pl.ds(base, GATHER_ROWS), :] = jnp.concatenate(cols, axis=1).astype(o_ref.dtype)
        return carry

    lax.fori_loop(0, cap // GATHER_ROWS, chunk, 0)


def _gather(idx3, h4, n_e, cap):
    bsz, s, _, d = h4.shape
    return pl.pallas_call(
        functools.partial(_gather_kernel, cap=cap),
        grid=(bsz, n_e),
        in_specs=[pl.BlockSpec((1, 1, cap), lambda b, e: (b * n_e + e, 0, 0),
                               memory_space=pltpu.SMEM),
                  pl.BlockSpec((1, s, 1, d), lambda b, e: (b, 0, 0, 0))],
        out_specs=pl.BlockSpec((1, 1, cap, d), lambda b, e: (b, e, 0, 0)),
        out_shape=jax.ShapeDtypeStruct((bsz, n_e, cap, d), BF16),
        scratch_shapes=[pltpu.VMEM((cap * (d // LANES), LANES), F32)],
        compiler_params=_params(("parallel", "arbitrary")),
        name="gather",
    )(idx3, h4)


def _ffn_kernel(x_ref, val_ref, wg_hbm, wu_hbm, wd_hbm, y_ref,
                wg_bf, wu_bf, wd_bf, stage_g, stage_u, stage_d, sems):
    e = pl.program_id(0)
    b = pl.program_id(1)
    n_e = pl.num_programs(0)
    n_slices = pl.num_programs(1)
    rows_g = stage_g.shape[0]
    rows_d = stage_d.shape[0]
    slot = e % 2

    def slice_copies(expert, i):
        return (pltpu.make_async_copy(wg_hbm.at[expert, pl.ds(i * rows_g, rows_g), :], stage_g, sems.at[0]),
                pltpu.make_async_copy(wu_hbm.at[expert, pl.ds(i * rows_g, rows_g), :], stage_u, sems.at[1]),
                pltpu.make_async_copy(wd_hbm.at[expert, pl.ds(i * rows_d, rows_d), :], stage_d, sems.at[2]))

    def cast_slice(dst_slot, i):
        wg_bf[dst_slot, pl.ds(pl.multiple_of(i * rows_g, 16), rows_g), :] = stage_g[...].astype(BF16)
        wu_bf[dst_slot, pl.ds(pl.multiple_of(i * rows_g, 16), rows_g), :] = stage_u[...].astype(BF16)
        wd_bf[dst_slot, pl.ds(pl.multiple_of(i * rows_d, 16), rows_d), :] = stage_d[...].astype(BF16)

    def fetched_expert(expert):
        return jnp.minimum(expert + 1, n_e - 1)

    @pl.when(jnp.logical_and(e == 0, b == 0))
    def _():
        def load(i, carry):
            cps = slice_copies(0, i)
            for cp in cps:
                cp.start()
            for cp in cps:
                cp.wait()
            cast_slice(0, i)
            return carry
        lax.fori_loop(0, n_slices, load, 0)
        for cp in slice_copies(fetched_expert(0), 0):
            cp.start()

    x = x_ref[0, 0]
    gt = _dot(x, wg_bf[slot])
    up = _dot(x, wu_bf[slot])
    act = (gt * (1.0 / (1.0 + jnp.exp(-gt))) * up).astype(BF16)
    y = _dot(act, wd_bf[slot]) * val_ref[0, 0]
    y_ref[0, 0] = y.reshape(y.shape[0], 1, y.shape[1])

    for cp in slice_copies(fetched_expert(e), b):
        cp.wait()
    cast_slice(1 - slot, b)

    @pl.when(jnp.logical_or(e + 1 < n_e, b + 1 < n_slices))
    def _():
        wrap = b + 1 == n_slices
        e2 = jnp.where(wrap, e + 1, e)
        b2 = jnp.where(wrap, 0, b + 1)
        for cp in slice_copies(fetched_expert(e2), b2):
            cp.start()


def _ffn(xin, vals4, wg, wu, wd):
    bsz, n_e, cap, d = xin.shape
    f = wg.shape[-1]
    assert d % (16 * bsz) == 0 and f % (16 * bsz) == 0
    hbm = pl.BlockSpec(memory_space=pl.ANY)
    return pl.pallas_call(
        _ffn_kernel,
        grid=(n_e, bsz),
        in_specs=[pl.BlockSpec((1, 1, cap, d), lambda e, b: (b, e, 0, 0)),
                  pl.BlockSpec((1, 1, cap, 1), lambda e, b: (b, e, 0, 0)), hbm, hbm, hbm],
        out_specs=pl.BlockSpec((1, 1, cap, 1, d), lambda e, b: (b, e, 0, 0, 0)),
        out_shape=jax.ShapeDtypeStruct((bsz, n_e, cap, 1, d), F32),
        scratch_shapes=[pltpu.VMEM((2, d, f), BF16), pltpu.VMEM((2, d, f), BF16), pltpu.VMEM((2, f, d), BF16),
                        pltpu.VMEM((d // bsz, f), F32), pltpu.VMEM((d // bsz, f), F32),
                        pltpu.VMEM((f // bsz, d), F32), pltpu.SemaphoreType.DMA((3,))],
        compiler_params=_params(("arbitrary", "arbitrary")),
        name="ffn",
    )(xin, vals4, wg, wu, wd)


SCATTER_UNROLL = 8


def _combine_kernel(idx_ref, y_ref, o_ref, *, cap):
    @pl.when(pl.program_id(1) == 0)
    def _():
        o_ref[...] = jnp.zeros_like(o_ref)

    def chunk(g, carry):
        base = g * SCATTER_UNROLL
        rows = []
        for u in range(SCATTER_UNROLL):
            c = base + u
            t = idx_ref[0, 0, c]
            rows.append((t, o_ref[0, t] + y_ref[0, 0, c]))
        for t, r in rows:
            o_ref[0, t] = r
        return carry

    lax.fori_loop(0, cap // SCATTER_UNROLL, chunk, 0)


def _combine(idx3, y5, s):
    bsz, n_e, cap, _, d = y5.shape
    return pl.pallas_call(
        functools.partial(_combine_kernel, cap=cap),
        grid=(bsz, n_e),
        in_specs=[pl.BlockSpec((1, 1, cap), lambda b, e: (b * n_e + e, 0, 0),
                               memory_space=pltpu.SMEM),
                  pl.BlockSpec((1, 1, cap, 1, d), lambda b, e: (b, e, 0, 0, 0))],
        out_specs=pl.BlockSpec((1, s, 1, d), lambda b, e: (b, 0, 0, 0)),
        out_shape=jax.ShapeDtypeStruct((bsz, s, 1, d), F32),
        compiler_params=_params(("parallel", "arbitrary")),
        name="combine",
    )(idx3, y5)


FINAL_ROWS = 8


def _final_kernel(x1_ref, moe_ref, g2_ref, lg_ref, lb_ref, o_ref, moe2d_ref, rows_ref, *, alpha):
    tm, d = moe2d_ref.shape
    pieces = d // LANES

    def retile(g, carry):
        r0 = pl.multiple_of(g * FINAL_ROWS, FINAL_ROWS)
        for u in range(FINAL_ROWS):
            rows_ref[pl.ds(u * pieces, pieces), :] = moe_ref[0, r0 + u].reshape(pieces, LANES)
        cols = [rows_ref[pl.ds(j, FINAL_ROWS, stride=pieces), :] for j in range(pieces)]
        moe2d_ref[pl.ds(r0, FINAL_ROWS), :] = jnp.concatenate(cols, axis=1)
        return carry

    lax.fori_loop(0, tm // FINAL_ROWS, retile, 0)
    y = alpha * x1_ref[0] + g2_ref[0] * moe2d_ref[...]
    o_ref[0] = _layer_norm(y, lg_ref[...], lb_ref[...])


def _final(x1, moe4, mod3, ln_g, ln_b, alpha, tm=512):
    bsz, s, d = x1.shape
    row = pl.BlockSpec((1, tm, d), lambda b, i: (b, i, 0))
    vec = pl.BlockSpec((1, d), lambda b, i: (0, 0))
    return pl.pallas_call(
        functools.partial(_final_kernel, alpha=alpha),
        grid=(bsz, s // tm),
        in_specs=[row, pl.BlockSpec((1, tm, 1, d), lambda b, i: (b, i, 0, 0)),
                  pl.BlockSpec((1, 1, d), lambda b, i: (6 * b + 5, 0, 0)), vec, vec],
        out_specs=row,
        out_shape=jax.ShapeDtypeStruct((bsz, s, d), F32),
        scratch_shapes=[pltpu.VMEM((tm, d), F32), pltpu.VMEM((FINAL_ROWS * (d // LANES), LANES), F32)],
        compiler_params=_params(("parallel", "arbitrary")),
        name="final",
    )(x1, moe4, mod3, ln_g, ln_b)


def _rope_tables(s):
    inv = ROPE_THETA ** (-jnp.arange(0, MLA_ROPE, 2, dtype=F32) / MLA_ROPE)
    ang = jnp.arange(s, dtype=F32)[:, None] * inv[None, :]
    cos, sin = jnp.cos(ang), jnp.sin(ang)
    scale = (MLA_NOPE + MLA_ROPE) ** -0.5 * math.log2(math.e)
    z32 = jnp.zeros((s, HEAD_SLAB - MLA_NOPE - MLA_ROPE), F32)
    z64 = jnp.zeros((s, MLA_NOPE), F32)
    cosq = jnp.concatenate([jnp.full((s, MLA_NOPE), scale, F32), cos * scale, cos * scale, z32], 1)
    sinq_scaled = jnp.concatenate([z64, -sin * scale, sin * scale, z32], 1)
    cosk = jnp.concatenate([z64, cos, cos, z32], 1)
    sink = jnp.concatenate([z64, sin, sin, z32], 1)
    return cosq, sinq_scaled, cosk, sink


def _layout_weights(w_in, w_uq, w_ukv):
    d = w_in.shape[0]
    c0 = MLA_Q_LORA + MLA_KV_LORA
    kr = w_in[:, c0:c0 + MLA_ROPE]
    t1, t2 = kr[:, :HALF_ROPE], kr[:, HALF_ROPE:]
    z = lambda n: jnp.zeros((d, n), w_in.dtype)
    pad = HEAD_SLAB - MLA_NOPE - MLA_ROPE
    w_in_r = jnp.concatenate(
        [w_in[:, :c0], w_in[:, c0 + MLA_ROPE:],
         z(MLA_NOPE), t1, t2, z(pad),
         z(MLA_NOPE), -t2, t1, z(pad)], axis=1).astype(BF16)

    ql = w_uq.shape[0]
    wq = w_uq.reshape(ql, MLA_HEADS, MLA_NOPE + MLA_ROPE)
    qn, q1, q2 = wq[..., :MLA_NOPE], wq[..., MLA_NOPE:MLA_NOPE + HALF_ROPE], wq[..., MLA_NOPE + HALF_ROPE:]
    zq = lambda n: jnp.zeros((ql, MLA_HEADS, n), w_uq.dtype)
    wq_pre = jnp.concatenate([qn, q1, q2, zq(pad)], -1).reshape(ql, -1).astype(BF16)

    kl = w_ukv.shape[0]
    wkv = w_ukv.reshape(kl, MLA_HEADS, MLA_NOPE + MLA_V)
    wk = jnp.concatenate([wkv[..., :MLA_NOPE],
                          jnp.zeros((kl, MLA_HEADS, HEAD_SLAB - MLA_NOPE), w_ukv.dtype)],
                         -1).reshape(kl, -1).astype(BF16)
    wv = jnp.concatenate([wkv[..., MLA_NOPE:],
                          jnp.zeros((kl, MLA_HEADS, HEAD_SLAB - MLA_V), w_ukv.dtype)],
                         -1).reshape(kl, -1).T.astype(BF16)
    return w_in_r, wq_pre, wk, wv


def kernel(x, c, w_ada, b_ada, w_in, q_norm_g, w_uq, kv_norm_g, w_ukv, rel_bias, w_out, ln1_g,
           ln1_b, w_router, w_gate, w_up, w_down, ln2_g, ln2_b):
    bsz, s, d = x.shape
    depth = w_ada.shape[0]
    alpha = (2 * depth) ** 0.25
    cap = max(1, EC_CAPACITY_FACTOR * s // N_EXPERTS)
    assert s % (max(2 * Q_TILE_DIL, DIL_COPY_ROWS) * max(dl for _, dl in DIL_PATTERNS)) == 0
    assert all(win // 2 // dl == BAND_HALF for win, dl in DIL_PATTERNS)

    cosq, sinq, cosk, sink = _rope_tables(s)
    bias = _bias_tiles(rel_bias)

    for l in range(depth):
        mod3 = _ada(c, w_ada[l], b_ada[l]).reshape(bsz * 6, 1, d)
        w_in_r, wq_pre, wk, wv = _layout_weights(w_in[l], w_uq[l], w_ukv[l])
        q, k, v, dq, dk, dv = _proj(
            x, mod3, w_in_r, q_norm_g[l].reshape(1, -1), wq_pre,
            kv_norm_g[l].reshape(1, -1), wk, wv, cosq, sinq, cosk, sink)
        mla = _mla(q, k, v)
        dil_out = _dilated(dq, dk, dv, bias)
        x1, h2, aff = _mix(x, mla, dil_out, mod3, w_out[l].astype(BF16),
                           ln1_g[l].reshape(1, d), ln1_b[l].reshape(1, d),
                           w_router[l].T, alpha)
        idx, vals = _topk(aff, cap)
        idx3 = idx.reshape(bsz * N_EXPERTS, 1, cap)
        xin = _gather(idx3, h2, N_EXPERTS, cap)
        y = _ffn(xin, vals.reshape(bsz, N_EXPERTS, cap, 1), w_gate[l], w_up[l], w_down[l])
        moe = _combine(idx3, y, s)
        x = _final(x1, moe, mod3, ln2_g[l].reshape(1, d),
                   ln2_b[l].reshape(1, d), alpha)
    return x
```

```python
import functools
import math

import numpy as np
import jax
import jax.numpy as jnp
from jax import lax
from jax.experimental import pallas as pl
from jax.experimental.pallas import tpu as pltpu

MLA_HEADS = 8
MLA_NOPE = 64
MLA_ROPE = 32
MLA_V = 64
MLA_Q_LORA = 384
MLA_KV_LORA = 256
ROPE_THETA = 10000.0
DIL_HEADS = 8
DIL_HEAD_DIM = 64
DIL_PATTERNS = ((128, 1), (512, 4), (2048, 16))
REL_BUCKETS = 32
REL_MAX_EXACT = 8
REL_MAX_DIST = 1024
N_EXPERTS = 16
EC_CAPACITY_FACTOR = 2
NORM_EPS = 1e-6
NEG_INF = -1e30

LANES = 128
HEAD_SLAB = 128
VMEM_LIMIT = 48 * 1024 * 1024
F32 = jnp.float32
BF16 = jnp.bfloat16
HALF_ROPE = MLA_ROPE // 2
DIL_WIDTH = DIL_HEADS * DIL_HEAD_DIM
MLA_WIDTH = MLA_HEADS * MLA_V
BAND_HALF = 64
Q_TILE_DIL = 2 * BAND_HALF
K_WIN_DIL = 4 * BAND_HALF


def _params(sem, vmem=VMEM_LIMIT):
    return pltpu.CompilerParams(dimension_semantics=sem, vmem_limit_bytes=vmem)


def _dot(a, b):
    return jnp.dot(a, b, preferred_element_type=F32)


def _dot_nt(a, b):
    return lax.dot_general(a, b, (((1,), (1,)), ((), ())), preferred_element_type=F32)


def _layer_norm(y, g, b):
    mu = jnp.mean(y, axis=-1, keepdims=True)
    d = y - mu
    var = jnp.mean(d * d, axis=-1, keepdims=True)
    return d * lax.rsqrt(var + NORM_EPS) * g + b


def _ada_kernel(c_ref, w_ref, b_ref, o_ref):
    c = c_ref[...]
    s = c * (1.0 / (1.0 + jnp.exp(-c)))
    o_ref[...] = jnp.dot(s, w_ref[...], preferred_element_type=F32,
                         precision=lax.Precision.HIGHEST) + b_ref[...]


def _ada(c, w_ada, b_ada):
    bsz, d = c.shape
    n = w_ada.shape[1]
    tn = 1024
    return pl.pallas_call(
        _ada_kernel,
        grid=(n // tn,),
        in_specs=[pl.BlockSpec((bsz, d), lambda j: (0, 0)),
                  pl.BlockSpec((d, tn), lambda j: (0, j)),
                  pl.BlockSpec((1, tn), lambda j: (0, j))],
        out_specs=pl.BlockSpec((bsz, tn), lambda j: (0, j)),
        out_shape=jax.ShapeDtypeStruct((bsz, n), F32),
        compiler_params=_params(("arbitrary",)),
        name="ada",
    )(c, w_ada, b_ada.reshape(1, n))


_C_Q = 0
_C_KV = MLA_Q_LORA
_C_DQ = MLA_Q_LORA + MLA_KV_LORA
_C_DK = _C_DQ + DIL_WIDTH
_C_DV = _C_DK + DIL_WIDTH
_C_KR = _C_DV + DIL_WIDTH
_C_KRS = _C_KR + HEAD_SLAB
_C_END = _C_KRS + HEAD_SLAB


def _proj_kernel(x_ref, sc_ref, sh_ref, win_ref, gq_ref, wqp_ref, gkv_ref,
                 wk_ref, wv_ref, cosq_ref, sinq_ref, cosk_ref, sink_ref,
                 q_ref, k_ref, v_ref, dq_ref, dk_ref, dv_ref):
    h = (x_ref[0] * (1.0 + sc_ref[0]) + sh_ref[0]).astype(BF16)
    proj = _dot(h, win_ref[...])

    c_q = proj[:, _C_Q:_C_KV]
    cqn = (c_q * lax.rsqrt(jnp.mean(c_q * c_q, axis=-1, keepdims=True) + NORM_EPS)
           * gq_ref[...]).astype(BF16)
    q_pre = _dot(cqn, wqp_ref[...])
    cosq = cosq_ref[...]
    sinq = sinq_ref[...]
    lane = lax.broadcasted_iota(jnp.int32, (1, HEAD_SLAB), 1)
    first_half = lane < MLA_NOPE + HALF_ROPE

    c_kv = proj[:, _C_KV:_C_DQ]
    ckvn = (c_kv * lax.rsqrt(jnp.mean(c_kv * c_kv, axis=-1, keepdims=True) + NORM_EPS)
            * gkv_ref[...]).astype(BF16)
    k_nope = _dot(ckvn, wk_ref[...])
    vt = _dot_nt(wv_ref[...], ckvn)
    vrow = lax.broadcasted_iota(jnp.int32, vt.shape, 0)
    vt = jnp.where((vrow & MLA_V) != 0, 1.0, vt).astype(BF16)
    v_ref[0] = vt.reshape(MLA_HEADS, HEAD_SLAB, vt.shape[1])
    k_rope = proj[:, _C_KR:_C_KRS] * cosk_ref[...] + proj[:, _C_KRS:_C_END] * sink_ref[...]

    for hd in range(MLA_HEADS):
        sl = slice(hd * HEAD_SLAB, (hd + 1) * HEAD_SLAB)
        qp = q_pre[:, sl]
        partner = jnp.where(first_half, pltpu.roll(qp, HEAD_SLAB - HALF_ROPE, axis=1),
                            pltpu.roll(qp, HALF_ROPE, axis=1))
        q_ref[0, :, sl] = (qp * cosq + partner * sinq).astype(BF16)
        k_ref[0, :, sl] = (k_nope[:, sl] + k_rope).astype(BF16)

    dq_ref[0] = proj[:, _C_DQ:_C_DK] * (DIL_HEAD_DIM ** -0.5)
    dk_ref[0] = proj[:, _C_DK:_C_DV]
    dv_ref[0] = proj[:, _C_DV:_C_KR]


def _proj(x, mod3, w_in_r, gq, wq_pre, gkv, wk, wv, cosq, sinq, cosk, sink, tm=512):
    bsz, s, d = x.shape
    full = lambda a: pl.BlockSpec(a.shape, lambda b, i: (0,) * a.ndim)
    tab = pl.BlockSpec((tm, LANES), lambda b, i: (i, 0))
    hq = MLA_HEADS * HEAD_SLAB
    outs = [jax.ShapeDtypeStruct((bsz, s, hq), BF16),
            jax.ShapeDtypeStruct((bsz, s, hq), BF16),
            jax.ShapeDtypeStruct((bsz, MLA_HEADS, HEAD_SLAB, s), BF16),
            jax.ShapeDtypeStruct((bsz, s, DIL_WIDTH), F32),
            jax.ShapeDtypeStruct((bsz, s, DIL_WIDTH), F32),
            jax.ShapeDtypeStruct((bsz, s, DIL_WIDTH), F32)]
    ospec = lambda w: pl.BlockSpec((1, tm, w), lambda b, i: (b, i, 0))
    return pl.pallas_call(
        _proj_kernel,
        grid=(bsz, s // tm),
        in_specs=[pl.BlockSpec((1, tm, d), lambda b, i: (b, i, 0)),
                  pl.BlockSpec((1, 1, d), lambda b, i: (6 * b + 1, 0, 0)),
                  pl.BlockSpec((1, 1, d), lambda b, i: (6 * b + 0, 0, 0)),
                  full(w_in_r), full(gq), full(wq_pre), full(gkv),
                  full(wk), full(wv), tab, tab, tab, tab],
        out_specs=[ospec(hq), ospec(hq),
                   pl.BlockSpec((1, MLA_HEADS, HEAD_SLAB, tm), lambda b, i: (b, 0, 0, i)),
                   ospec(DIL_WIDTH),
                   ospec(DIL_WIDTH), ospec(DIL_WIDTH)],
        out_shape=outs,
        compiler_params=_params(("parallel", "arbitrary")),
        name="proj",
    )(x, mod3, mod3, w_in_r, gq, wq_pre, gkv, wk, wv, cosq, sinq, cosk, sink)


MLA_KEY_CHUNK = 512


MLA_Q_SUB = 256
MLA_VALUE_PARTS = 4


def _mla_kernel(q_ref, k_ref, vt_ref, o_ref, s_ref, p_ref):
    tq = q_ref.shape[1]
    seq = k_ref.shape[1]
    nchunk = seq // MLA_KEY_CHUNK
    units = [(a, j) for a in range(tq // MLA_Q_SUB) for j in range(2)]
    n = len(units)
    maxima = [None] * n
    heads_out = {}

    def chunk(c):
        return slice(c * MLA_KEY_CHUNK, (c + 1) * MLA_KEY_CHUNK)

    def scores(u, c):
        a, j = units[u]
        sl = slice(j * HEAD_SLAB, (j + 1) * HEAD_SLAB)
        s = _dot_nt(k_ref[0, chunk(c), sl], q_ref[0, a * MLA_Q_SUB:(a + 1) * MLA_Q_SUB, sl])
        s_ref[u % 2, chunk(c), :] = s
        mc = jnp.max(s, axis=0, keepdims=True)
        maxima[u] = mc if maxima[u] is None else jnp.maximum(maxima[u], mc)

    def probs(u, c):
        p_ref[u % 2, chunk(c), :] = jnp.exp2(s_ref[u % 2, chunk(c), :] - maxima[u]).astype(BF16)

    partial = {}

    def values_part(u, part):
        a, j = units[u]
        width = seq // MLA_VALUE_PARTS
        ks = slice(part * width, (part + 1) * width)
        d = _dot(vt_ref[0, j, :, ks], p_ref[u % 2, ks, :])
        partial[u] = d if part == 0 else partial[u] + d

    def values(u):
        a, j = units[u]
        acc = partial.pop(u)
        heads_out[(a, j)] = acc[0:MLA_V] / acc[MLA_V:MLA_V + 1]
        if j == 1:
            o_t = jnp.concatenate([heads_out[(a, 0)], heads_out[(a, 1)]], axis=0)
            o_ref[0, a * MLA_Q_SUB:(a + 1) * MLA_Q_SUB, :] = o_t.T.astype(o_ref.dtype)

    chunks_per_part = nchunk // MLA_VALUE_PARTS
    for stage in range(n + 2):
        for c in range(nchunk):
            if 0 <= stage - 2 < n and c % chunks_per_part == 0:
                values_part(stage - 2, c // chunks_per_part)
            if stage < n:
                scores(stage, c)
            if 0 <= stage - 1 < n:
                probs(stage - 1, c)
        if 0 <= stage - 2 < n:
            values(stage - 2)


def _mla(q, k, vt, tq=2048):
    bsz, s, _ = q.shape
    pairs = MLA_HEADS // 2
    return pl.pallas_call(
        _mla_kernel,
        grid=(bsz, pairs, s // tq),
        in_specs=[pl.BlockSpec((1, tq, 2 * HEAD_SLAB), lambda b, p, i: (b, i, p)),
                  pl.BlockSpec((1, s, 2 * HEAD_SLAB), lambda b, p, i: (b, 0, p)),
                  pl.BlockSpec((1, 2, 2 * MLA_V, s), lambda b, p, i: (b, p, 0, 0))],
        out_specs=pl.BlockSpec((1, tq, 2 * MLA_V), lambda b, p, i: (b, i, p)),
        out_shape=jax.ShapeDtypeStruct((bsz, s, MLA_WIDTH), BF16),
        scratch_shapes=[pltpu.VMEM((2, s, MLA_Q_SUB), F32),
                        pltpu.VMEM((2, s, MLA_Q_SUB), BF16)],
        compiler_params=_params(("parallel", "arbitrary", "arbitrary")),
        name="mla",
    )(q, k, vt)


def _t5_bucket(rel):
    half = REL_BUCKETS // 2
    ret = (rel > 0).astype(np.int32) * half
    n = np.abs(rel)
    large = REL_MAX_EXACT + (np.log(np.maximum(n, 1) / REL_MAX_EXACT)
                             / np.log(REL_MAX_DIST / REL_MAX_EXACT)
                             * (half - REL_MAX_EXACT)).astype(np.int32)
    large = np.minimum(large, half - 1)
    return ret + np.where(n < REL_MAX_EXACT, n, large).astype(np.int32)


def _bucket_tiles():
    a = np.arange(Q_TILE_DIL)[:, None]
    j = np.arange(K_WIN_DIL)[None, :]
    rel = j - BAND_HALF - a
    in_range = [np.ones_like(j, bool), j >= BAND_HALF, j < K_WIN_DIL - BAND_HALF]
    tiles = []
    for _, dil in DIL_PATTERNS:
        for ok in in_range:
            tiles.append(np.where((np.abs(rel) <= BAND_HALF) & ok, _t5_bucket(rel * dil), -1))
    return np.stack(tiles).astype(np.int32)


N_EDGE = 3


def _bias_kernel(rb_ref, bucket_ref, o_ref):
    bucket = bucket_ref[0]
    accs = [jnp.where(bucket < 0, NEG_INF, 0.0).astype(F32) for _ in range(DIL_HEADS)]
    for bk in range(REL_BUCKETS):
        hit = bucket == bk
        for hd in range(DIL_HEADS):
            accs[hd] = jnp.where(hit, rb_ref[bk, hd], accs[hd])
    for hd in range(DIL_HEADS):
        o_ref[0, hd] = accs[hd]


def _bias_tiles(rel_bias):
    buckets = jnp.asarray(_bucket_tiles())
    n = buckets.shape[0]
    return pl.pallas_call(
        _bias_kernel,
        grid=(n,),
        in_specs=[pl.BlockSpec(memory_space=pltpu.SMEM),
                  pl.BlockSpec((1, Q_TILE_DIL, K_WIN_DIL), lambda p: (p, 0, 0))],
        out_specs=pl.BlockSpec((1, DIL_HEADS, Q_TILE_DIL, K_WIN_DIL), lambda p: (p, 0, 0, 0)),
        out_shape=jax.ShapeDtypeStruct((n, DIL_HEADS, Q_TILE_DIL, K_WIN_DIL), F32),
        compiler_params=_params(("arbitrary",)),
        name="dil_bias",
    )(rel_bias.astype(F32), buckets)


DIL_TILES_PER_ITER = 32
DIL_COPY_ROWS = 256
DIL_MERGE_ROWS = 256


def _dil_kernel(q_ref, k_ref, v_ref, bias_ref, o_ref,
                k0_ref, k1_ref, vp_ref, op_ref, lp_ref, *, seq):
    lane = lax.broadcasted_iota(jnp.int32, (1, LANES), 1)
    low = lane < DIL_HEAD_DIM
    zeros = jnp.zeros((BAND_HALF, LANES), BF16)

    def rows(start, size, dil):
        return pl.ds(start, size) if dil == 1 else pl.ds(start, size, stride=dil)

    for p, (_, dil) in enumerate(DIL_PATTERNS):
        length = seq // dil
        ntile = length // Q_TILE_DIL
        span = length + 2 * BAND_HALF

        def deinterleave(r, carry, dil=dil, length=length, span=span):
            base = pl.multiple_of(r * span, 2 * BAND_HALF)
            for ref in (k0_ref, k1_ref, vp_ref):
                ref[pl.ds(base, BAND_HALF), :] = zeros
                ref[pl.ds(base + BAND_HALF + length, BAND_HALF), :] = zeros
            for cidx in range(length // DIL_COPY_ROWS):
                src = rows(r + dil * DIL_COPY_ROWS * cidx, DIL_COPY_ROWS, dil)
                dst = pl.ds(pl.multiple_of(base + BAND_HALF + DIL_COPY_ROWS * cidx, BAND_HALF),
                            DIL_COPY_ROWS)
                kk = k_ref[0, src, :]
                k0_ref[dst, :] = jnp.where(low, kk, 0.0).astype(BF16)
                k1_ref[dst, :] = jnp.where(low, 0.0, kk).astype(BF16)
                vp_ref[dst, :] = v_ref[0, src, :].astype(BF16)
            return carry

        lax.fori_loop(0, dil, deinterleave, 0)

        def tiles(it, carry, p=p, dil=dil, ntile=ntile, span=span):
            for u in range(DIL_TILES_PER_ITER):
                t = it * DIL_TILES_PER_ITER + u
                r = t // ntile
                i = t % ntile
                edge = jnp.where(i == 0, 1, jnp.where(i == ntile - 1, 2, 0))
                r0 = i * Q_TILE_DIL
                tok = rows(r + dil * r0, Q_TILE_DIL, dil)
                win = pl.ds(pl.multiple_of(r * span + r0, Q_TILE_DIL), K_WIN_DIL)
                q = q_ref[0, tok, :].astype(BF16)
                vw = vp_ref[win, :]
                pvs, maxs, dens = [], [], []
                for j, kref in enumerate((k0_ref, k1_ref)):
                    s = _dot_nt(q, kref[win, :]) + bias_ref[p * N_EDGE + edge, j]
                    m = jnp.max(s, axis=-1, keepdims=True)
                    e = jnp.exp(s - m)
                    dens.append(jnp.sum(e, axis=-1, keepdims=True))
                    pvs.append(_dot(e.astype(BF16), vw))
                    maxs.append(m)
                den = jnp.where(low, dens[0], dens[1])
                op_ref[p, tok, :] = jnp.where(low, pvs[0], pvs[1]) / den
                lp_ref[p, tok, :] = jnp.where(low, maxs[0], maxs[1]) + jnp.log(den)
            return carry

        lax.fori_loop(0, seq // Q_TILE_DIL // DIL_TILES_PER_ITER, tiles, 0)

    def merge(g, carry):
        sl = pl.ds(pl.multiple_of(g * DIL_MERGE_ROWS, DIL_MERGE_ROWS), DIL_MERGE_ROWS)
        ls = [lp_ref[p, sl, :] for p in range(len(DIL_PATTERNS))]
        m = functools.reduce(jnp.maximum, ls)
        es = [jnp.exp(l - m) for l in ls]
        num = functools.reduce(lambda a, b: a + b,
                               [e * op_ref[p, sl, :] for p, e in enumerate(es)])
        den = functools.reduce(lambda a, b: a + b, es)
        o_ref[0, sl, :] = (num / den).astype(o_ref.dtype)
        return carry

    lax.fori_loop(0, seq // DIL_MERGE_ROWS, merge, 0)


def _dilated(dq, dk, dv, bias):
    bsz, s, w = dq.shape
    pairs = DIL_HEADS // 2
    npat = len(DIL_PATTERNS)
    blk = pl.BlockSpec((1, s, LANES), lambda b, p: (b, 0, p))
    pad_rows = s + 2 * BAND_HALF * max(dl for _, dl in DIL_PATTERNS)
    assert (s // Q_TILE_DIL) % DIL_TILES_PER_ITER == 0
    return pl.pallas_call(
        functools.partial(_dil_kernel, seq=s),
        grid=(bsz, pairs),
        in_specs=[blk, blk, blk,
                  pl.BlockSpec((npat * N_EDGE, 2, Q_TILE_DIL, K_WIN_DIL), lambda b, p: (0, p, 0, 0))],
        out_specs=blk,
        out_shape=jax.ShapeDtypeStruct((bsz, s, w), BF16),
        scratch_shapes=[pltpu.VMEM((pad_rows, LANES), BF16)] * 3
                       + [pltpu.VMEM((npat, s, LANES), F32)] * 2,
        compiler_params=_params(("parallel", "arbitrary")),
        name="dilated",
    )(dq, dk, dv, bias)


def _mix_kernel(x_ref, mla_ref, dil_ref,
                g1_ref, sc2_ref, sh2_ref, wout_ref, lg_ref, lb_ref, wr_ref,
                x1_ref, h2_ref, aff_ref, *, alpha):
    mix = _dot(mla_ref[0], wout_ref[0:MLA_WIDTH, :]) + _dot(dil_ref[0], wout_ref[MLA_WIDTH:, :])
    y = alpha * x_ref[0] + g1_ref[0] * mix
    x1 = _layer_norm(y, lg_ref[...], lb_ref[...])
    x1_ref[0] = x1
    h2 = x1 * (1.0 + sc2_ref[0]) + sh2_ref[0]
    h2_ref[0] = h2.reshape(h2.shape[0], 1, h2.shape[1])
    logits = lax.dot_general(wr_ref[...], h2, (((1,), (1,)), ((), ())),
                             preferred_element_type=F32,
                             precision=lax.Precision.HIGHEST)
    mx = jnp.max(logits, axis=0, keepdims=True)
    ex = jnp.exp(logits - mx)
    aff_ref[0] = ex / jnp.sum(ex, axis=0, keepdims=True)


def _mix(x, mla, dil, mod3, w_out, ln_g, ln_b, w_router_t, alpha, tm=1024):
    bsz, s, d = x.shape
    row = lambda w: pl.BlockSpec((1, tm, w), lambda b, i: (b, i, 0))
    modspec = lambda k: pl.BlockSpec((1, 1, d), lambda b, i: (6 * b + k, 0, 0))
    full = lambda a: pl.BlockSpec(a.shape, lambda b, i: (0,) * a.ndim)
    return pl.pallas_call(
        functools.partial(_mix_kernel, alpha=alpha),
        grid=(bsz, s // tm),
        in_specs=[row(d), row(MLA_WIDTH), row(DIL_WIDTH)]
                 + [modspec(2), modspec(4), modspec(3),
                    full(w_out), full(ln_g), full(ln_b), full(w_router_t)],
        out_specs=[row(d), pl.BlockSpec((1, tm, 1, d), lambda b, i: (b, i, 0, 0)),
                   pl.BlockSpec((1, N_EXPERTS, tm), lambda b, i: (b, 0, i))],
        out_shape=[jax.ShapeDtypeStruct((bsz, s, d), F32),
                   jax.ShapeDtypeStruct((bsz, s, 1, d), F32),
                   jax.ShapeDtypeStruct((bsz, N_EXPERTS, s), F32)],
        compiler_params=_params(("parallel", "arbitrary")),
        name="mix",
    )(x, mla, dil, mod3, mod3, mod3, w_out, ln_g, ln_b, w_router_t)


def _cumsum_lanes(x, tri):
    rows, n = x.shape
    carry = jnp.zeros((rows, 1), F32)
    parts = []
    for j in range(n // LANES):
        inc = _dot(x[:, j * LANES:(j + 1) * LANES].astype(BF16), tri) + carry
        parts.append(inc)
        carry = inc[:, LANES - 1:LANES]
    return jnp.concatenate(parts, axis=1)


TOPK_EXPERTS_PER_ITER = 2


def _topk_kernel(aff_ref, idx_ref, val_ref, key_ref, *, cap):
    a = aff_ref[0]
    n_e, s = a.shape
    t = jnp.zeros((n_e, 1), jnp.int32)
    for bit in range(30, -1, -1):
        cand = t | (1 << bit)
        cnt = jnp.sum((a >= pltpu.bitcast(cand, F32)).astype(jnp.int32), axis=1, keepdims=True)
        t = jnp.where(cnt >= cap, cand, t)
    thr = pltpu.bitcast(t, F32)
    gt = a > thr
    eq = a == thr
    n_gt = jnp.sum(gt.astype(jnp.int32), axis=1, keepdims=True)
    ri = lax.broadcasted_iota(jnp.int32, (LANES, LANES), 0)
    ci = lax.broadcasted_iota(jnp.int32, (LANES, LANES), 1)
    tri = jnp.where(ri <= ci, 1.0, 0.0).astype(BF16)
    eq_f = jnp.where(eq, 1.0, 0.0)
    rank_eq = _cumsum_lanes(eq_f, tri) - eq_f
    sel = gt | (eq & (rank_eq < (cap - n_gt).astype(F32)))
    sel_f = jnp.where(sel, 1.0, 0.0)
    pos = _cumsum_lanes(sel_f, tri) - sel_f
    key_ref[...] = jnp.where(sel, pos.astype(jnp.int32), -1)

    tok = lax.broadcasted_iota(jnp.int32, (1, s), 1)
    tok_hi = (tok >> 6).astype(F32)
    tok_lo = (tok & 63).astype(F32)
    slot = lax.broadcasted_iota(jnp.int32, (cap, 1), 0)
    rid = lax.broadcasted_iota(jnp.int32, (8, s), 0)

    def per_expert(e):
        key = key_ref[pl.ds(e, 1), :]
        onehot = jnp.where(key == slot, 1.0, 0.0).astype(BF16)
        ar = aff_ref[0, pl.ds(e, 1), :]
        a_hi = ar.astype(BF16).astype(F32)
        r1 = ar - a_hi
        a_mid = r1.astype(BF16).astype(F32)
        a_lo = r1 - a_mid
        lhs = jnp.where(rid == 0, tok_hi,
              jnp.where(rid == 1, tok_lo,
              jnp.where(rid == 2, a_hi,
              jnp.where(rid == 3, a_mid,
              jnp.where(rid == 4, a_lo, 0.0))))).astype(BF16)
        res = _dot_nt(lhs, onehot)
        idx_ref[0, pl.ds(e, 1), :] = (res[0:1] * 64.0 + res[1:2]).astype(jnp.int32)
        val_ref[0, pl.ds(e, 1), :] = res[2:3] + res[3:4] + res[4:5]

    def expert_group(g, carry):
        for u in range(TOPK_EXPERTS_PER_ITER):
            per_expert(g * TOPK_EXPERTS_PER_ITER + u)
        return carry

    lax.fori_loop(0, n_e // TOPK_EXPERTS_PER_ITER, expert_group, 0)


def _topk(aff, cap):
    bsz, n_e, s = aff.shape
    return pl.pallas_call(
        functools.partial(_topk_kernel, cap=cap),
        grid=(bsz,),
        in_specs=[pl.BlockSpec((1, n_e, s), lambda b: (b, 0, 0))],
        out_specs=[pl.BlockSpec((1, n_e, cap), lambda b: (b, 0, 0)),
                   pl.BlockSpec((1, n_e, cap), lambda b: (b, 0, 0))],
        out_shape=[jax.ShapeDtypeStruct((bsz, n_e, cap), jnp.int32),
                   jax.ShapeDtypeStruct((bsz, n_e, cap), F32)],
        scratch_shapes=[pltpu.VMEM((n_e, s), jnp.int32)],
        compiler_params=_params(("parallel",)),
        name="topk",
    )(aff)


GATHER_ROWS = 16


def _gather_kernel(idx_ref, h_ref, o_ref, rows_ref, *, cap):
    d = o_ref.shape[-1]
    pieces = d // LANES

    def chunk(g, carry):
        base = pl.multiple_of(g * GATHER_ROWS, GATHER_ROWS)
        for u in range(GATHER_ROWS):
            row = h_ref[0, idx_ref[0, 0, base + u]]
            rows_ref[pl.ds(pl.multiple_of((base + u) * pieces, pieces), pieces), :] = row.reshape(pieces, LANES)
        cols = [rows_ref[pl.ds(base * pieces + j, GATHER_ROWS, stride=pieces), :] for j in range(pieces)]
        o_ref[0, 0, pl.ds(base, GATHER_ROWS), :] = jnp.concatenate(cols, axis=1).astype(o_ref.dtype)
        return carry

    lax.fori_loop(0, cap // GATHER_ROWS, chunk, 0)


def _gather(idx3, h4, n_e, cap):
    bsz, s, _, d = h4.shape
    return pl.pallas_call(
        functools.partial(_gather_kernel, cap=cap),
        grid=(bsz, n_e),
        in_specs=[pl.BlockSpec((1, 1, cap), lambda b, e: (b * n_e + e, 0, 0),
                               memory_space=pltpu.SMEM),
                  pl.BlockSpec((1, s, 1, d), lambda b, e: (b, 0, 0, 0))],
        out_specs=pl.BlockSpec((1, 1, cap, d), lambda b, e: (b, e, 0, 0)),
        out_shape=jax.ShapeDtypeStruct((bsz, n_e, cap, d), BF16),
        scratch_shapes=[pltpu.VMEM((cap * (d // LANES), LANES), F32)],
        compiler_params=_params(("parallel", "arbitrary")),
        name="gather",
    )(idx3, h4)


def _ffn_kernel(x_ref, val_ref, wg_hbm, wu_hbm, wd_hbm, y_ref,
                wg_bf, wu_bf, wd_bf, stage_g, stage_u, stage_d, sems):
    e = pl.program_id(0)
    b = pl.program_id(1)
    n_e = pl.num_programs(0)
    n_slices = pl.num_programs(1)
    rows_g = stage_g.shape[0]
    rows_d = stage_d.shape[0]
    slot = e % 2

    def slice_copies(expert, i):
        return (pltpu.make_async_copy(wg_hbm.at[expert, pl.ds(i * rows_g, rows_g), :], stage_g, sems.at[0]),
                pltpu.make_async_copy(wu_hbm.at[expert, pl.ds(i * rows_g, rows_g), :], stage_u, sems.at[1]),
                pltpu.make_async_copy(wd_hbm.at[expert, pl.ds(i * rows_d, rows_d), :], stage_d, sems.at[2]))

    def cast_slice(dst_slot, i):
        wg_bf[dst_slot, pl.ds(pl.multiple_of(i * rows_g, 16), rows_g), :] = stage_g[...].astype(BF16)
        wu_bf[dst_slot, pl.ds(pl.multiple_of(i * rows_g, 16), rows_g), :] = stage_u[...].astype(BF16)
        wd_bf[dst_slot, pl.ds(pl.multiple_of(i * rows_d, 16), rows_d), :] = stage_d[...].astype(BF16)

    def fetched_expert(expert):
        return jnp.minimum(expert + 1, n_e - 1)

    @pl.when(jnp.logical_and(e == 0, b == 0))
    def _():
        def load(i, carry):
            cps = slice_copies(0, i)
            for cp in cps:
                cp.start()
            for cp in cps:
                cp.wait()
            cast_slice(0, i)
            return carry
        lax.fori_loop(0, n_slices, load, 0)
        for cp in slice_copies(fetched_expert(0), 0):
            cp.start()

    x = x_ref[0, 0]
    gt = _dot(x, wg_bf[slot])
    up = _dot(x, wu_bf[slot])
    act = (gt * (1.0 / (1.0 + jnp.exp(-gt))) * up).astype(BF16)
    y = _dot(act, wd_bf[slot]) * val_ref[0, 0]
    y_ref[0, 0] = y.reshape(y.shape[0], 1, y.shape[1])

    for cp in slice_copies(fetched_expert(e), b):
        cp.wait()
    cast_slice(1 - slot, b)

    @pl.when(jnp.logical_or(e + 1 < n_e, b + 1 < n_slices))
    def _():
        wrap = b + 1 == n_slices
        e2 = jnp.where(wrap, e + 1, e)
        b2 = jnp.where(wrap, 0, b + 1)
        for cp in slice_copies(fetched_expert(e2), b2):
            cp.start()


def _ffn(xin, vals4, wg, wu, wd):
    bsz, n_e, cap, d = xin.shape
    f = wg.shape[-1]
    assert d % (16 * bsz) == 0 and f % (16 * bsz) == 0
    hbm = pl.BlockSpec(memory_space=pl.ANY)
    return pl.pallas_call(
        _ffn_kernel,
        grid=(n_e, bsz),
        in_specs=[pl.BlockSpec((1, 1, cap, d), lambda e, b: (b, e, 0, 0)),
                  pl.BlockSpec((1, 1, cap, 1), lambda e, b: (b, e, 0, 0)), hbm, hbm, hbm],
        out_specs=pl.BlockSpec((1, 1, cap, 1, d), lambda e, b: (b, e, 0, 0, 0)),
        out_shape=jax.ShapeDtypeStruct((bsz, n_e, cap, 1, d), F32),
        scratch_shapes=[pltpu.VMEM((2, d, f), BF16), pltpu.VMEM((2, d, f), BF16), pltpu.VMEM((2, f, d), BF16),
                        pltpu.VMEM((d // bsz, f), F32), pltpu.VMEM((d // bsz, f), F32),
                        pltpu.VMEM((f // bsz, d), F32), pltpu.SemaphoreType.DMA((3,))],
        compiler_params=_params(("arbitrary", "arbitrary")),
        name="ffn",
    )(xin, vals4, wg, wu, wd)


SCATTER_UNROLL = 16


def _combine_kernel(idx_ref, y_ref, o_ref, *, cap):
    @pl.when(pl.program_id(1) == 0)
    def _():
        o_ref[...] = jnp.zeros_like(o_ref)

    def chunk(g, carry):
        base = g * SCATTER_UNROLL
        rows = []
        for u in range(SCATTER_UNROLL):
            c = base + u
            t = idx_ref[0, 0, c]
            rows.append((t, o_ref[0, t] + y_ref[0, 0, c]))
        for t, r in rows:
            o_ref[0, t] = r
        return carry

    lax.fori_loop(0, cap // SCATTER_UNROLL, chunk, 0)


def _combine(idx3, y5, s):
    bsz, n_e, cap, _, d = y5.shape
    return pl.pallas_call(
        functools.partial(_combine_kernel, cap=cap),
        grid=(bsz, n_e),
        in_specs=[pl.BlockSpec((1, 1, cap), lambda b, e: (b * n_e + e, 0, 0),
                               memory_space=pltpu.SMEM),
                  pl.BlockSpec((1, 1, cap, 1, d), lambda b, e: (b, e, 0, 0, 0))],
        out_specs=pl.BlockSpec((1, s, 1, d), lambda b, e: (b, 0, 0, 0)),
        out_shape=jax.ShapeDtypeStruct((bsz, s, 1, d), F32),
        compiler_params=_params(("parallel", "arbitrary")),
        name="combine",
    )(idx3, y5)


FINAL_ROWS = 8


def _final_kernel(x1_ref, moe_ref, g2_ref, lg_ref, lb_ref, o_ref, moe2d_ref, rows_ref, *, alpha):
    tm, d = moe2d_ref.shape
    pieces = d // LANES

    def retile(g, carry):
        r0 = pl.multiple_of(g * FINAL_ROWS, FINAL_ROWS)
        for u in range(FINAL_ROWS):
            rows_ref[pl.ds(u * pieces, pieces), :] = moe_ref[0, r0 + u].reshape(pieces, LANES)
        cols = [rows_ref[pl.ds(j, FINAL_ROWS, stride=pieces), :] for j in range(pieces)]
        moe2d_ref[pl.ds(r0, FINAL_ROWS), :] = jnp.concatenate(cols, axis=1)
        return carry

    lax.fori_loop(0, tm // FINAL_ROWS, retile, 0)
    y = alpha * x1_ref[0] + g2_ref[0] * moe2d_ref[...]
    o_ref[0] = _layer_norm(y, lg_ref[...], lb_ref[...])


def _final(x1, moe4, mod3, ln_g, ln_b, alpha, tm=512):
    bsz, s, d = x1.shape
    row = pl.BlockSpec((1, tm, d), lambda b, i: (b, i, 0))
    vec = pl.BlockSpec((1, d), lambda b, i: (0, 0))
    return pl.pallas_call(
        functools.partial(_final_kernel, alpha=alpha),
        grid=(bsz, s // tm),
        in_specs=[row, pl.BlockSpec((1, tm, 1, d), lambda b, i: (b, i, 0, 0)),
                  pl.BlockSpec((1, 1, d), lambda b, i: (6 * b + 5, 0, 0)), vec, vec],
        out_specs=row,
        out_shape=jax.ShapeDtypeStruct((bsz, s, d), F32),
        scratch_shapes=[pltpu.VMEM((tm, d), F32), pltpu.VMEM((FINAL_ROWS * (d // LANES), LANES), F32)],
        compiler_params=_params(("parallel", "arbitrary")),
        name="final",
    )(x1, moe4, mod3, ln_g, ln_b)


def _rope_tables(s):
    inv = ROPE_THETA ** (-jnp.arange(0, MLA_ROPE, 2, dtype=F32) / MLA_ROPE)
    ang = jnp.arange(s, dtype=F32)[:, None] * inv[None, :]
    cos, sin = jnp.cos(ang), jnp.sin(ang)
    scale = (MLA_NOPE + MLA_ROPE) ** -0.5 * math.log2(math.e)
    z32 = jnp.zeros((s, HEAD_SLAB - MLA_NOPE - MLA_ROPE), F32)
    z64 = jnp.zeros((s, MLA_NOPE), F32)
    cosq = jnp.concatenate([jnp.full((s, MLA_NOPE), scale, F32), cos * scale, cos * scale, z32], 1)
    sinq_scaled = jnp.concatenate([z64, -sin * scale, sin * scale, z32], 1)
    cosk = jnp.concatenate([z64, cos, cos, z32], 1)
    sink = jnp.concatenate([z64, sin, sin, z32], 1)
    return cosq, sinq_scaled, cosk, sink


def _layout_weights(w_in, w_uq, w_ukv):
    d = w_in.shape[0]
    c0 = MLA_Q_LORA + MLA_KV_LORA
    kr = w_in[:, c0:c0 + MLA_ROPE]
    t1, t2 = kr[:, :HALF_ROPE], kr[:, HALF_ROPE:]
    z = lambda n: jnp.zeros((d, n), w_in.dtype)
    pad = HEAD_SLAB - MLA_NOPE - MLA_ROPE
    w_in_r = jnp.concatenate(
        [w_in[:, :c0], w_in[:, c0 + MLA_ROPE:],
         z(MLA_NOPE), t1, t2, z(pad),
         z(MLA_NOPE), -t2, t1, z(pad)], axis=1).astype(BF16)

    ql = w_uq.shape[0]
    wq = w_uq.reshape(ql, MLA_HEADS, MLA_NOPE + MLA_ROPE)
    qn, q1, q2 = wq[..., :MLA_NOPE], wq[..., MLA_NOPE:MLA_NOPE + HALF_ROPE], wq[..., MLA_NOPE + HALF_ROPE:]
    zq = lambda n: jnp.zeros((ql, MLA_HEADS, n), w_uq.dtype)
    wq_pre = jnp.concatenate([qn, q1, q2, zq(pad)], -1).reshape(ql, -1).astype(BF16)

    kl = w_ukv.shape[0]
    wkv = w_ukv.reshape(kl, MLA_HEADS, MLA_NOPE + MLA_V)
    wk = jnp.concatenate([wkv[..., :MLA_NOPE],
                          jnp.zeros((kl, MLA_HEADS, HEAD_SLAB - MLA_NOPE), w_ukv.dtype)],
                         -1).reshape(kl, -1).astype(BF16)
    wv = jnp.concatenate([wkv[..., MLA_NOPE:],
                          jnp.zeros((kl, MLA_HEADS, HEAD_SLAB - MLA_V), w_ukv.dtype)],
                         -1).reshape(kl, -1).T.astype(BF16)
    return w_in_r, wq_pre, wk, wv


def kernel(x, c, w_ada, b_ada, w_in, q_norm_g, w_uq, kv_norm_g, w_ukv, rel_bias, w_out, ln1_g,
           ln1_b, w_router, w_gate, w_up, w_down, ln2_g, ln2_b):
    bsz, s, d = x.shape
    depth = w_ada.shape[0]
    alpha = (2 * depth) ** 0.25
    cap = max(1, EC_CAPACITY_FACTOR * s // N_EXPERTS)
    assert s % (max(2 * Q_TILE_DIL, DIL_COPY_ROWS) * max(dl for _, dl in DIL_PATTERNS)) == 0
    assert all(win // 2 // dl == BAND_HALF for win, dl in DIL_PATTERNS)

    cosq, sinq, cosk, sink = _rope_tables(s)
    bias = _bias_tiles(rel_bias)

    for l in range(depth):
        mod3 = _ada(c, w_ada[l], b_ada[l]).reshape(bsz * 6, 1, d)
        w_in_r, wq_pre, wk, wv = _layout_weights(w_in[l], w_uq[l], w_ukv[l])
        q, k, v, dq, dk, dv = _proj(
            x, mod3, w_in_r, q_norm_g[l].reshape(1, -1), wq_pre,
            kv_norm_g[l].reshape(1, -1), wk, wv, cosq, sinq, cosk, sink)
        mla = _mla(q, k, v)
        dil_out = _dilated(dq, dk, dv, bias)
        x1, h2, aff = _mix(x, mla, dil_out, mod3, w_out[l].astype(BF16),
                           ln1_g[l].reshape(1, d), ln1_b[l].reshape(1, d),
                           w_router[l].T, alpha)
        idx, vals = _topk(aff, cap)
        idx3 = idx.reshape(bsz * N_EXPERTS, 1, cap)
        xin = _gather(idx3, h2, N_EXPERTS, cap)
        y = _ffn(xin, vals.reshape(bsz, N_EXPERTS, cap, 1), w_gate[l], w_up[l], w_down[l])
        moe = _combine(idx3, y, s)
        x = _final(x1, moe, mod3, ln2_g[l].reshape(1, d),
                   ln2_b[l].reshape(1, d), alpha)
    return x
```

```python
import functools
import math

import numpy as np
import jax
import jax.numpy as jnp
from jax import lax
from jax.experimental import pallas as pl
from jax.experimental.pallas import tpu as pltpu

MLA_HEADS = 8
MLA_NOPE = 64
MLA_ROPE = 32
MLA_V = 64
MLA_Q_LORA = 384
MLA_KV_LORA = 256
ROPE_THETA = 10000.0
DIL_HEADS = 8
DIL_HEAD_DIM = 64
DIL_PATTERNS = ((128, 1), (512, 4), (2048, 16))
REL_BUCKETS = 32
REL_MAX_EXACT = 8
REL_MAX_DIST = 1024
N_EXPERTS = 16
EC_CAPACITY_FACTOR = 2
NORM_EPS = 1e-6
NEG_INF = -1e30

LANES = 128
HEAD_SLAB = 128
VMEM_LIMIT = 48 * 1024 * 1024
F32 = jnp.float32
BF16 = jnp.bfloat16
HALF_ROPE = MLA_ROPE // 2
DIL_WIDTH = DIL_HEADS * DIL_HEAD_DIM
MLA_WIDTH = MLA_HEADS * MLA_V
BAND_HALF = 64
Q_TILE_DIL = 2 * BAND_HALF
K_WIN_DIL = 4 * BAND_HALF


def _params(sem, vmem=VMEM_LIMIT):
    return pltpu.CompilerParams(dimension_semantics=sem, vmem_limit_bytes=vmem)


def _dot(a, b):
    return jnp.dot(a, b, preferred_element_type=F32)


def _dot_nt(a, b):
    return lax.dot_general(a, b, (((1,), (1,)), ((), ())), preferred_element_type=F32)


def _layer_norm(y, g, b):
    mu = jnp.mean(y, axis=-1, keepdims=True)
    d = y - mu
    var = jnp.mean(d * d, axis=-1, keepdims=True)
    return d * lax.rsqrt(var + NORM_EPS) * g + b


def _ada_kernel(c_ref, w_ref, b_ref, o_ref):
    c = c_ref[...]
    s = c * (1.0 / (1.0 + jnp.exp(-c)))
    o_ref[...] = jnp.dot(s, w_ref[...], preferred_element_type=F32,
                         precision=lax.Precision.HIGHEST) + b_ref[...]


def _ada(c, w_ada, b_ada):
    bsz, d = c.shape
    n = w_ada.shape[1]
    tn = 1024
    return pl.pallas_call(
        _ada_kernel,
        grid=(n // tn,),
        in_specs=[pl.BlockSpec((bsz, d), lambda j: (0, 0)),
                  pl.BlockSpec((d, tn), lambda j: (0, j)),
                  pl.BlockSpec((1, tn), lambda j: (0, j))],
        out_specs=pl.BlockSpec((bsz, tn), lambda j: (0, j)),
        out_shape=jax.ShapeDtypeStruct((bsz, n), F32),
        compiler_params=_params(("arbitrary",)),
        name="ada",
    )(c, w_ada, b_ada.reshape(1, n))


_C_Q = 0
_C_KV = MLA_Q_LORA
_C_DQ = MLA_Q_LORA + MLA_KV_LORA
_C_DK = _C_DQ + DIL_WIDTH
_C_DV = _C_DK + DIL_WIDTH
_C_KR = _C_DV + DIL_WIDTH
_C_KRS = _C_KR + HEAD_SLAB
_C_END = _C_KRS + HEAD_SLAB


def _proj_kernel(x_ref, sc_ref, sh_ref, win_ref, gq_ref, wqp_ref, gkv_ref,
                 wk_ref, wv_ref, cosq_ref, sinq_ref, cosk_ref, sink_ref,
                 q_ref, k_ref, v_ref, dq_ref, dk_ref, dv_ref):
    h = (x_ref[0] * (1.0 + sc_ref[0]) + sh_ref[0]).astype(BF16)
    proj = _dot(h, win_ref[...])

    c_q = proj[:, _C_Q:_C_KV]
    cqn = (c_q * lax.rsqrt(jnp.mean(c_q * c_q, axis=-1, keepdims=True) + NORM_EPS)
           * gq_ref[...]).astype(BF16)
    q_pre = _dot(cqn, wqp_ref[...])
    cosq = cosq_ref[...]
    sinq = sinq_ref[...]
    lane = lax.broadcasted_iota(jnp.int32, (1, HEAD_SLAB), 1)
    first_half = lane < MLA_NOPE + HALF_ROPE

    c_kv = proj[:, _C_KV:_C_DQ]
    ckvn = (c_kv * lax.rsqrt(jnp.mean(c_kv * c_kv, axis=-1, keepdims=True) + NORM_EPS)
            * gkv_ref[...]).astype(BF16)
    k_nope = _dot(ckvn, wk_ref[...])
    vt = _dot_nt(wv_ref[...], ckvn)
    vrow = lax.broadcasted_iota(jnp.int32, vt.shape, 0)
    vt = jnp.where((vrow & MLA_V) != 0, 1.0, vt).astype(BF16)
    v_ref[0] = vt.reshape(MLA_HEADS, HEAD_SLAB, vt.shape[1])
    k_rope = proj[:, _C_KR:_C_KRS] * cosk_ref[...] + proj[:, _C_KRS:_C_END] * sink_ref[...]

    for hd in range(MLA_HEADS):
        sl = slice(hd * HEAD_SLAB, (hd + 1) * HEAD_SLAB)
        qp = q_pre[:, sl]
        partner = jnp.where(first_half, pltpu.roll(qp, HEAD_SLAB - HALF_ROPE, axis=1),
                            pltpu.roll(qp, HALF_ROPE, axis=1))
        q_ref[0, :, sl] = (qp * cosq + partner * sinq).astype(BF16)
        k_ref[0, :, sl] = (k_nope[:, sl] + k_rope).astype(BF16)

    dq_ref[0] = proj[:, _C_DQ:_C_DK] * (DIL_HEAD_DIM ** -0.5)
    dk_ref[0] = proj[:, _C_DK:_C_DV]
    dv_ref[0] = proj[:, _C_DV:_C_KR]


def _proj(x, mod3, w_in_r, gq, wq_pre, gkv, wk, wv, cosq, sinq, cosk, sink, tm=512):
    bsz, s, d = x.shape
    full = lambda a: pl.BlockSpec(a.shape, lambda b, i: (0,) * a.ndim)
    tab = pl.BlockSpec((tm, LANES), lambda b, i: (i, 0))
    hq = MLA_HEADS * HEAD_SLAB
    outs = [jax.ShapeDtypeStruct((bsz, s, hq), BF16),
            jax.ShapeDtypeStruct((bsz, s, hq), BF16),
            jax.ShapeDtypeStruct((bsz, MLA_HEADS, HEAD_SLAB, s), BF16),
            jax.ShapeDtypeStruct((bsz, s, DIL_WIDTH), F32),
            jax.ShapeDtypeStruct((bsz, s, DIL_WIDTH), F32),
            jax.ShapeDtypeStruct((bsz, s, DIL_WIDTH), F32)]
    ospec = lambda w: pl.BlockSpec((1, tm, w), lambda b, i: (b, i, 0))
    return pl.pallas_call(
        _proj_kernel,
        grid=(bsz, s // tm),
        in_specs=[pl.BlockSpec((1, tm, d), lambda b, i: (b, i, 0)),
                  pl.BlockSpec((1, 1, d), lambda b, i: (6 * b + 1, 0, 0)),
                  pl.BlockSpec((1, 1, d), lambda b, i: (6 * b + 0, 0, 0)),
                  full(w_in_r), full(gq), full(wq_pre), full(gkv),
                  full(wk), full(wv), tab, tab, tab, tab],
        out_specs=[ospec(hq), ospec(hq),
                   pl.BlockSpec((1, MLA_HEADS, HEAD_SLAB, tm), lambda b, i: (b, 0, 0, i)),
                   ospec(DIL_WIDTH),
                   ospec(DIL_WIDTH), ospec(DIL_WIDTH)],
        out_shape=outs,
        compiler_params=_params(("parallel", "arbitrary")),
        name="proj",
    )(x, mod3, mod3, w_in_r, gq, wq_pre, gkv, wk, wv, cosq, sinq, cosk, sink)


MLA_KEY_CHUNK = 512


MLA_Q_SUB = 256
MLA_VALUE_PARTS = 4


def _mla_kernel(q_ref, k_ref, vt_ref, o_ref, s_ref, p_ref):
    tq = q_ref.shape[1]
    seq = k_ref.shape[1]
    nchunk = seq // MLA_KEY_CHUNK
    units = [(a, j) for a in range(tq // MLA_Q_SUB) for j in range(2)]
    n = len(units)
    maxima = [None] * n
    heads_out = {}

    def chunk(c):
        return slice(c * MLA_KEY_CHUNK, (c + 1) * MLA_KEY_CHUNK)

    def scores(u, c):
        a, j = units[u]
        sl = slice(j * HEAD_SLAB, (j + 1) * HEAD_SLAB)
        s = _dot_nt(k_ref[0, chunk(c), sl], q_ref[0, a * MLA_Q_SUB:(a + 1) * MLA_Q_SUB, sl])
        s_ref[u % 2, chunk(c), :] = s
        mc = jnp.max(s, axis=0, keepdims=True)
        maxima[u] = mc if maxima[u] is None else jnp.maximum(maxima[u], mc)

    def probs(u, c):
        p_ref[u % 2, chunk(c), :] = jnp.exp2(s_ref[u % 2, chunk(c), :] - maxima[u]).astype(BF16)

    partial = {}

    def values_part(u, part):
        a, j = units[u]
        width = seq // MLA_VALUE_PARTS
        ks = slice(part * width, (part + 1) * width)
        d = _dot(vt_ref[0, j, :, ks], p_ref[u % 2, ks, :])
        partial[u] = d if part == 0 else partial[u] + d

    def values(u):
        a, j = units[u]
        acc = partial.pop(u)
        heads_out[(a, j)] = acc[0:MLA_V] / acc[MLA_V:MLA_V + 1]
        if j == 1:
            o_t = jnp.concatenate([heads_out[(a, 0)], heads_out[(a, 1)]], axis=0)
            o_ref[0, a * MLA_Q_SUB:(a + 1) * MLA_Q_SUB, :] = o_t.T.astype(o_ref.dtype)

    chunks_per_part = nchunk // MLA_VALUE_PARTS
    for stage in range(n + 2):
        for c in range(nchunk):
            if 0 <= stage - 2 < n and c % chunks_per_part == 0:
                values_part(stage - 2, c // chunks_per_part)
            if stage < n:
                scores(stage, c)
            if 0 <= stage - 1 < n:
                probs(stage - 1, c)
        if 0 <= stage - 2 < n:
            values(stage - 2)


def _mla(q, k, vt, tq=2048):
    bsz, s, _ = q.shape
    pairs = MLA_HEADS // 2
    return pl.pallas_call(
        _mla_kernel,
        grid=(bsz, pairs, s // tq),
        in_specs=[pl.BlockSpec((1, tq, 2 * HEAD_SLAB), lambda b, p, i: (b, i, p)),
                  pl.BlockSpec((1, s, 2 * HEAD_SLAB), lambda b, p, i: (b, 0, p)),
                  pl.BlockSpec((1, 2, 2 * MLA_V, s), lambda b, p, i: (b, p, 0, 0))],
        out_specs=pl.BlockSpec((1, tq, 2 * MLA_V), lambda b, p, i: (b, i, p)),
        out_shape=jax.ShapeDtypeStruct((bsz, s, MLA_WIDTH), BF16),
        scratch_shapes=[pltpu.VMEM((2, s, MLA_Q_SUB), F32),
                        pltpu.VMEM((2, s, MLA_Q_SUB), BF16)],
        compiler_params=_params(("parallel", "arbitrary", "arbitrary")),
        name="mla",
    )(q, k, vt)


def _t5_bucket(rel):
    half = REL_BUCKETS // 2
    ret = (rel > 0).astype(np.int32) * half
    n = np.abs(rel)
    large = REL_MAX_EXACT + (np.log(np.maximum(n, 1) / REL_MAX_EXACT)
                             / np.log(REL_MAX_DIST / REL_MAX_EXACT)
                             * (half - REL_MAX_EXACT)).astype(np.int32)
    large = np.minimum(large, half - 1)
    return ret + np.where(n < REL_MAX_EXACT, n, large).astype(np.int32)


def _bucket_tiles():
    a = np.arange(Q_TILE_DIL)[:, None]
    j = np.arange(K_WIN_DIL)[None, :]
    rel = j - BAND_HALF - a
    in_range = [np.ones_like(j, bool), j >= BAND_HALF, j < K_WIN_DIL - BAND_HALF]
    tiles = []
    for _, dil in DIL_PATTERNS:
        for ok in in_range:
            tiles.append(np.where((np.abs(rel) <= BAND_HALF) & ok, _t5_bucket(rel * dil), -1))
    return np.stack(tiles).astype(np.int32)


N_EDGE = 3


def _bias_kernel(rb_ref, bucket_ref, o_ref):
    bucket = bucket_ref[0]
    accs = [jnp.where(bucket < 0, NEG_INF, 0.0).astype(F32) for _ in range(DIL_HEADS)]
    for bk in range(REL_BUCKETS):
        hit = bucket == bk
        for hd in range(DIL_HEADS):
            accs[hd] = jnp.where(hit, rb_ref[bk, hd], accs[hd])
    for hd in range(DIL_HEADS):
        o_ref[0, hd] = accs[hd]


def _bias_tiles(rel_bias):
    buckets = jnp.asarray(_bucket_tiles())
    n = buckets.shape[0]
    return pl.pallas_call(
        _bias_kernel,
        grid=(n,),
        in_specs=[pl.BlockSpec(memory_space=pltpu.SMEM),
                  pl.BlockSpec((1, Q_TILE_DIL, K_WIN_DIL), lambda p: (p, 0, 0))],
        out_specs=pl.BlockSpec((1, DIL_HEADS, Q_TILE_DIL, K_WIN_DIL), lambda p: (p, 0, 0, 0)),
        out_shape=jax.ShapeDtypeStruct((n, DIL_HEADS, Q_TILE_DIL, K_WIN_DIL), F32),
        compiler_params=_params(("arbitrary",)),
        name="dil_bias",
    )(rel_bias.astype(F32), buckets)


DIL_TILES_PER_ITER = 32
DIL_COPY_ROWS = 256
DIL_MERGE_ROWS = 256


def _dil_kernel(q_ref, k_ref, v_ref, bias_ref, o_ref,
                k0_ref, k1_ref, vp_ref, op_ref, lp_ref, *, seq):
    lane = lax.broadcasted_iota(jnp.int32, (1, LANES), 1)
    low = lane < DIL_HEAD_DIM
    zeros = jnp.zeros((BAND_HALF, LANES), BF16)

    def rows(start, size, dil):
        return pl.ds(start, size) if dil == 1 else pl.ds(start, size, stride=dil)

    for p, (_, dil) in enumerate(DIL_PATTERNS):
        length = seq // dil
        ntile = length // Q_TILE_DIL
        span = length + 2 * BAND_HALF

        def deinterleave(r, carry, dil=dil, length=length, span=span):
            base = pl.multiple_of(r * span, 2 * BAND_HALF)
            for ref in (k0_ref, k1_ref, vp_ref):
                ref[pl.ds(base, BAND_HALF), :] = zeros
                ref[pl.ds(base + BAND_HALF + length, BAND_HALF), :] = zeros
            for cidx in range(length // DIL_COPY_ROWS):
                src = rows(r + dil * DIL_COPY_ROWS * cidx, DIL_COPY_ROWS, dil)
                dst = pl.ds(pl.multiple_of(base + BAND_HALF + DIL_COPY_ROWS * cidx, BAND_HALF),
                            DIL_COPY_ROWS)
                kk = k_ref[0, src, :]
                k0_ref[dst, :] = jnp.where(low, kk, 0.0).astype(BF16)
                k1_ref[dst, :] = jnp.where(low, 0.0, kk).astype(BF16)
                vp_ref[dst, :] = v_ref[0, src, :].astype(BF16)
            return carry

        lax.fori_loop(0, dil, deinterleave, 0)

        def tiles(it, carry, p=p, dil=dil, ntile=ntile, span=span):
            for u in range(DIL_TILES_PER_ITER):
                t = it * DIL_TILES_PER_ITER + u
                r = t // ntile
                i = t % ntile
                edge = jnp.where(i == 0, 1, jnp.where(i == ntile - 1, 2, 0))
                r0 = i * Q_TILE_DIL
                tok = rows(r + dil * r0, Q_TILE_DIL, dil)
                win = pl.ds(pl.multiple_of(r * span + r0, Q_TILE_DIL), K_WIN_DIL)
                q = q_ref[0, tok, :].astype(BF16)
                vw = vp_ref[win, :]
                pvs, maxs, dens = [], [], []
                for j, kref in enumerate((k0_ref, k1_ref)):
                    s = _dot_nt(q, kref[win, :]) + bias_ref[p * N_EDGE + edge, j]
                    m = jnp.max(s, axis=-1, keepdims=True)
                    e = jnp.exp(s - m)
                    dens.append(jnp.sum(e, axis=-1, keepdims=True))
                    pvs.append(_dot(e.astype(BF16), vw))
                    maxs.append(m)
                den = jnp.where(low, dens[0], dens[1])
                op_ref[p, tok, :] = jnp.where(low, pvs[0], pvs[1]) / den
                lp_ref[p, tok, :] = jnp.where(low, maxs[0], maxs[1]) + jnp.log(den)
            return carry

        lax.fori_loop(0, seq // Q_TILE_DIL // DIL_TILES_PER_ITER, tiles, 0)

    def merge(g, carry):
        sl = pl.ds(pl.multiple_of(g * DIL_MERGE_ROWS, DIL_MERGE_ROWS), DIL_MERGE_ROWS)
        ls = [lp_ref[p, sl, :] for p in range(len(DIL_PATTERNS))]
        m = functools.reduce(jnp.maximum, ls)
        es = [jnp.exp(l - m) for l in ls]
        num = functools.reduce(lambda a, b: a + b,
                               [e * op_ref[p, sl, :] for p, e in enumerate(es)])
        den = functools.reduce(lambda a, b: a + b, es)
        o_ref[0, sl, :] = (num / den).astype(o_ref.dtype)
        return carry

    lax.fori_loop(0, seq // DIL_MERGE_ROWS, merge, 0)


def _dilated(dq, dk, dv, bias):
    bsz, s, w = dq.shape
    pairs = DIL_HEADS // 2
    npat = len(DIL_PATTERNS)
    blk = pl.BlockSpec((1, s, LANES), lambda b, p: (b, 0, p))
    pad_rows = s + 2 * BAND_HALF * max(dl for _, dl in DIL_PATTERNS)
    assert (s // Q_TILE_DIL) % DIL_TILES_PER_ITER == 0
    return pl.pallas_call(
        functools.partial(_dil_kernel, seq=s),
        grid=(bsz, pairs),
        in_specs=[blk, blk, blk,
                  pl.BlockSpec((npat * N_EDGE, 2, Q_TILE_DIL, K_WIN_DIL), lambda b, p: (0, p, 0, 0))],
        out_specs=blk,
        out_shape=jax.ShapeDtypeStruct((bsz, s, w), BF16),
        scratch_shapes=[pltpu.VMEM((pad_rows, LANES), BF16)] * 3
                       + [pltpu.VMEM((npat, s, LANES), F32)] * 2,
        compiler_params=_params(("parallel", "arbitrary")),
        name="dilated",
    )(dq, dk, dv, bias)


def _mix_kernel(x_ref, mla_ref, dil_ref,
                g1_ref, sc2_ref, sh2_ref, wout_ref, lg_ref, lb_ref, wr_ref,
                x1_ref, h2_ref, aff_ref, *, alpha):
    mix = _dot(mla_ref[0], wout_ref[0:MLA_WIDTH, :]) + _dot(dil_ref[0], wout_ref[MLA_WIDTH:, :])
    y = alpha * x_ref[0] + g1_ref[0] * mix
    x1 = _layer_norm(y, lg_ref[...], lb_ref[...])
    x1_ref[0] = x1
    h2 = x1 * (1.0 + sc2_ref[0]) + sh2_ref[0]
    h2_ref[0] = h2.reshape(h2.shape[0], 1, h2.shape[1])
    wr = wr_ref[...]
    w_hi = wr.astype(BF16)
    w_lo = (wr - w_hi.astype(F32)).astype(BF16)
    h_hi = h2.astype(BF16)
    h_lo = (h2 - h_hi.astype(F32)).astype(BF16)
    n_e = wr.shape[0]
    both = _dot_nt(jnp.concatenate([w_hi, w_lo], axis=0), h_hi)
    logits = both[0:n_e] + both[n_e:] + _dot_nt(w_hi, h_lo)
    mx = jnp.max(logits, axis=0, keepdims=True)
    ex = jnp.exp(logits - mx)
    aff_ref[0] = ex / jnp.sum(ex, axis=0, keepdims=True)


def _mix(x, mla, dil, mod3, w_out, ln_g, ln_b, w_router_t, alpha, tm=1024):
    bsz, s, d = x.shape
    row = lambda w: pl.BlockSpec((1, tm, w), lambda b, i: (b, i, 0))
    modspec = lambda k: pl.BlockSpec((1, 1, d), lambda b, i: (6 * b + k, 0, 0))
    full = lambda a: pl.BlockSpec(a.shape, lambda b, i: (0,) * a.ndim)
    return pl.pallas_call(
        functools.partial(_mix_kernel, alpha=alpha),
        grid=(bsz, s // tm),
        in_specs=[row(d), row(MLA_WIDTH), row(DIL_WIDTH)]
                 + [modspec(2), modspec(4), modspec(3),
                    full(w_out), full(ln_g), full(ln_b), full(w_router_t)],
        out_specs=[row(d), pl.BlockSpec((1, tm, 1, d), lambda b, i: (b, i, 0, 0)),
                   pl.BlockSpec((1, N_EXPERTS, tm), lambda b, i: (b, 0, i))],
        out_shape=[jax.ShapeDtypeStruct((bsz, s, d), F32),
                   jax.ShapeDtypeStruct((bsz, s, 1, d), F32),
                   jax.ShapeDtypeStruct((bsz, N_EXPERTS, s), F32)],
        compiler_params=_params(("parallel", "arbitrary")),
        name="mix",
    )(x, mla, dil, mod3, mod3, mod3, w_out, ln_g, ln_b, w_router_t)


def _cumsum_lanes(x, tri):
    rows, n = x.shape
    carry = jnp.zeros((rows, 1), F32)
    parts = []
    for j in range(n // LANES):
        inc = _dot(x[:, j * LANES:(j + 1) * LANES].astype(BF16), tri) + carry
        parts.append(inc)
        carry = inc[:, LANES - 1:LANES]
    return jnp.concatenate(parts, axis=1)


TOPK_EXPERTS_PER_ITER = 2


def _topk_kernel(aff_ref, idx_ref, val_ref, key_ref, *, cap):
    a = aff_ref[0]
    n_e, s = a.shape
    t = jnp.zeros((n_e, 1), jnp.int32)
    for bit in range(30, -1, -1):
        cand = t | (1 << bit)
        cnt = jnp.sum((a >= pltpu.bitcast(cand, F32)).astype(jnp.int32), axis=1, keepdims=True)
        t = jnp.where(cnt >= cap, cand, t)
    thr = pltpu.bitcast(t, F32)
    gt = a > thr
    eq = a == thr
    n_gt = jnp.sum(gt.astype(jnp.int32), axis=1, keepdims=True)
    ri = lax.broadcasted_iota(jnp.int32, (LANES, LANES), 0)
    ci = lax.broadcasted_iota(jnp.int32, (LANES, LANES), 1)
    tri = jnp.where(ri <= ci, 1.0, 0.0).astype(BF16)
    eq_f = jnp.where(eq, 1.0, 0.0)
    rank_eq = _cumsum_lanes(eq_f, tri) - eq_f
    sel = gt | (eq & (rank_eq < (cap - n_gt).astype(F32)))
    sel_f = jnp.where(sel, 1.0, 0.0)
    pos = _cumsum_lanes(sel_f, tri) - sel_f
    key_ref[...] = jnp.where(sel, pos.astype(jnp.int32), -1)

    tok = lax.broadcasted_iota(jnp.int32, (1, s), 1)
    tok_hi = (tok >> 6).astype(F32)
    tok_lo = (tok & 63).astype(F32)
    slot = lax.broadcasted_iota(jnp.int32, (cap, 1), 0)
    rid = lax.broadcasted_iota(jnp.int32, (8, s), 0)

    def per_expert(e):
        key = key_ref[pl.ds(e, 1), :]
        onehot = jnp.where(key == slot, 1.0, 0.0).astype(BF16)
        ar = aff_ref[0, pl.ds(e, 1), :]
        a_hi = ar.astype(BF16).astype(F32)
        r1 = ar - a_hi
        a_mid = r1.astype(BF16).astype(F32)
        a_lo = r1 - a_mid
        lhs = jnp.where(rid == 0, tok_hi,
              jnp.where(rid == 1, tok_lo,
              jnp.where(rid == 2, a_hi,
              jnp.where(rid == 3, a_mid,
              jnp.where(rid == 4, a_lo, 0.0))))).astype(BF16)
        res = _dot_nt(lhs, onehot)
        idx_ref[0, pl.ds(e, 1), :] = (res[0:1] * 64.0 + res[1:2]).astype(jnp.int32)
        val_ref[0, pl.ds(e, 1), :] = res[2:3] + res[3:4] + res[4:5]

    def expert_group(g, carry):
        for u in range(TOPK_EXPERTS_PER_ITER):
            per_expert(g * TOPK_EXPERTS_PER_ITER + u)
        return carry

    lax.fori_loop(0, n_e // TOPK_EXPERTS_PER_ITER, expert_group, 0)


def _topk(aff, cap):
    bsz, n_e, s = aff.shape
    return pl.pallas_call(
        functools.partial(_topk_kernel, cap=cap),
        grid=(bsz,),
        in_specs=[pl.BlockSpec((1, n_e, s), lambda b: (b, 0, 0))],
        out_specs=[pl.BlockSpec((1, n_e, cap), lambda b: (b, 0, 0)),
                   pl.BlockSpec((1, n_e, cap), lambda b: (b, 0, 0))],
        out_shape=[jax.ShapeDtypeStruct((bsz, n_e, cap), jnp.int32),
                   jax.ShapeDtypeStruct((bsz, n_e, cap), F32)],
        scratch_shapes=[pltpu.VMEM((n_e, s), jnp.int32)],
        compiler_params=_params(("parallel",)),
        name="topk",
    )(aff)


GATHER_ROWS = 16


def _gather_kernel(idx_ref, h_ref, o_ref, rows_ref, *, cap):
    d = o_ref.shape[-1]
    pieces = d // LANES

    def chunk(g, carry):
        base = pl.multiple_of(g * GATHER_ROWS, GATHER_ROWS)
        for u in range(GATHER_ROWS):
            row = h_ref[0, idx_ref[0, 0, base + u]]
            rows_ref[pl.ds(pl.multiple_of((base + u) * pieces, pieces), pieces), :] = row.reshape(pieces, LANES)
        cols = [rows_ref[pl.ds(base * pieces + j, GATHER_ROWS, stride=pieces), :] for j in range(pieces)]
        o_ref[0, 0, pl.ds(base, GATHER_ROWS), :] = jnp.concatenate(cols, axis=1).astype(o_ref.dtype)
        return carry

    lax.fori_loop(0, cap // GATHER_ROWS, chunk, 0)


def _gather(idx3, h4, n_e, cap):
    bsz, s, _, d = h4.shape
    return pl.pallas_call(
        functools.partial(_gather_kernel, cap=cap),
        grid=(bsz, n_e),
        in_specs=[pl.BlockSpec((1, 1, cap), lambda b, e: (b * n_e + e, 0, 0),
                               memory_space=pltpu.SMEM),
                  pl.BlockSpec((1, s, 1, d), lambda b, e: (b, 0, 0, 0))],
        out_specs=pl.BlockSpec((1, 1, cap, d), lambda b, e: (b, e, 0, 0)),
        out_shape=jax.ShapeDtypeStruct((bsz, n_e, cap, d), BF16),
        scratch_shapes=[pltpu.VMEM((cap * (d // LANES), LANES), F32)],
        compiler_params=_params(("parallel", "arbitrary")),
        name="gather",
    )(idx3, h4)


def _ffn_kernel(x_ref, val_ref, wg_hbm, wu_hbm, wd_hbm, y_ref,
                wg_bf, wu_bf, wd_bf, stage_g, stage_u, stage_d, sems):
    e = pl.program_id(0)
    b = pl.program_id(1)
    n_e = pl.num_programs(0)
    n_slices = pl.num_programs(1)
    rows_g = stage_g.shape[0]
    rows_d = stage_d.shape[0]
    slot = e % 2

    def slice_copies(expert, i):
        return (pltpu.make_async_copy(wg_hbm.at[expert, pl.ds(i * rows_g, rows_g), :], stage_g, sems.at[0]),
                pltpu.make_async_copy(wu_hbm.at[expert, pl.ds(i * rows_g, rows_g), :], stage_u, sems.at[1]),
                pltpu.make_async_copy(wd_hbm.at[expert, pl.ds(i * rows_d, rows_d), :], stage_d, sems.at[2]))

    def cast_slice(dst_slot, i):
        wg_bf[dst_slot, pl.ds(pl.multiple_of(i * rows_g, 16), rows_g), :] = stage_g[...].astype(BF16)
        wu_bf[dst_slot, pl.ds(pl.multiple_of(i * rows_g, 16), rows_g), :] = stage_u[...].astype(BF16)
        wd_bf[dst_slot, pl.ds(pl.multiple_of(i * rows_d, 16), rows_d), :] = stage_d[...].astype(BF16)

    def fetched_expert(expert):
        return jnp.minimum(expert + 1, n_e - 1)

    @pl.when(jnp.logical_and(e == 0, b == 0))
    def _():
        def load(i, carry):
            cps = slice_copies(0, i)
            for cp in cps:
                cp.start()
            for cp in cps:
                cp.wait()
            cast_slice(0, i)
            return carry
        lax.fori_loop(0, n_slices, load, 0)
        for cp in slice_copies(fetched_expert(0), 0):
            cp.start()

    x = x_ref[0, 0]
    gt = _dot(x, wg_bf[slot])
    up = _dot(x, wu_bf[slot])
    act = (gt * (1.0 / (1.0 + jnp.exp(-gt))) * up).astype(BF16)
    y = _dot(act, wd_bf[slot]) * val_ref[0, 0]
    y_ref[0, 0] = y.reshape(y.shape[0], 1, y.shape[1])

    for cp in slice_copies(fetched_expert(e), b):
        cp.wait()
    cast_slice(1 - slot, b)

    @pl.when(jnp.logical_or(e + 1 < n_e, b + 1 < n_slices))
    def _():
        wrap = b + 1 == n_slices
        e2 = jnp.where(wrap, e + 1, e)
        b2 = jnp.where(wrap, 0, b + 1)
        for cp in slice_copies(fetched_expert(e2), b2):
            cp.start()


def _ffn(xin, vals4, wg, wu, wd):
    bsz, n_e, cap, d = xin.shape
    f = wg.shape[-1]
    assert d % (16 * bsz) == 0 and f % (16 * bsz) == 0
    hbm = pl.BlockSpec(memory_space=pl.ANY)
    return pl.pallas_call(
        _ffn_kernel,
        grid=(n_e, bsz),
        in_specs=[pl.BlockSpec((1, 1, cap, d), lambda e, b: (b, e, 0, 0)),
                  pl.BlockSpec((1, 1, cap, 1), lambda e, b: (b, e, 0, 0)), hbm, hbm, hbm],
        out_specs=pl.BlockSpec((1, 1, cap, 1, d), lambda e, b: (b, e, 0, 0, 0)),
        out_shape=jax.ShapeDtypeStruct((bsz, n_e, cap, 1, d), F32),
        scratch_shapes=[pltpu.VMEM((2, d, f), BF16), pltpu.VMEM((2, d, f), BF16), pltpu.VMEM((2, f, d), BF16),
                        pltpu.VMEM((d // bsz, f), F32), pltpu.VMEM((d // bsz, f), F32),
                        pltpu.VMEM((f // bsz, d), F32), pltpu.SemaphoreType.DMA((3,))],
        compiler_params=_params(("arbitrary", "arbitrary")),
        name="ffn",
    )(xin, vals4, wg, wu, wd)


SCATTER_UNROLL = 16


def _combine_kernel(idx_ref, y_ref, o_ref, *, cap):
    @pl.when(pl.program_id(1) == 0)
    def _():
        o_ref[...] = jnp.zeros_like(o_ref)

    def chunk(g, carry):
        base = g * SCATTER_UNROLL
        rows = []
        for u in range(SCATTER_UNROLL):
            c = base + u
            t = idx_ref[0, 0, c]
            rows.append((t, o_ref[0, t] + y_ref[0, 0, c]))
        for t, r in rows:
            o_ref[0, t] = r
        return carry

    lax.fori_loop(0, cap // SCATTER_UNROLL, chunk, 0)


def _combine(idx3, y5, s):
    bsz, n_e, cap, _, d = y5.shape
    return pl.pallas_call(
        functools.partial(_combine_kernel, cap=cap),
        grid=(bsz, n_e),
        in_specs=[pl.BlockSpec((1, 1, cap), lambda b, e: (b * n_e + e, 0, 0),
                               memory_space=pltpu.SMEM),
                  pl.BlockSpec((1, 1, cap, 1, d), lambda b, e: (b, e, 0, 0, 0))],
        out_specs=pl.BlockSpec((1, s, 1, d), lambda b, e: (b, 0, 0, 0)),
        out_shape=jax.ShapeDtypeStruct((bsz, s, 1, d), F32),
        compiler_params=_params(("parallel", "arbitrary")),
        name="combine",
    )(idx3, y5)


FINAL_ROWS = 8


def _final_kernel(x1_ref, moe_ref, g2_ref, lg_ref, lb_ref, o_ref, moe2d_ref, rows_ref, *, alpha):
    tm, d = moe2d_ref.shape
    pieces = d // LANES

    def retile(g, carry):
        r0 = pl.multiple_of(g * FINAL_ROWS, FINAL_ROWS)
        for u in range(FINAL_ROWS):
            rows_ref[pl.ds(u * pieces, pieces), :] = moe_ref[0, r0 + u].reshape(pieces, LANES)
        cols = [rows_ref[pl.ds(j, FINAL_ROWS, stride=pieces), :] for j in range(pieces)]
        moe2d_ref[pl.ds(r0, FINAL_ROWS), :] = jnp.concatenate(cols, axis=1)
        return carry

    lax.fori_loop(0, tm // FINAL_ROWS, retile, 0)
    y = alpha * x1_ref[0] + g2_ref[0] * moe2d_ref[...]
    o_ref[0] = _layer_norm(y, lg_ref[...], lb_ref[...])


def _final(x1, moe4, mod3, ln_g, ln_b, alpha, tm=512):
    bsz, s, d = x1.shape
    row = pl.BlockSpec((1, tm, d), lambda b, i: (b, i, 0))
    vec = pl.BlockSpec((1, d), lambda b, i: (0, 0))
    return pl.pallas_call(
        functools.partial(_final_kernel, alpha=alpha),
        grid=(bsz, s // tm),
        in_specs=[row, pl.BlockSpec((1, tm, 1, d), lambda b, i: (b, i, 0, 0)),
                  pl.BlockSpec((1, 1, d), lambda b, i: (6 * b + 5, 0, 0)), vec, vec],
        out_specs=row,
        out_shape=jax.ShapeDtypeStruct((bsz, s, d), F32),
        scratch_shapes=[pltpu.VMEM((tm, d), F32), pltpu.VMEM((FINAL_ROWS * (d // LANES), LANES), F32)],
        compiler_params=_params(("parallel", "arbitrary")),
        name="final",
    )(x1, moe4, mod3, ln_g, ln_b)


def _rope_tables(s):
    inv = ROPE_THETA ** (-jnp.arange(0, MLA_ROPE, 2, dtype=F32) / MLA_ROPE)
    ang = jnp.arange(s, dtype=F32)[:, None] * inv[None, :]
    cos, sin = jnp.cos(ang), jnp.sin(ang)
    scale = (MLA_NOPE + MLA_ROPE) ** -0.5 * math.log2(math.e)
    z32 = jnp.zeros((s, HEAD_SLAB - MLA_NOPE - MLA_ROPE), F32)
    z64 = jnp.zeros((s, MLA_NOPE), F32)
    cosq = jnp.concatenate([jnp.full((s, MLA_NOPE), scale, F32), cos * scale, cos * scale, z32], 1)
    sinq_scaled = jnp.concatenate([z64, -sin * scale, sin * scale, z32], 1)
    cosk = jnp.concatenate([z64, cos, cos, z32], 1)
    sink = jnp.concatenate([z64, sin, sin, z32], 1)
    return cosq, sinq_scaled, cosk, sink


def _layout_weights(w_in, w_uq, w_ukv):
    d = w_in.shape[0]
    c0 = MLA_Q_LORA + MLA_KV_LORA
    kr = w_in[:, c0:c0 + MLA_ROPE]
    t1, t2 = kr[:, :HALF_ROPE], kr[:, HALF_ROPE:]
    z = lambda n: jnp.zeros((d, n), w_in.dtype)
    pad = HEAD_SLAB - MLA_NOPE - MLA_ROPE
    w_in_r = jnp.concatenate(
        [w_in[:, :c0], w_in[:, c0 + MLA_ROPE:],
         z(MLA_NOPE), t1, t2, z(pad),
         z(MLA_NOPE), -t2, t1, z(pad)], axis=1).astype(BF16)

    ql = w_uq.shape[0]
    wq = w_uq.reshape(ql, MLA_HEADS, MLA_NOPE + MLA_ROPE)
    qn, q1, q2 = wq[..., :MLA_NOPE], wq[..., MLA_NOPE:MLA_NOPE + HALF_ROPE], wq[..., MLA_NOPE + HALF_ROPE:]
    zq = lambda n: jnp.zeros((ql, MLA_HEADS, n), w_uq.dtype)
    wq_pre = jnp.concatenate([qn, q1, q2, zq(pad)], -1).reshape(ql, -1).astype(BF16)

    kl = w_ukv.shape[0]
    wkv = w_ukv.reshape(kl, MLA_HEADS, MLA_NOPE + MLA_V)
    wk = jnp.concatenate([wkv[..., :MLA_NOPE],
                          jnp.zeros((kl, MLA_HEADS, HEAD_SLAB - MLA_NOPE), w_ukv.dtype)],
                         -1).reshape(kl, -1).astype(BF16)
    wv = jnp.concatenate([wkv[..., MLA_NOPE:],
                          jnp.zeros((kl, MLA_HEADS, HEAD_SLAB - MLA_V), w_ukv.dtype)],
                         -1).reshape(kl, -1).T.astype(BF16)
    return w_in_r, wq_pre, wk, wv


def kernel(x, c, w_ada, b_ada, w_in, q_norm_g, w_uq, kv_norm_g, w_ukv, rel_bias, w_out, ln1_g,
           ln1_b, w_router, w_gate, w_up, w_down, ln2_g, ln2_b):
    bsz, s, d = x.shape
    depth = w_ada.shape[0]
    alpha = (2 * depth) ** 0.25
    cap = max(1, EC_CAPACITY_FACTOR * s // N_EXPERTS)
    assert s % (max(2 * Q_TILE_DIL, DIL_COPY_ROWS) * max(dl for _, dl in DIL_PATTERNS)) == 0
    assert all(win // 2 // dl == BAND_HALF for win, dl in DIL_PATTERNS)

    cosq, sinq, cosk, sink = _rope_tables(s)
    bias = _bias_tiles(rel_bias)

    for l in range(depth):
        mod3 = _ada(c, w_ada[l], b_ada[l]).reshape(bsz * 6, 1, d)
        w_in_r, wq_pre, wk, wv = _layout_weights(w_in[l], w_uq[l], w_ukv[l])
        q, k, v, dq, dk, dv = _proj(
            x, mod3, w_in_r, q_norm_g[l].reshape(1, -1), wq_pre,
            kv_norm_g[l].reshape(1, -1), wk, wv, cosq, sinq, cosk, sink)
        mla = _mla(q, k, v)
        dil_out = _dilated(dq, dk, dv, bias)
        x1, h2, aff = _mix(x, mla, dil_out, mod3, w_out[l].astype(BF16),
                           ln1_g[l].reshape(1, d), ln1_b[l].reshape(1, d),
                           w_router[l].T, alpha)
        idx, vals = _topk(aff, cap)
        idx3 = idx.reshape(bsz * N_EXPERTS, 1, cap)
        xin = _gather(idx3, h2, N_EXPERTS, cap)
        y = _ffn(xin, vals.reshape(bsz, N_EXPERTS, cap, 1), w_gate[l], w_up[l], w_down[l])
        moe = _combine(idx3, y, s)
        x = _final(x1, moe, mod3, ln2_g[l].reshape(1, d),
                   ln2_b[l].reshape(1, d), alpha)
    return x
```
